```python
import jax, jax.numpy as jnp
from jax import lax
import numpy as np

D_MODEL = 1024
BATCH = 8
SEQ = 8192
DEPTH = 4
DEC_BATCH = 8
DEC_SEQ = 2048
PAST_LEN = 128

GRID_W = 64
HEAD_DIM = 64
N_HEADS_A = 8
N_KV_A = 2
N_HEADS_B = 8
D_NOPE = 64
D_ROPE = 32
D_V = 64
D_CQ = 384
D_CKV = 256
N_FGROUPS = 4
FGROUP = D_MODEL // N_FGROUPS
Q_BLOCK = 128
ROPE_THETA = 10000.0
N_EXPERTS = 16
N_EGROUPS = 4
EXPERTS_PER_GROUP = N_EXPERTS // N_EGROUPS
TOP_K = 2
D_EXPERT = 256
EPS = 1e-6

MIX_WIDTH = N_HEADS_A * HEAD_DIM + N_HEADS_B * D_V
IN_SIZES = (N_HEADS_A * HEAD_DIM, N_KV_A * HEAD_DIM, N_KV_A * HEAD_DIM, D_CQ, D_CKV, D_ROPE)
IN_WIDTH = sum(IN_SIZES)

kernel_name = "hybrid_gqa_mla_fourier_moe_encoder"


def rms_norm(x, g):
    xf = x.astype(jnp.float32)
    y = xf * lax.rsqrt(jnp.mean(xf * xf, axis=-1, keepdims=True) + EPS)
    return (y * g.astype(jnp.float32)).astype(x.dtype)


def grid_positions(n_tokens):
    n_rows = n_tokens // GRID_W
    rows = jnp.repeat(jnp.arange(n_rows, dtype=jnp.int32), GRID_W)
    cols = jnp.tile(jnp.arange(GRID_W, dtype=jnp.int32), n_rows)
    return rows, cols


def _rope_1d(x, pos):
    h = x.shape[-1]
    inv = ROPE_THETA ** (-jnp.arange(0, h, 2, dtype=jnp.float32) / h)
    ang = pos.astype(jnp.float32)[:, None] * inv[None, :]
    cos = jnp.cos(ang)[None, :, None, :]
    sin = jnp.sin(ang)[None, :, None, :]
    x1, x2 = jnp.split(x.astype(jnp.float32), 2, axis=-1)
    return jnp.concatenate([x1 * cos - x2 * sin, x1 * sin + x2 * cos], axis=-1).astype(x.dtype)


def axial_rope(x, rows, cols):
    d2 = x.shape[-1] // 2
    return jnp.concatenate([_rope_1d(x[..., :d2], rows), _rope_1d(x[..., d2:], cols)], axis=-1)


def block_attention(q, k, v, scale):
    B, S, Hk, G, dq = q.shape
    nb = S // Q_BLOCK
    qb = q.reshape(B, nb, Q_BLOCK, Hk, G, dq).transpose(1, 0, 2, 3, 4, 5)

    def one_block(qi):
        s = jnp.einsum('bqhgd,bkhd->bhgqk', qi, k, preferred_element_type=jnp.float32) * scale
        p = jax.nn.softmax(s, axis=-1)
        return jnp.einsum('bhgqk,bkhd->bqhgd', p.astype(v.dtype), v)

    out = lax.map(one_block, qb)
    return out.transpose(1, 0, 2, 3, 4, 5).reshape(B, S, Hk * G, v.shape[-1])


def attention_mixer(h, rows, cols, w_in, g_qa, g_ka, g_cq, w_uq, g_ckv, w_ukv, w_o):
    B, S, _ = h.shape
    z = h @ w_in
    split_idx = np.cumsum(IN_SIZES)[:-1].tolist()
    qa, ka, va, cq, ckv, kr = jnp.split(z, split_idx, axis=-1)
    qa = axial_rope(rms_norm(qa.reshape(B, S, N_HEADS_A, HEAD_DIM), g_qa), rows, cols)
    ka = axial_rope(rms_norm(ka.reshape(B, S, N_KV_A, HEAD_DIM), g_ka), rows, cols)
    va = va.reshape(B, S, N_KV_A, HEAD_DIM)
    qa = qa.reshape(B, S, N_KV_A, N_HEADS_A // N_KV_A, HEAD_DIM)
    oa = block_attention(qa, ka, va, HEAD_DIM ** -0.5)
    qb = (rms_norm(cq, g_cq) @ w_uq).reshape(B, S, N_HEADS_B, D_NOPE + D_ROPE)
    q_nope, q_rope = qb[..., :D_NOPE], axial_rope(qb[..., D_NOPE:], rows, cols)
    kv = (rms_norm(ckv, g_ckv) @ w_ukv).reshape(B, S, N_HEADS_B, D_NOPE + D_V)
    k_nope, vb = kv[..., :D_NOPE], kv[..., D_NOPE:]
    k_rope = axial_rope(kr.reshape(B, S, 1, D_ROPE), rows, cols)
    qb = jnp.concatenate([q_nope, q_rope], axis=-1)[:, :, :, None, :]
    kb = jnp.concatenate([k_nope, jnp.broadcast_to(k_rope, (B, S, N_HEADS_B, D_ROPE))], axis=-1)
    ob = block_attention(qb, kb, vb, (D_NOPE + D_ROPE) ** -0.5)
    o = jnp.concatenate([oa.reshape(B, S, -1), ob.reshape(B, S, -1)], axis=-1)
    return o @ w_o


def fourier_mixer(h, w_c):
    B, S, D = h.shape
    hg = h.astype(jnp.float32).reshape(B, S, N_FGROUPS, FGROUP)
    f = jnp.fft.fft2(hg, axes=(1, 3), norm='ortho').real
    return f.reshape(B, S, D).astype(h.dtype) @ w_c


def moe(h, w_router, router_bias, w_gate, w_up, w_down):
    B, S, D = h.shape
    t = h.reshape(-1, D)
    T = t.shape[0]
    s = jax.nn.sigmoid((t @ w_router).astype(jnp.float32))
    sb = s + router_bias.astype(jnp.float32)
    grp_score = lax.top_k(sb.reshape(T, N_EGROUPS, EXPERTS_PER_GROUP), 2)[0].sum(-1)
    gsel = jnp.argmax(grp_score, axis=-1)
    in_grp = (jnp.arange(N_EXPERTS) // EXPERTS_PER_GROUP)[None, :] == gsel[:, None]
    _, idx = lax.top_k(jnp.where(in_grp, sb, -jnp.inf), TOP_K)
    w = jnp.take_along_axis(s, idx, axis=-1)
    w = w / jnp.sum(w, axis=-1, keepdims=True)
    gates = jnp.sum(jax.nn.one_hot(idx, N_EXPERTS, dtype=jnp.float32) * w[..., None], axis=1)
    gates = gates.astype(t.dtype)
    y = jnp.zeros_like(t)
    for e in range(N_EXPERTS):
        he = jax.nn.silu(t @ w_gate[e]) * (t @ w_up[e])
        y = y + gates[:, e:e + 1] * (he @ w_down[e])
    return y.reshape(B, S, D)


def encoder_trunk(x, c, w_ada, b_ada, g_mix, g_ffn, w_in, g_qa, g_ka, g_cq, w_uq, g_ckv, w_ukv,
                  w_o_attn, w_fourier, w_router, router_bias, w_gate, w_up, w_down, g_final):
    rows, cols = grid_positions(x.shape[1])
    c_act = jax.nn.silu(c)
    for l in range(DEPTH):
        mod = (c_act @ w_ada[l] + b_ada[l])[:, None, :]
        sh1, sc1, gt1, sh2, sc2, gt2 = jnp.split(mod, 6, axis=-1)
        h = rms_norm(x, g_mix[l]) * (1 + sc1) + sh1
        if l % 2 == 0:
            i = l // 2
            m = attention_mixer(h, rows, cols, w_in[i], g_qa[i], g_ka[i], g_cq[i], w_uq[i],
                                g_ckv[i], w_ukv[i], w_o_attn[i])
        else:
            m = fourier_mixer(h, w_fourier[l // 2])
        x = x + gt1 * m
        h = rms_norm(x, g_ffn[l]) * (1 + sc2) + sh2
        x = x + gt2 * moe(h, w_router, router_bias, w_gate[l], w_up[l], w_down[l])
    return rms_norm(x, g_final)


def setup_inputs(seed: int = 0) -> dict:
    key = jax.random.key(seed)
    ks = jax.random.split(key, 24)
    n_a = (DEPTH + 1) // 2
    n_c = DEPTH // 2

    def nrm(k, shape, scale):
        return jax.random.normal(k, shape, jnp.float32) * scale

    def gain(k, shape):
        return 1.0 + 0.02 * jax.random.normal(k, shape, jnp.float32)

    return {
        "x_prompt": nrm(ks[0], (BATCH, SEQ, D_MODEL), 1.0),
        "x_sample": nrm(ks[1], (DEC_BATCH, DEC_SEQ, D_MODEL), 1.0),
        "c_prompt": nrm(ks[2], (BATCH, D_MODEL), 1.0),
        "c_sample": nrm(ks[3], (DEC_BATCH, D_MODEL), 1.0),
        "w_ada": nrm(ks[4], (DEPTH, D_MODEL, 6 * D_MODEL), 0.5 * D_MODEL ** -0.5),
        "b_ada": nrm(ks[5], (DEPTH, 6 * D_MODEL), 0.02),
        "g_mix": gain(ks[6], (DEPTH, D_MODEL)),
        "g_ffn": gain(ks[7], (DEPTH, D_MODEL)),
        "w_in": nrm(ks[8], (n_a, D_MODEL, IN_WIDTH), D_MODEL ** -0.5),
        "g_qa": gain(ks[9], (n_a, HEAD_DIM)),
        "g_ka": gain(ks[10], (n_a, HEAD_DIM)),
        "g_cq": gain(ks[11], (n_a, D_CQ)),
        "w_uq": nrm(ks[12], (n_a, D_CQ, N_HEADS_B * (D_NOPE + D_ROPE)), D_CQ ** -0.5),
        "g_ckv": gain(ks[13], (n_a, D_CKV)),
        "w_ukv": nrm(ks[14], (n_a, D_CKV, N_HEADS_B * (D_NOPE + D_V)), D_CKV ** -0.5),
        "w_o_attn": nrm(ks[15], (n_a, MIX_WIDTH, D_MODEL), MIX_WIDTH ** -0.5),
        "w_fourier": nrm(ks[16], (n_c, D_MODEL, D_MODEL), D_MODEL ** -0.5),
        "w_router": nrm(ks[17], (D_MODEL, N_EXPERTS), D_MODEL ** -0.5),
        "router_bias": nrm(ks[18], (N_EXPERTS,), 0.01),
        "w_gate": nrm(ks[19], (DEPTH, N_EXPERTS, D_MODEL, D_EXPERT), D_MODEL ** -0.5),
        "w_up": nrm(ks[20], (DEPTH, N_EXPERTS, D_MODEL, D_EXPERT), D_MODEL ** -0.5),
        "w_down": nrm(ks[21], (DEPTH, N_EXPERTS, D_EXPERT, D_MODEL), D_EXPERT ** -0.5),
        "g_final": gain(ks[22], (D_MODEL,)),
    }


def reference(x_prompt, x_sample, c_prompt, c_sample, w_ada, b_ada, g_mix, g_ffn, w_in, g_qa, g_ka,
              g_cq, w_uq, g_ckv, w_ukv, w_o_attn, w_fourier, w_router, router_bias, w_gate, w_up,
              w_down, g_final):
    y_prompt = encoder_trunk(x_prompt, c_prompt, w_ada, b_ada, g_mix, g_ffn, w_in, g_qa, g_ka, g_cq,
                             w_uq, g_ckv, w_ukv, w_o_attn, w_fourier, w_router, router_bias,
                             w_gate, w_up, w_down, g_final)
    y_sample = encoder_trunk(x_sample, c_sample, w_ada, b_ada, g_mix, g_ffn, w_in, g_qa, g_ka, g_cq,
                             w_uq, g_ckv, w_ukv, w_o_attn, w_fourier, w_router, router_bias,
                             w_gate, w_up, w_down, g_final)
    return (y_prompt, y_sample)
```

```python
import functools
import math

import numpy as np
import jax
import jax.numpy as jnp
from jax import lax
from jax.experimental import pallas as pl
from jax.experimental.pallas import tpu as pltpu

F32 = jnp.float32
BF16 = jnp.bfloat16

D_MODEL = 1024
DEPTH = 4
GRID_W = 64
HEAD_DIM = 64
N_HEADS_A = 8
N_KV_A = 2
N_HEADS_B = 8
D_NOPE = 64
D_ROPE = 32
D_V = 64
D_CQ = 384
D_CKV = 256
N_FGROUPS = 4
FGROUP = D_MODEL // N_FGROUPS
ROPE_THETA = 10000.0
N_EXPERTS = 16
N_EGROUPS = 4
EXPERTS_PER_GROUP = 4
D_EXPERT = 256
EPS = 1e-6
IN_SIZES = (N_HEADS_A * HEAD_DIM, N_KV_A * HEAD_DIM, N_KV_A * HEAD_DIM, D_CQ, D_CKV, D_ROPE)
IN_WIDTH = sum(IN_SIZES)
IN_OFFS = tuple(int(v) for v in np.cumsum((0,) + IN_SIZES))

N_HEADS = N_HEADS_A + N_HEADS_B
N_KSLOTS = 1 + N_HEADS_B
N_VSLOTS = N_KV_A + N_HEADS_B
QK_PAD = 128
LOG2E = 1.4426950408889634
SCALE_A = HEAD_DIM ** -0.5 * LOG2E
SCALE_B = (D_NOPE + D_ROPE) ** -0.5 * LOG2E
NEG_BIG = -1e30

FFT_N2 = 128
FFT_T = 8

TOKEN_TILE = 256
Q_TILE = 256
MOE_TILE = 512
VMEM_LIMIT = 56 * 1024 * 1024


def _cparams(sem):
    return pltpu.CompilerParams(dimension_semantics=sem, vmem_limit_bytes=VMEM_LIMIT)


def _sigmoid(x):
    return 1.0 / (1.0 + jnp.exp(-x))


def _rms_rows(x, g):
    ms = jnp.mean(x * x, axis=-1, keepdims=True)
    return x * lax.rsqrt(ms + EPS) * g


def _rms_cols(xt, gcol):
    ms = jnp.mean(xt * xt, axis=0, keepdims=True)
    return xt * lax.rsqrt(ms + EPS) * gcol


def _dot(a, b):
    return jnp.dot(a, b, preferred_element_type=F32)


def _dot_nt(a, b):
    return lax.dot_general(a, b, (((1,), (1,)), ((), ())), preferred_element_type=F32)


def _dot_tn(a, b):
    return lax.dot_general(a, b, (((0,), (0,)), ((), ())), preferred_element_type=F32)


def _mod_kernel(c_ref, w_ref, b_ref, o_ref):
    c = c_ref[...]
    ca = c * _sigmoid(c)
    o_ref[...] = _dot(ca, w_ref[...]) + b_ref[...]


def _modulation(c_all, w_ada, b_ada):
    nb = c_all.shape[0]
    tn = 1536
    return pl.pallas_call(
        _mod_kernel,
        out_shape=jax.ShapeDtypeStruct((DEPTH, nb, 6 * D_MODEL), F32),
        grid=(DEPTH, 6 * D_MODEL // tn),
        in_specs=[
            pl.BlockSpec((nb, D_MODEL), lambda l, j: (0, 0)),
            pl.BlockSpec((None, D_MODEL, tn), lambda l, j: (l, 0, j)),
            pl.BlockSpec((None, 1, tn), lambda l, j: (l, 0, j)),
        ],
        out_specs=pl.BlockSpec((None, nb, tn), lambda l, j: (l, 0, j)),
        compiler_params=_cparams(("arbitrary", "arbitrary")),
        name="adaln_mod",
    )(c_all, w_ada, b_ada.reshape(DEPTH, 1, 6 * D_MODEL))


def _rope_t(xt, c, s):
    n = c.shape[0]
    x1, x2 = xt[:n], xt[n:]
    return jnp.concatenate([x1 * c - x2 * s, x1 * s + x2 * c], axis=0)


def _axial_t(xt, cr, sr, cc, sc):
    d2 = xt.shape[0] // 2
    return jnp.concatenate([_rope_t(xt[:d2], cr, sr), _rope_t(xt[d2:], cc, sc)], axis=0)


def _pre_attn_kernel(x_ref, mod_ref, gmix_ref, tab_ref, wint_ref, gqa_ref, gka_ref, gcq_ref,
                     wuqt_ref, gckv_ref, wukvt_ref, q_ref, k_ref, v_ref):
    x = x_ref[...]
    tm = x.shape[0]
    mod = mod_ref[...]
    h = _rms_rows(x, gmix_ref[...]) * (1.0 + mod[1:2]) + mod[0:1]
    zt = _dot_nt(wint_ref[...], h.astype(BF16))

    tab = tab_ref[...]
    ra = (tab[0:16], tab[16:32], tab[32:48], tab[48:64])
    rb = (tab[64:72], tab[72:80], tab[80:88], tab[88:96])
    zeros64 = jnp.zeros((HEAD_DIM, tm), F32)
    zeros32 = jnp.zeros((QK_PAD - D_NOPE - D_ROPE, tm), F32)

    o_q, o_k, o_v, o_cq, o_ckv, o_kr = IN_OFFS[:6]
    gqa = gqa_ref[...]
    for hh in range(N_HEADS_A):
        qh = _axial_t(_rms_cols(zt[o_q + HEAD_DIM * hh:o_q + HEAD_DIM * (hh + 1)], gqa), *ra) * SCALE_A
        parts = [qh, zeros64] if hh // (N_HEADS_A // N_KV_A) == 0 else [zeros64, qh]
        q_ref[hh] = jnp.concatenate(parts, axis=0).astype(BF16)
    gka = gka_ref[...]
    kts = [jnp.concatenate(
        [_axial_t(_rms_cols(zt[o_k + HEAD_DIM * g:o_k + HEAD_DIM * (g + 1)], gka), *ra)
         for g in range(N_KV_A)], axis=0)]
    for g in range(N_KV_A):
        v_ref[g] = zt[o_v + HEAD_DIM * g:o_v + HEAD_DIM * (g + 1)].astype(BF16)

    cqn = _rms_cols(zt[o_cq:o_ckv], gcq_ref[...]).astype(BF16)
    qbt = _dot(wuqt_ref[...], cqn)
    dq = D_NOPE + D_ROPE
    for hh in range(N_HEADS_B):
        nope = qbt[dq * hh:dq * hh + D_NOPE]
        rp = _axial_t(qbt[dq * hh + D_NOPE:dq * (hh + 1)], *rb)
        q_ref[N_HEADS_A + hh] = (jnp.concatenate([nope, rp, zeros32], axis=0) * SCALE_B).astype(BF16)
    ckvn = _rms_cols(zt[o_ckv:o_kr], gckv_ref[...]).astype(BF16)
    kvt = _dot(wukvt_ref[...], ckvn)
    kr = _axial_t(zt[o_kr:o_kr + D_ROPE], *rb)
    dkv = D_NOPE + D_V
    for hh in range(N_HEADS_B):
        kts.append(jnp.concatenate([kvt[dkv * hh:dkv * hh + D_NOPE], kr, zeros32], axis=0))
        v_ref[N_KV_A + hh] = kvt[dkv * hh + D_NOPE:dkv * (hh + 1)].astype(BF16)

    k_all = jnp.concatenate(kts, axis=0).T
    for i in range(N_KSLOTS):
        k_ref[i] = k_all[:, QK_PAD * i:QK_PAD * (i + 1)].astype(BF16)


def _pre_attn(x, mod_l, g_mix, tab, wint, gqa, gka, gcq, wuqt, gckv, wukvt):
    B, S, _ = x.shape
    tm = TOKEN_TILE
    nt = S // tm
    const = lambda b, i: (0, 0)
    return pl.pallas_call(
        _pre_attn_kernel,
        out_shape=(
            jax.ShapeDtypeStruct((B, N_HEADS, QK_PAD, S), BF16),
            jax.ShapeDtypeStruct((B, N_KSLOTS, S, QK_PAD), BF16),
            jax.ShapeDtypeStruct((B, N_VSLOTS, nt, D_V, tm), BF16),
        ),
        grid=(B, nt),
        in_specs=[
            pl.BlockSpec((None, tm, D_MODEL), lambda b, i: (b, i, 0)),
            pl.BlockSpec((None, 6, D_MODEL), lambda b, i: (b, 0, 0)),
            pl.BlockSpec((1, D_MODEL), const),
            pl.BlockSpec((96, tm), lambda b, i: (0, i)),
            pl.BlockSpec((IN_WIDTH, D_MODEL), const),
            pl.BlockSpec((HEAD_DIM, 1), const),
            pl.BlockSpec((HEAD_DIM, 1), const),
            pl.BlockSpec((D_CQ, 1), const),
            pl.BlockSpec((N_HEADS_B * (D_NOPE + D_ROPE), D_CQ), const),
            pl.BlockSpec((D_CKV, 1), const),
            pl.BlockSpec((N_HEADS_B * (D_NOPE + D_V), D_CKV), const),
        ],
        out_specs=(
            pl.BlockSpec((None, N_HEADS, QK_PAD, tm), lambda b, i: (b, 0, 0, i)),
            pl.BlockSpec((None, N_KSLOTS, tm, QK_PAD), lambda b, i: (b, 0, i, 0)),
            pl.BlockSpec((None, N_VSLOTS, None, D_V, tm), lambda b, i: (b, 0, i, 0, 0)),
        ),
        compiler_params=_cparams(("parallel", "parallel")),
        name="pre_attn",
    )(x, mod_l, g_mix, tab, wint, gqa, gka, gcq, wuqt, gckv, wukvt)


def _attn_kernel(q_ref, k_ref, v_ref, o_ref, *, n_chunks, tk):
    qt = q_ref[...]
    tq = qt.shape[1]

    def body(c, carry):
        m, l, acc = carry
        off = pl.multiple_of(c * tk, tk)
        s = _dot(k_ref[pl.ds(off, tk), :], qt)
        m_new = jnp.maximum(m, jnp.max(s, axis=0, keepdims=True))
        p = jnp.exp2(s - m_new)
        alpha = jnp.exp2(m - m_new)
        l = alpha * l + jnp.sum(p, axis=0, keepdims=True)
        acc = alpha * acc + _dot(v_ref[c], p.astype(BF16))
        return m_new, l, acc

    m0 = jnp.full((1, tq), NEG_BIG, F32)
    l0 = jnp.zeros((1, tq), F32)
    acc0 = jnp.zeros((D_V, tq), F32)
    _, l, acc = lax.fori_loop(0, n_chunks, body, (m0, l0, acc0))
    o_ref[...] = (acc * (1.0 / l)).astype(o_ref.dtype)


def _attention(qt, k, vt):
    B, _, _, S = qt.shape
    n_chunks, tk = vt.shape[2], vt.shape[4]
    tq = Q_TILE
    kslot = lambda h: jnp.where(h < N_HEADS_A, 0, h - (N_HEADS_A - 1))
    vslot = lambda h: jnp.where(h < N_HEADS_A, h // (N_HEADS_A // N_KV_A), h - (N_HEADS_A - N_KV_A))
    return pl.pallas_call(
        functools.partial(_attn_kernel, n_chunks=n_chunks, tk=tk),
        out_shape=jax.ShapeDtypeStruct((B, N_HEADS, D_V, S), BF16),
        grid=(B, N_HEADS, S // tq),
        in_specs=[
            pl.BlockSpec((None, None, QK_PAD, tq), lambda b, h, i: (b, h, 0, i)),
            pl.BlockSpec((None, None, S, QK_PAD), lambda b, h, i: (b, kslot(h), 0, 0)),
            pl.BlockSpec((None, None, n_chunks, D_V, tk), lambda b, h, i: (b, vslot(h), 0, 0, 0)),
        ],
        out_specs=pl.BlockSpec((None, None, D_V, tq), lambda b, h, i: (b, h, 0, i)),
        compiler_params=_cparams(("parallel", "parallel", "arbitrary")),
        name="attn_sweep",
    )(qt, k, vt)


def _post_attn_kernel(o_ref, x_ref, mod_ref, wo_ref, out_ref):
    tm = x_ref.shape[0]
    ot = o_ref[...].reshape(N_HEADS * D_V, tm)
    m = _dot_tn(ot, wo_ref[...])
    out_ref[...] = x_ref[...] + mod_ref[...][2:3] * m


def _post_attn(ot, x, mod_l, wo):
    B, S, _ = x.shape
    tm = TOKEN_TILE
    return pl.pallas_call(
        _post_attn_kernel,
        out_shape=jax.ShapeDtypeStruct(x.shape, F32),
        grid=(B, S // tm),
        in_specs=[
            pl.BlockSpec((None, N_HEADS, D_V, tm), lambda b, i: (b, 0, 0, i)),
            pl.BlockSpec((None, tm, D_MODEL), lambda b, i: (b, i, 0)),
            pl.BlockSpec((None, 6, D_MODEL), lambda b, i: (b, 0, 0)),
            pl.BlockSpec((D_MODEL, D_MODEL), lambda b, i: (0, 0)),
        ],
        out_specs=pl.BlockSpec((None, tm, D_MODEL), lambda b, i: (b, i, 0)),
        compiler_params=_cparams(("parallel", "parallel")),
        name="post_attn",
    )(ot, x, mod_l, wo)


def _f1_kernel(x_ref, mod_ref, gmix_ref, fc_ref, m1_ref, twr_ref, twi_ref, y_ref, z_scr):
    n1 = x_ref.shape[0]
    x = x_ref[...].reshape(n1 * FFT_T, D_MODEL)
    mod = mod_ref[...]
    h = _rms_rows(x, gmix_ref[...]) * (1.0 + mod[1:2]) + mod[0:1]
    hb = h.astype(BF16)
    fc = fc_ref[...]
    for g in range(N_FGROUPS):
        zg = _dot(hb[:, FGROUP * g:FGROUP * (g + 1)], fc)
        z_scr[:, :, FGROUP * g:FGROUP * (g + 1)] = zg[:, :FGROUP].reshape(n1, FFT_T, FGROUP)
        z_scr[:, :, D_MODEL + FGROUP * g:D_MODEL + FGROUP * (g + 1)] = zg[:, FGROUP:].reshape(n1, FFT_T, FGROUP)
    m1 = m1_ref[...]
    for j in range(FFT_T):
        zj = z_scr[:, j, :]
        st = jnp.concatenate([zj[:, :D_MODEL], zj[:, D_MODEL:]], axis=0).astype(BF16)
        y = _dot(m1, st)
        yr, yi = y[:n1], y[n1:]
        tr = jnp.tile(twr_ref[j], (1, D_MODEL // 128))
        ti = jnp.tile(twi_ref[j], (1, D_MODEL // 128))
        y_ref[:, j, 0:D_MODEL] = yr * tr - yi * ti
        y_ref[:, j, D_MODEL:2 * D_MODEL] = yr * ti + yi * tr


def _f2_kernel(y_ref, x_ref, mod_ref, cs_ref, wc_ref, o_ref, f_scr):
    cs = cs_ref[...]
    for j in range(FFT_T):
        yk = y_ref[j]
        st = jnp.concatenate([yk[:, :D_MODEL], yk[:, D_MODEL:]], axis=0).astype(BF16)
        f_scr[FFT_N2 * j:FFT_N2 * (j + 1), :] = _dot(cs, st).astype(BF16)
    m = _dot(f_scr[...], wc_ref[...])
    gt1 = mod_ref[...][2:3]
    for j in range(FFT_T):
        o_ref[:, j, :] = x_ref[:, j, :] + gt1 * m[FFT_N2 * j:FFT_N2 * (j + 1)]


def _dft_constants(S):
    n1 = S // FFT_N2
    k = np.arange(FGROUP)
    ang = 2.0 * np.pi * np.outer(k, k) / FGROUP
    fc = np.concatenate([np.cos(ang), -np.sin(ang)], axis=1) / math.sqrt(FGROUP)
    a = np.arange(n1)
    ang1 = 2.0 * np.pi * np.outer(a, a) / n1
    c1, s1 = np.cos(ang1), np.sin(ang1)
    m1 = np.block([[c1, s1], [-s1, c1]]) / math.sqrt(n1)
    b = np.arange(FFT_N2)
    ang2 = 2.0 * np.pi * np.outer(b, b) / FFT_N2
    cs = np.concatenate([np.cos(ang2), np.sin(ang2)], axis=1) / math.sqrt(FFT_N2)
    angt = 2.0 * np.pi * np.outer(b, a) / S
    twr = np.broadcast_to(np.cos(angt)[:, :, None], (FFT_N2, n1, 128))
    twi = np.broadcast_to(-np.sin(angt)[:, :, None], (FFT_N2, n1, 128))
    return (jnp.asarray(fc, BF16), jnp.asarray(m1, BF16), jnp.asarray(cs, BF16),
            jnp.asarray(twr, F32), jnp.asarray(twi, F32))


def _fourier_layer(x, mod_l, g_mix, wc):
    B, S, _ = x.shape
    n1 = S // FFT_N2
    fc, m1, cs, twr, twi = _dft_constants(S)
    const2 = lambda b, i: (0, 0)
    y = pl.pallas_call(
        _f1_kernel,
        out_shape=jax.ShapeDtypeStruct((B, n1, FFT_N2, 2 * D_MODEL), F32),
        grid=(B, FFT_N2 // FFT_T),
        in_specs=[
            pl.BlockSpec((None, n1, FFT_T, D_MODEL), lambda b, i: (b, 0, i, 0)),
            pl.BlockSpec((None, 6, D_MODEL), lambda b, i: (b, 0, 0)),
            pl.BlockSpec((1, D_MODEL), const2),
            pl.BlockSpec((FGROUP, 2 * FGROUP), const2),
            pl.BlockSpec((2 * n1, 2 * n1), const2),
            pl.BlockSpec((FFT_T, n1, 128), lambda b, i: (i, 0, 0)),
            pl.BlockSpec((FFT_T, n1, 128), lambda b, i: (i, 0, 0)),
        ],
        out_specs=pl.BlockSpec((None, n1, FFT_T, 2 * D_MODEL), lambda b, i: (b, 0, i, 0)),
        scratch_shapes=[pltpu.VMEM((n1, FFT_T, 2 * D_MODEL), F32)],
        compiler_params=_cparams(("parallel", "parallel")),
        name="fourier_stage1",
    )(x.reshape(B, n1, FFT_N2, D_MODEL), mod_l, g_mix, fc, m1, twr, twi)
    out = pl.pallas_call(
        _f2_kernel,
        out_shape=jax.ShapeDtypeStruct((B, FFT_N2, n1, D_MODEL), F32),
        grid=(B, n1 // FFT_T),
        in_specs=[
            pl.BlockSpec((None, FFT_T, FFT_N2, 2 * D_MODEL), lambda b, i: (b, i, 0, 0)),
            pl.BlockSpec((None, FFT_N2, FFT_T, D_MODEL), lambda b, i: (b, 0, i, 0)),
            pl.BlockSpec((None, 6, D_MODEL), lambda b, i: (b, 0, 0)),
            pl.BlockSpec((FFT_N2, 2 * FFT_N2), const2),
            pl.BlockSpec((D_MODEL, D_MODEL), const2),
        ],
        out_specs=pl.BlockSpec((None, FFT_N2, FFT_T, D_MODEL), lambda b, i: (b, 0, i, 0)),
        scratch_shapes=[pltpu.VMEM((FFT_T * FFT_N2, D_MODEL), BF16)],
        compiler_params=_cparams(("parallel", "parallel")),
        name="fourier_stage2",
    )(y, x.reshape(B, FFT_N2, n1, D_MODEL), mod_l, cs, wc)
    return out.reshape(B, S, D_MODEL)


def _route_t(s, sb):
    rows = [sb[e:e + 1] for e in range(N_EXPERTS)]
    gscore = []
    for g in range(N_EGROUPS):
        a, b, c, d = rows[4 * g:4 * g + 4]
        gscore.append(jnp.maximum(jnp.maximum(jnp.maximum(a + b, a + c), jnp.maximum(a + d, b + c)),
                                  jnp.maximum(b + d, c + d)))
    best = gscore[0]
    gsel = jnp.zeros_like(best)
    for g in range(1, N_EGROUPS):
        upd = gscore[g] > best
        gsel = jnp.where(upd, float(g), gsel)
        best = jnp.where(upd, gscore[g], best)
    picked = []
    for e in range(N_EXPERTS):
        g = e // EXPERTS_PER_GROUP
        rank = jnp.zeros_like(best)
        for j in range(4 * g, 4 * g + 4):
            if j == e:
                continue
            ahead = (rows[j] >= rows[e]) if j < e else (rows[j] > rows[e])
            rank = rank + jnp.where(ahead, 1.0, 0.0)
        sel = jnp.where(rank < 1.5, 1.0, 0.0) * jnp.where(gsel == float(g), 1.0, 0.0)
        picked.append(sel * s[e:e + 1])
    total = picked[0]
    for e in range(1, N_EXPERTS):
        total = total + picked[e]
    inv = 1.0 / total
    return [p * inv for p in picked]


def _moe_kernel(x_ref, mod_ref, gffn_ref, wr_ref, bias_ref, wg_ref, wu_ref, wd_ref, gfin_ref,
                out_ref, hb_scr, gate_scr, acc_scr, *, final):
    g = pl.program_id(2)
    tm = x_ref.shape[0]

    @pl.when(g == 0)
    def _():
        mod = mod_ref[...]
        t = _rms_rows(x_ref[...], gffn_ref[...]) * (1.0 + mod[4:5]) + mod[3:4]
        t_hi = t.astype(BF16)
        t_lo = (t - t_hi.astype(F32)).astype(BF16)
        wr = wr_ref[...]
        a = _dot_nt(wr, t_hi)
        b = _dot_nt(wr[0:N_EXPERTS], t_lo)
        logits = a[0:N_EXPERTS] + a[N_EXPERTS:2 * N_EXPERTS] + b
        s = _sigmoid(logits)
        gates = _route_t(s, s + bias_ref[...])
        zpad = jnp.zeros((128 - EXPERTS_PER_GROUP, tm), F32)
        for gg in range(N_EGROUPS):
            gt = jnp.concatenate(gates[4 * gg:4 * gg + 4] + [zpad], axis=0)
            gate_scr[gg] = gt.T
        hb_scr[...] = t_hi
        acc_scr[...] = jnp.zeros_like(acc_scr)

    hb = hb_scr[...]
    u1 = _dot(hb, wg_ref[...])
    u2 = _dot(hb, wu_ref[...])
    he = u1 * _sigmoid(u1) * u2
    gcols = gate_scr[g]
    he = jnp.concatenate(
        [he[:, D_EXPERT * j:D_EXPERT * (j + 1)] * gcols[:, j:j + 1] for j in range(EXPERTS_PER_GROUP)],
        axis=1)
    acc_scr[...] += _dot(he.astype(BF16), wd_ref[...])

    @pl.when(g == N_EGROUPS - 1)
    def _():
        y = x_ref[...] + mod_ref[...][5:6] * acc_scr[...]
        if final:
            y = _rms_rows(y, gfin_ref[...])
        out_ref[...] = y


def _moe(x, mod_l, g_ffn, wr, bias, wg, wu, wd, g_final, final):
    B, S, _ = x.shape
    tm = MOE_TILE
    const = lambda b, i, g: (0, 0)
    wspec = pl.BlockSpec((None, D_MODEL, D_MODEL), lambda b, i, g: (g, 0, 0))
    return pl.pallas_call(
        functools.partial(_moe_kernel, final=final),
        out_shape=jax.ShapeDtypeStruct(x.shape, F32),
        grid=(B, S // tm, N_EGROUPS),
        in_specs=[
            pl.BlockSpec((None, tm, D_MODEL), lambda b, i, g: (b, i, 0)),
            pl.BlockSpec((None, 6, D_MODEL), lambda b, i, g: (b, 0, 0)),
            pl.BlockSpec((1, D_MODEL), const),
            pl.BlockSpec((2 * N_EXPERTS, D_MODEL), const),
            pl.BlockSpec((N_EXPERTS, 1), const),
            wspec, wspec, wspec,
            pl.BlockSpec((1, D_MODEL), const),
        ],
        out_specs=pl.BlockSpec((None, tm, D_MODEL), lambda b, i, g: (b, i, 0)),
        scratch_shapes=[
            pltpu.VMEM((tm, D_MODEL), BF16),
            pltpu.VMEM((N_EGROUPS, tm, 128), F32),
            pltpu.VMEM((tm, D_MODEL), F32),
        ],
        compiler_params=_cparams(("parallel", "parallel", "arbitrary")),
        name="moe",
    )(x, mod_l, g_ffn, wr, bias, wg, wu, wd, g_final)


def _rope_tables(S):
    pos = jnp.arange(S, dtype=jnp.int32)
    rows = (pos // GRID_W).astype(F32)[None, :]
    cols = (pos % GRID_W).astype(F32)[None, :]
    tabs = []
    for half in (HEAD_DIM // 2, D_ROPE // 2):
        inv = (ROPE_THETA ** (-jnp.arange(0, half, 2, dtype=F32) / half))[:, None]
        for p in (rows, cols):
            ang = inv * p
            tabs += [jnp.cos(ang), jnp.sin(ang)]
    return jnp.concatenate(tabs, axis=0)


def _group_experts(w, transpose_cols):
    e, a, b = w.shape
    w = w.reshape(N_EGROUPS, EXPERTS_PER_GROUP, a, b)
    if transpose_cols:
        return w.transpose(0, 2, 1, 3).reshape(N_EGROUPS, a, EXPERTS_PER_GROUP * b).astype(BF16)
    return w.reshape(N_EGROUPS, EXPERTS_PER_GROUP * a, b).astype(BF16)


def _trunk(x, mod, p):
    B, S, _ = x.shape
    tab = _rope_tables(S)
    for l in range(DEPTH):
        mod_l = mod[l]
        g_mix = p["g_mix"][l][None, :]
        if l % 2 == 0:
            i = l // 2
            qt, k, vt = _pre_attn(x, mod_l, g_mix, tab, p["wint"][i], p["g_qa"][i][:, None],
                                  p["g_ka"][i][:, None], p["g_cq"][i][:, None], p["wuqt"][i],
                                  p["g_ckv"][i][:, None], p["wukvt"][i])
            ot = _attention(qt, k, vt)
            x = _post_attn(ot, x, mod_l, p["wo"][i])
        else:
            x = _fourier_layer(x, mod_l, g_mix, p["wc"][l // 2])
        x = _moe(x, mod_l, p["g_ffn"][l][None, :], p["wr"], p["bias"], p["wg"][l], p["wu"][l],
                 p["wd"][l], p["g_final"], final=(l == DEPTH - 1))
    return x


def kernel(x_prompt, x_sample, c_prompt, c_sample, w_ada, b_ada, g_mix, g_ffn, w_in, g_qa, g_ka, g_cq,
           w_uq, g_ckv, w_ukv, w_o_attn, w_fourier, w_router, router_bias, w_gate, w_up, w_down, g_final):
    nbp = x_prompt.shape[0]
    mod = _modulation(jnp.concatenate([c_prompt, c_sample], axis=0), w_ada, b_ada)
    mod = mod.reshape(DEPTH, -1, 6, D_MODEL)
    wrt = w_router.T
    wr_hi = wrt.astype(BF16)
    wr_lo = (wrt - wr_hi.astype(F32)).astype(BF16)
    p = {
        "g_mix": g_mix, "g_ffn": g_ffn, "g_qa": g_qa, "g_ka": g_ka, "g_cq": g_cq, "g_ckv": g_ckv,
        "wint": jnp.swapaxes(w_in, 1, 2).astype(BF16),
        "wuqt": jnp.swapaxes(w_uq, 1, 2).astype(BF16),
        "wukvt": jnp.swapaxes(w_ukv, 1, 2).astype(BF16),
        "wo": w_o_attn.astype(BF16),
        "wc": w_fourier.astype(BF16),
        "wr": jnp.concatenate([wr_hi, wr_lo], axis=0),
        "bias": router_bias[:, None],
        "wg": [_group_experts(w_gate[l], True) for l in range(DEPTH)],
        "wu": [_group_experts(w_up[l], True) for l in range(DEPTH)],
        "wd": [_group_experts(w_down[l], False) for l in range(DEPTH)],
        "g_final": g_final[None, :],
    }
    y_prompt = _trunk(x_prompt, mod[:, :nbp], p)
    y_sample = _trunk(x_sample, mod[:, nbp:], p)
    return (y_prompt, y_sample)
```

```python
import functools
import math

import numpy as np
import jax
import jax.numpy as jnp
from jax import lax
from jax.experimental import pallas as pl
from jax.experimental.pallas import tpu as pltpu

F32 = jnp.float32
BF16 = jnp.bfloat16

D_MODEL = 1024
DEPTH = 4
GRID_W = 64
HEAD_DIM = 64
N_HEADS_A = 8
N_KV_A = 2
N_HEADS_B = 8
D_NOPE = 64
D_ROPE = 32
D_V = 64
D_CQ = 384
D_CKV = 256
N_FGROUPS = 4
FGROUP = D_MODEL // N_FGROUPS
ROPE_THETA = 10000.0
N_EXPERTS = 16
N_EGROUPS = 4
EXPERTS_PER_GROUP = 4
D_EXPERT = 256
EPS = 1e-6
IN_SIZES = (N_HEADS_A * HEAD_DIM, N_KV_A * HEAD_DIM, N_KV_A * HEAD_DIM, D_CQ, D_CKV, D_ROPE)
IN_WIDTH = sum(IN_SIZES)
IN_OFFS = tuple(int(v) for v in np.cumsum((0,) + IN_SIZES))

N_HEADS = N_HEADS_A + N_HEADS_B
N_KSLOTS = 1 + N_HEADS_B
N_VSLOTS = N_KV_A + N_HEADS_B
QK_PAD = 128
LOG2E = 1.4426950408889634
SCALE_A = HEAD_DIM ** -0.5 * LOG2E
SCALE_B = (D_NOPE + D_ROPE) ** -0.5 * LOG2E
NEG_BIG = -1e30

FFT_N2 = 128
FFT_T = 8

TOKEN_TILE = 512
Q_TILE = 512
V_ROWS = D_V + 16
MOE_TILE = 512
VMEM_LIMIT = 56 * 1024 * 1024


def _cparams(sem):
    return pltpu.CompilerParams(dimension_semantics=sem, vmem_limit_bytes=VMEM_LIMIT)


def _sigmoid(x):
    return 1.0 / (1.0 + jnp.exp(-x))


def _rms_rows(x, g):
    ms = jnp.mean(x * x, axis=-1, keepdims=True)
    return x * lax.rsqrt(ms + EPS) * g


def _rms_cols(xt, gcol):
    ms = jnp.mean(xt * xt, axis=0, keepdims=True)
    return xt * lax.rsqrt(ms + EPS) * gcol


def _dot(a, b):
    return jnp.dot(a, b, preferred_element_type=F32)


def _dot_nt(a, b):
    return lax.dot_general(a, b, (((1,), (1,)), ((), ())), preferred_element_type=F32)


def _dot_tn(a, b):
    return lax.dot_general(a, b, (((0,), (0,)), ((), ())), preferred_element_type=F32)


def _mod_kernel(c_ref, w_ref, b_ref, o_ref):
    c = c_ref[...]
    ca = c * _sigmoid(c)
    o_ref[...] = _dot(ca, w_ref[...]) + b_ref[...]


def _modulation(c_all, w_ada, b_ada):
    nb = c_all.shape[0]
    tn = 1536
    return pl.pallas_call(
        _mod_kernel,
        out_shape=jax.ShapeDtypeStruct((DEPTH, nb, 6 * D_MODEL), F32),
        grid=(DEPTH, 6 * D_MODEL // tn),
        in_specs=[
            pl.BlockSpec((nb, D_MODEL), lambda l, j: (0, 0)),
            pl.BlockSpec((None, D_MODEL, tn), lambda l, j: (l, 0, j)),
            pl.BlockSpec((None, 1, tn), lambda l, j: (l, 0, j)),
        ],
        out_specs=pl.BlockSpec((None, nb, tn), lambda l, j: (l, 0, j)),
        compiler_params=_cparams(("arbitrary", "arbitrary")),
        name="adaln_mod",
    )(c_all, w_ada, b_ada.reshape(DEPTH, 1, 6 * D_MODEL))


def _rope_t(xt, c, s):
    n = c.shape[0]
    x1, x2 = xt[:n], xt[n:]
    return jnp.concatenate([x1 * c - x2 * s, x1 * s + x2 * c], axis=0)


def _axial_t(xt, cr, sr, cc, sc):
    d2 = xt.shape[0] // 2
    return jnp.concatenate([_rope_t(xt[:d2], cr, sr), _rope_t(xt[d2:], cc, sc)], axis=0)


def _pre_attn_kernel(x_ref, mod_ref, gmix_ref, tab_ref, wint_ref, gqa_ref, gka_ref, gcq_ref,
                     wuqt_ref, gckv_ref, wukvt_ref, q_ref, k_ref, v_ref):
    x = x_ref[...]
    tm = x.shape[0]
    mod = mod_ref[...]
    h = _rms_rows(x, gmix_ref[...]) * (1.0 + mod[1:2]) + mod[0:1]
    zt = _dot_nt(wint_ref[...], h.astype(BF16))

    tab = tab_ref[...]
    ra = (tab[0:16], tab[16:32], tab[32:48], tab[48:64])
    rb = (tab[64:72], tab[72:80], tab[80:88], tab[88:96])
    zeros64 = jnp.zeros((HEAD_DIM, tm), F32)
    zeros32 = jnp.zeros((QK_PAD - D_NOPE - D_ROPE, tm), F32)

    o_q, o_k, o_v, o_cq, o_ckv, o_kr = IN_OFFS[:6]
    gqa = gqa_ref[...]
    for hh in range(N_HEADS_A):
        qh = _axial_t(_rms_cols(zt[o_q + HEAD_DIM * hh:o_q + HEAD_DIM * (hh + 1)], gqa), *ra) * SCALE_A
        parts = [qh, zeros64] if hh // (N_HEADS_A // N_KV_A) == 0 else [zeros64, qh]
        q_ref[hh] = jnp.concatenate(parts, axis=0).astype(BF16)
    gka = gka_ref[...]
    kts = [jnp.concatenate(
        [_axial_t(_rms_cols(zt[o_k + HEAD_DIM * g:o_k + HEAD_DIM * (g + 1)], gka), *ra)
         for g in range(N_KV_A)], axis=0)]
    ones16 = jnp.ones((V_ROWS - D_V, tm), F32)
    for g in range(N_KV_A):
        v_ref[g] = jnp.concatenate(
            [zt[o_v + HEAD_DIM * g:o_v + HEAD_DIM * (g + 1)], ones16], axis=0).astype(BF16)

    cqn = _rms_cols(zt[o_cq:o_ckv], gcq_ref[...]).astype(BF16)
    qbt = _dot(wuqt_ref[...], cqn)
    dq = D_NOPE + D_ROPE
    for hh in range(N_HEADS_B):
        nope = qbt[dq * hh:dq * hh + D_NOPE]
        rp = _axial_t(qbt[dq * hh + D_NOPE:dq * (hh + 1)], *rb)
        q_ref[N_HEADS_A + hh] = (jnp.concatenate([nope, rp, zeros32], axis=0) * SCALE_B).astype(BF16)
    ckvn = _rms_cols(zt[o_ckv:o_kr], gckv_ref[...]).astype(BF16)
    kvt = _dot(wukvt_ref[...], ckvn)
    kr = _axial_t(zt[o_kr:o_kr + D_ROPE], *rb)
    dkv = D_NOPE + D_V
    for hh in range(N_HEADS_B):
        kts.append(jnp.concatenate([kvt[dkv * hh:dkv * hh + D_NOPE], kr, zeros32], axis=0))
        v_ref[N_KV_A + hh] = jnp.concatenate(
            [kvt[dkv * hh + D_NOPE:dkv * (hh + 1)], ones16], axis=0).astype(BF16)

    k_all = jnp.concatenate(kts, axis=0).T
    for i in range(N_KSLOTS):
        k_ref[i] = k_all[:, QK_PAD * i:QK_PAD * (i + 1)].astype(BF16)


def _pre_attn(x, mod_l, g_mix, tab, wint, gqa, gka, gcq, wuqt, gckv, wukvt):
    B, S, _ = x.shape
    tm = TOKEN_TILE
    nt = S // tm
    const = lambda b, i: (0, 0)
    return pl.pallas_call(
        _pre_attn_kernel,
        out_shape=(
            jax.ShapeDtypeStruct((B, N_HEADS, QK_PAD, S), BF16),
            jax.ShapeDtypeStruct((B, N_KSLOTS, S, QK_PAD), BF16),
            jax.ShapeDtypeStruct((B, N_VSLOTS, nt, V_ROWS, tm), BF16),
        ),
        grid=(B, nt),
        in_specs=[
            pl.BlockSpec((None, tm, D_MODEL), lambda b, i: (b, i, 0)),
            pl.BlockSpec((None, 6, D_MODEL), lambda b, i: (b, 0, 0)),
            pl.BlockSpec((1, D_MODEL), const),
            pl.BlockSpec((96, tm), lambda b, i: (0, i)),
            pl.BlockSpec((IN_WIDTH, D_MODEL), const),
            pl.BlockSpec((HEAD_DIM, 1), const),
            pl.BlockSpec((HEAD_DIM, 1), const),
            pl.BlockSpec((D_CQ, 1), const),
            pl.BlockSpec((N_HEADS_B * (D_NOPE + D_ROPE), D_CQ), const),
            pl.BlockSpec((D_CKV, 1), const),
            pl.BlockSpec((N_HEADS_B * (D_NOPE + D_V), D_CKV), const),
        ],
        out_specs=(
            pl.BlockSpec((None, N_HEADS, QK_PAD, tm), lambda b, i: (b, 0, 0, i)),
            pl.BlockSpec((None, N_KSLOTS, tm, QK_PAD), lambda b, i: (b, 0, i, 0)),
            pl.BlockSpec((None, N_VSLOTS, None, V_ROWS, tm), lambda b, i: (b, 0, i, 0, 0)),
        ),
        compiler_params=_cparams(("parallel", "parallel")),
        name="pre_attn",
    )(x, mod_l, g_mix, tab, wint, gqa, gka, gcq, wuqt, gckv, wukvt)


def _attn_kernel(q_ref, k_ref, v_ref, o_ref, sa_scr, sb_scr, *, n_chunks, tk, trip):
    qt = q_ref[...]
    tq = qt.shape[1]

    def scores(c, s_ref):
        off = pl.multiple_of(c * tk, tk)
        s = _dot(k_ref[pl.ds(off, tk), :], qt)
        s_ref[...] = s
        return jnp.max(s, axis=0, keepdims=True)

    def update(c, s_ref, cm, m, acc):
        m_new = jnp.maximum(m, cm)
        p = jnp.exp2((s_ref[...] - m_new).astype(BF16))
        return m_new, acc * jnp.exp2(m - m_new) + _dot(v_ref[c], p)

    bufs = (sa_scr, sb_scr)

    def steps(c0, count, cm, m, acc, last):
        for u in range(count):
            if not (last and u == count - 1):
                cm_next = scores(c0 + u + 1, bufs[(u + 1) % 2])
            m, acc = update(c0 + u, bufs[u % 2], cm, m, acc)
            cm = cm_next
        return cm, m, acc

    def body(i, carry):
        return steps(i * trip, trip, *carry, last=False)

    m0 = jnp.full((1, tq), NEG_BIG, F32)
    acc0 = jnp.zeros((V_ROWS, tq), F32)
    n_trips = n_chunks // trip - 1
    carry = lax.fori_loop(0, n_trips, body, (scores(0, sa_scr), m0, acc0))
    _, _, acc = steps(n_trips * trip, trip, *carry, last=True)
    o_ref[...] = (acc[:D_V] * (1.0 / acc[D_V:D_V + 1])).astype(o_ref.dtype)


def _attention(qt, k, vt):
    B, _, _, S = qt.shape
    n_chunks, tk = vt.shape[2], vt.shape[4]
    tq = Q_TILE
    kslot = lambda h: jnp.where(h < N_HEADS_A, 0, h - (N_HEADS_A - 1))
    vslot = lambda h: jnp.where(h < N_HEADS_A, h // (N_HEADS_A // N_KV_A), h - (N_HEADS_A - N_KV_A))
    return pl.pallas_call(
        functools.partial(_attn_kernel, n_chunks=n_chunks, tk=tk,
                          trip=4 if n_chunks % 4 == 0 and n_chunks >= 8 else 2),
        out_shape=jax.ShapeDtypeStruct((B, N_HEADS, D_V, S), BF16),
        grid=(B, N_HEADS, S // tq),
        in_specs=[
            pl.BlockSpec((None, None, QK_PAD, tq), lambda b, h, i: (b, h, 0, i)),
            pl.BlockSpec((None, None, S, QK_PAD), lambda b, h, i: (b, kslot(h), 0, 0)),
            pl.BlockSpec((None, None, n_chunks, V_ROWS, tk), lambda b, h, i: (b, vslot(h), 0, 0, 0)),
        ],
        out_specs=pl.BlockSpec((None, None, D_V, tq), lambda b, h, i: (b, h, 0, i)),
        scratch_shapes=[pltpu.VMEM((tk, tq), F32), pltpu.VMEM((tk, tq), F32)],
        compiler_params=_cparams(("parallel", "parallel", "arbitrary")),
        name="attn_sweep",
    )(qt, k, vt)


def _post_attn_kernel(o_ref, x_ref, mod_ref, wo_ref, out_ref):
    tm = x_ref.shape[0]
    ot = o_ref[...].reshape(N_HEADS * D_V, tm)
    m = _dot_tn(ot, wo_ref[...])
    out_ref[...] = x_ref[...] + mod_ref[...][2:3] * m


def _post_attn(ot, x, mod_l, wo):
    B, S, _ = x.shape
    tm = TOKEN_TILE
    return pl.pallas_call(
        _post_attn_kernel,
        out_shape=jax.ShapeDtypeStruct(x.shape, F32),
        grid=(B, S // tm),
        in_specs=[
            pl.BlockSpec((None, N_HEADS, D_V, tm), lambda b, i: (b, 0, 0, i)),
            pl.BlockSpec((None, tm, D_MODEL), lambda b, i: (b, i, 0)),
            pl.BlockSpec((None, 6, D_MODEL), lambda b, i: (b, 0, 0)),
            pl.BlockSpec((D_MODEL, D_MODEL), lambda b, i: (0, 0)),
        ],
        out_specs=pl.BlockSpec((None, tm, D_MODEL), lambda b, i: (b, i, 0)),
        compiler_params=_cparams(("parallel", "parallel")),
        name="post_attn",
    )(ot, x, mod_l, wo)


def _f1_kernel(x_ref, mod_ref, gmix_ref, fc_ref, m1_ref, twr_ref, twi_ref, y_ref, z_scr):
    n1 = x_ref.shape[0]
    x = x_ref[...].reshape(n1 * FFT_T, D_MODEL)
    mod = mod_ref[...]
    h = _rms_rows(x, gmix_ref[...]) * (1.0 + mod[1:2]) + mod[0:1]
    hb = h.astype(BF16)
    fc = fc_ref[...]
    for g in range(N_FGROUPS):
        zg = _dot(hb[:, FGROUP * g:FGROUP * (g + 1)], fc)
        z_scr[:, :, FGROUP * g:FGROUP * (g + 1)] = zg[:, :FGROUP].reshape(n1, FFT_T, FGROUP)
        z_scr[:, :, D_MODEL + FGROUP * g:D_MODEL + FGROUP * (g + 1)] = zg[:, FGROUP:].reshape(n1, FFT_T, FGROUP)
    m1 = m1_ref[...]
    for j in range(FFT_T):
        zj = z_scr[:, j, :]
        st = jnp.concatenate([zj[:, :D_MODEL], zj[:, D_MODEL:]], axis=0).astype(BF16)
        y = _dot(m1, st)
        yr, yi = y[:n1], y[n1:]
        tr = jnp.tile(twr_ref[j], (1, D_MODEL // 128))
        ti = jnp.tile(twi_ref[j], (1, D_MODEL // 128))
        y_ref[:, j, 0:D_MODEL] = yr * tr - yi * ti
        y_ref[:, j, D_MODEL:2 * D_MODEL] = yr * ti + yi * tr


def _f2_kernel(y_ref, x_ref, mod_ref, cs_ref, wc_ref, o_ref, f_scr):
    cs = cs_ref[...]
    for j in range(FFT_T):
        yk = y_ref[j]
        st = jnp.concatenate([yk[:, :D_MODEL], yk[:, D_MODEL:]], axis=0).astype(BF16)
        f_scr[FFT_N2 * j:FFT_N2 * (j + 1), :] = _dot(cs, st).astype(BF16)
    m = _dot(f_scr[...], wc_ref[...])
    gt1 = mod_ref[...][2:3]
    for j in range(FFT_T):
        o_ref[:, j, :] = x_ref[:, j, :] + gt1 * m[FFT_N2 * j:FFT_N2 * (j + 1)]


def _dft_constants(S):
    n1 = S // FFT_N2
    k = np.arange(FGROUP)
    ang = 2.0 * np.pi * np.outer(k, k) / FGROUP
    fc = np.concatenate([np.cos(ang), -np.sin(ang)], axis=1) / math.sqrt(FGROUP)
    a = np.arange(n1)
    ang1 = 2.0 * np.pi * np.outer(a, a) / n1
    c1, s1 = np.cos(ang1), np.sin(ang1)
    m1 = np.block([[c1, s1], [-s1, c1]]) / math.sqrt(n1)
    b = np.arange(FFT_N2)
    ang2 = 2.0 * np.pi * np.outer(b, b) / FFT_N2
    cs = np.concatenate([np.cos(ang2), np.sin(ang2)], axis=1) / math.sqrt(FFT_N2)
    angt = 2.0 * np.pi * np.outer(b, a) / S
    twr = np.broadcast_to(np.cos(angt)[:, :, None], (FFT_N2, n1, 128))
    twi = np.broadcast_to(-np.sin(angt)[:, :, None], (FFT_N2, n1, 128))
    return (jnp.asarray(fc, BF16), jnp.asarray(m1, BF16), jnp.asarray(cs, BF16),
            jnp.asarray(twr, F32), jnp.asarray(twi, F32))


def _fourier_layer(x, mod_l, g_mix, wc):
    B, S, _ = x.shape
    n1 = S // FFT_N2
    fc, m1, cs, twr, twi = _dft_constants(S)
    const2 = lambda b, i: (0, 0)
    y = pl.pallas_call(
        _f1_kernel,
        out_shape=jax.ShapeDtypeStruct((B, n1, FFT_N2, 2 * D_MODEL), F32),
        grid=(B, FFT_N2 // FFT_T),
        in_specs=[
            pl.BlockSpec((None, n1, FFT_T, D_MODEL), lambda b, i: (b, 0, i, 0)),
            pl.BlockSpec((None, 6, D_MODEL), lambda b, i: (b, 0, 0)),
            pl.BlockSpec((1, D_MODEL), const2),
            pl.BlockSpec((FGROUP, 2 * FGROUP), const2),
            pl.BlockSpec((2 * n1, 2 * n1), const2),
            pl.BlockSpec((FFT_T, n1, 128), lambda b, i: (i, 0, 0)),
            pl.BlockSpec((FFT_T, n1, 128), lambda b, i: (i, 0, 0)),
        ],
        out_specs=pl.BlockSpec((None, n1, FFT_T, 2 * D_MODEL), lambda b, i: (b, 0, i, 0)),
        scratch_shapes=[pltpu.VMEM((n1, FFT_T, 2 * D_MODEL), F32)],
        compiler_params=_cparams(("parallel", "parallel")),
        name="fourier_stage1",
    )(x.reshape(B, n1, FFT_N2, D_MODEL), mod_l, g_mix, fc, m1, twr, twi)
    out = pl.pallas_call(
        _f2_kernel,
        out_shape=jax.ShapeDtypeStruct((B, FFT_N2, n1, D_MODEL), F32),
        grid=(B, n1 // FFT_T),
        in_specs=[
            pl.BlockSpec((None, FFT_T, FFT_N2, 2 * D_MODEL), lambda b, i: (b, i, 0, 0)),
            pl.BlockSpec((None, FFT_N2, FFT_T, D_MODEL), lambda b, i: (b, 0, i, 0)),
            pl.BlockSpec((None, 6, D_MODEL), lambda b, i: (b, 0, 0)),
            pl.BlockSpec((FFT_N2, 2 * FFT_N2), const2),
            pl.BlockSpec((D_MODEL, D_MODEL), const2),
        ],
        out_specs=pl.BlockSpec((None, FFT_N2, FFT_T, D_MODEL), lambda b, i: (b, 0, i, 0)),
        scratch_shapes=[pltpu.VMEM((FFT_T * FFT_N2, D_MODEL), BF16)],
        compiler_params=_cparams(("parallel", "parallel")),
        name="fourier_stage2",
    )(y, x.reshape(B, FFT_N2, n1, D_MODEL), mod_l, cs, wc)
    return out.reshape(B, S, D_MODEL)


def _route_t(s, sb):
    rows = [sb[e:e + 1] for e in range(N_EXPERTS)]
    gscore = []
    for g in range(N_EGROUPS):
        a, b, c, d = rows[4 * g:4 * g + 4]
        gscore.append(jnp.maximum(jnp.maximum(jnp.maximum(a + b, a + c), jnp.maximum(a + d, b + c)),
                                  jnp.maximum(b + d, c + d)))
    best = gscore[0]
    gsel = jnp.zeros_like(best)
    for g in range(1, N_EGROUPS):
        upd = gscore[g] > best
        gsel = jnp.where(upd, float(g), gsel)
        best = jnp.where(upd, gscore[g], best)
    picked = []
    for e in range(N_EXPERTS):
        g = e // EXPERTS_PER_GROUP
        rank = jnp.zeros_like(best)
        for j in range(4 * g, 4 * g + 4):
            if j == e:
                continue
            ahead = (rows[j] >= rows[e]) if j < e else (rows[j] > rows[e])
            rank = rank + jnp.where(ahead, 1.0, 0.0)
        sel = jnp.where(rank < 1.5, 1.0, 0.0) * jnp.where(gsel == float(g), 1.0, 0.0)
        picked.append(sel * s[e:e + 1])
    total = picked[0]
    for e in range(1, N_EXPERTS):
        total = total + picked[e]
    inv = 1.0 / total
    return [p * inv for p in picked]


def _moe_kernel(x_ref, mod_ref, gffn_ref, wr_ref, bias_ref, wg_ref, wu_ref, wd_ref, gfin_ref,
                out_ref, hb_scr, gate_scr, acc_scr, *, final):
    g = pl.program_id(2)
    tm = x_ref.shape[0]

    @pl.when(g == 0)
    def _():
        mod = mod_ref[...]
        t = _rms_rows(x_ref[...], gffn_ref[...]) * (1.0 + mod[4:5]) + mod[3:4]
        t_hi = t.astype(BF16)
        t_lo = (t - t_hi.astype(F32)).astype(BF16)
        wr = wr_ref[...]
        a = _dot_nt(wr, t_hi)
        b = _dot_nt(wr[0:N_EXPERTS], t_lo)
        logits = a[0:N_EXPERTS] + a[N_EXPERTS:2 * N_EXPERTS] + b
        s = _sigmoid(logits)
        gates = _route_t(s, s + bias_ref[...])
        zpad = jnp.zeros((128 - EXPERTS_PER_GROUP, tm), F32)
        for gg in range(N_EGROUPS):
            gt = jnp.concatenate(gates[4 * gg:4 * gg + 4] + [zpad], axis=0)
            gate_scr[gg] = gt.T
        hb_scr[...] = t_hi
        acc_scr[...] = jnp.zeros_like(acc_scr)

    hb = hb_scr[...]
    u1 = _dot(hb, wg_ref[...])
    u2 = _dot(hb, wu_ref[...])
    he = u1 * _sigmoid(u1) * u2
    gcols = gate_scr[g]
    he = jnp.concatenate(
        [he[:, D_EXPERT * j:D_EXPERT * (j + 1)] * gcols[:, j:j + 1] for j in range(EXPERTS_PER_GROUP)],
        axis=1)
    acc_scr[...] += _dot(he.astype(BF16), wd_ref[...])

    @pl.when(g == N_EGROUPS - 1)
    def _():
        y = x_ref[...] + mod_ref[...][5:6] * acc_scr[...]
        if final:
            y = _rms_rows(y, gfin_ref[...])
        out_ref[...] = y


def _moe(x, mod_l, g_ffn, wr, bias, wg, wu, wd, g_final, final):
    B, S, _ = x.shape
    tm = MOE_TILE
    const = lambda b, i, g: (0, 0)
    wspec = pl.BlockSpec((None, D_MODEL, D_MODEL), lambda b, i, g: (g, 0, 0))
    return pl.pallas_call(
        functools.partial(_moe_kernel, final=final),
        out_shape=jax.ShapeDtypeStruct(x.shape, F32),
        grid=(B, S // tm, N_EGROUPS),
        in_specs=[
            pl.BlockSpec((None, tm, D_MODEL), lambda b, i, g: (b, i, 0)),
            pl.BlockSpec((None, 6, D_MODEL), lambda b, i, g: (b, 0, 0)),
            pl.BlockSpec((1, D_MODEL), const),
            pl.BlockSpec((2 * N_EXPERTS, D_MODEL), const),
            pl.BlockSpec((N_EXPERTS, 1), const),
            wspec, wspec, wspec,
            pl.BlockSpec((1, D_MODEL), const),
        ],
        out_specs=pl.BlockSpec((None, tm, D_MODEL), lambda b, i, g: (b, i, 0)),
        scratch_shapes=[
            pltpu.VMEM((tm, D_MODEL), BF16),
            pltpu.VMEM((N_EGROUPS, tm, 128), F32),
            pltpu.VMEM((tm, D_MODEL), F32),
        ],
        compiler_params=_cparams(("parallel", "parallel", "arbitrary")),
        name="moe",
    )(x, mod_l, g_ffn, wr, bias, wg, wu, wd, g_final)


def _rope_tables(S):
    pos = jnp.arange(S, dtype=jnp.int32)
    rows = (pos // GRID_W).astype(F32)[None, :]
    cols = (pos % GRID_W).astype(F32)[None, :]
    tabs = []
    for half in (HEAD_DIM // 2, D_ROPE // 2):
        inv = (ROPE_THETA ** (-jnp.arange(0, half, 2, dtype=F32) / half))[:, None]
        for p in (rows, cols):
            ang = inv * p
            tabs += [jnp.cos(ang), jnp.sin(ang)]
    return jnp.concatenate(tabs, axis=0)


def _group_experts(w, transpose_cols):
    e, a, b = w.shape
    w = w.reshape(N_EGROUPS, EXPERTS_PER_GROUP, a, b)
    if transpose_cols:
        return w.transpose(0, 2, 1, 3).reshape(N_EGROUPS, a, EXPERTS_PER_GROUP * b).astype(BF16)
    return w.reshape(N_EGROUPS, EXPERTS_PER_GROUP * a, b).astype(BF16)


def _trunk(x, mod, p):
    B, S, _ = x.shape
    tab = _rope_tables(S)
    for l in range(DEPTH):
        mod_l = mod[l]
        g_mix = p["g_mix"][l][None, :]
        if l % 2 == 0:
            i = l // 2
            qt, k, vt = _pre_attn(x, mod_l, g_mix, tab, p["wint"][i], p["g_qa"][i][:, None],
                                  p["g_ka"][i][:, None], p["g_cq"][i][:, None], p["wuqt"][i],
                                  p["g_ckv"][i][:, None], p["wukvt"][i])
            ot = _attention(qt, k, vt)
            x = _post_attn(ot, x, mod_l, p["wo"][i])
        else:
            x = _fourier_layer(x, mod_l, g_mix, p["wc"][l // 2])
        x = _moe(x, mod_l, p["g_ffn"][l][None, :], p["wr"], p["bias"], p["wg"][l], p["wu"][l],
                 p["wd"][l], p["g_final"], final=(l == DEPTH - 1))
    return x


def kernel(x_prompt, x_sample, c_prompt, c_sample, w_ada, b_ada, g_mix, g_ffn, w_in, g_qa, g_ka, g_cq,
           w_uq, g_ckv, w_ukv, w_o_attn, w_fourier, w_router, router_bias, w_gate, w_up, w_down, g_final):
    nbp = x_prompt.shape[0]
    mod = _modulation(jnp.concatenate([c_prompt, c_sample], axis=0), w_ada, b_ada)
    mod = mod.reshape(DEPTH, -1, 6, D_MODEL)
    wrt = w_router.T
    wr_hi = wrt.astype(BF16)
    wr_lo = (wrt - wr_hi.astype(F32)).astype(BF16)
    p = {
        "g_mix": g_mix, "g_ffn": g_ffn, "g_qa": g_qa, "g_ka": g_ka, "g_cq": g_cq, "g_ckv": g_ckv,
        "wint": jnp.swapaxes(w_in, 1, 2).astype(BF16),
        "wuqt": jnp.swapaxes(w_uq, 1, 2).astype(BF16),
        "wukvt": jnp.swapaxes(w_ukv, 1, 2).astype(BF16),
        "wo": w_o_attn.astype(BF16),
        "wc": w_fourier.astype(BF16),
        "wr": jnp.concatenate([wr_hi, wr_lo], axis=0),
        "bias": router_bias[:, None],
        "wg": [_group_experts(w_gate[l], True) for l in range(DEPTH)],
        "wu": [_group_experts(w_up[l], True) for l in range(DEPTH)],
        "wd": [_group_experts(w_down[l], False) for l in range(DEPTH)],
        "g_final": g_final[None, :],
    }
    y_prompt = _trunk(x_prompt, mod[:, :nbp], p)
    y_sample = _trunk(x_sample, mod[:, nbp:], p)
    return (y_prompt, y_sample)
```

```python
import functools
import math

import numpy as np
import jax
import jax.numpy as jnp
from jax import lax
from jax.experimental import pallas as pl
from jax.experimental.pallas import tpu as pltpu

F32 = jnp.float32
BF16 = jnp.bfloat16

D_MODEL = 1024
DEPTH = 4
GRID_W = 64
HEAD_DIM = 64
N_HEADS_A = 8
N_KV_A = 2
N_HEADS_B = 8
D_NOPE = 64
D_ROPE = 32
D_V = 64
D_CQ = 384
D_CKV = 256
N_FGROUPS = 4
FGROUP = D_MODEL // N_FGROUPS
ROPE_THETA = 10000.0
N_EXPERTS = 16
N_EGROUPS = 4
EXPERTS_PER_GROUP = 4
D_EXPERT = 256
EPS = 1e-6
IN_SIZES = (N_HEADS_A * HEAD_DIM, N_KV_A * HEAD_DIM, N_KV_A * HEAD_DIM, D_CQ, D_CKV, D_ROPE)
IN_WIDTH = sum(IN_SIZES)
IN_OFFS = tuple(int(v) for v in np.cumsum((0,) + IN_SIZES))

N_HEADS = N_HEADS_A + N_HEADS_B
N_KSLOTS = 1 + N_HEADS_B
N_VSLOTS = N_KV_A + N_HEADS_B
QK_PAD = 128
LOG2E = 1.4426950408889634
SCALE_A = HEAD_DIM ** -0.5 * LOG2E
SCALE_B = (D_NOPE + D_ROPE) ** -0.5 * LOG2E
NEG_BIG = -1e30

FFT_N2 = 128
FFT_T = 8

TOKEN_TILE = 512
Q_TILE = 512
V_ROWS = D_V + 16
MOE_TILE = 512
VMEM_LIMIT = 56 * 1024 * 1024


def _cparams(sem, flags=None):
    return pltpu.CompilerParams(dimension_semantics=sem, vmem_limit_bytes=VMEM_LIMIT, flags=flags)


def _sigmoid(x):
    return 1.0 / (1.0 + jnp.exp(-x))


def _rms_rows(x, g):
    ms = jnp.mean(x * x, axis=-1, keepdims=True)
    return x * lax.rsqrt(ms + EPS) * g


def _rms_cols(xt, gcol):
    ms = jnp.mean(xt * xt, axis=0, keepdims=True)
    return xt * lax.rsqrt(ms + EPS) * gcol


def _dot(a, b):
    return jnp.dot(a, b, preferred_element_type=F32)


def _dot_nt(a, b):
    return lax.dot_general(a, b, (((1,), (1,)), ((), ())), preferred_element_type=F32)


def _dot_tn(a, b):
    return lax.dot_general(a, b, (((0,), (0,)), ((), ())), preferred_element_type=F32)


def _mod_kernel(c_ref, w_ref, b_ref, o_ref):
    c = c_ref[...]
    ca = c * _sigmoid(c)
    o_ref[...] = _dot(ca, w_ref[...]) + b_ref[...]


def _modulation(c_all, w_ada, b_ada):
    nb = c_all.shape[0]
    tn = 1536
    return pl.pallas_call(
        _mod_kernel,
        out_shape=jax.ShapeDtypeStruct((DEPTH, nb, 6 * D_MODEL), F32),
        grid=(DEPTH, 6 * D_MODEL // tn),
        in_specs=[
            pl.BlockSpec((nb, D_MODEL), lambda l, j: (0, 0)),
            pl.BlockSpec((None, D_MODEL, tn), lambda l, j: (l, 0, j)),
            pl.BlockSpec((None, 1, tn), lambda l, j: (l, 0, j)),
        ],
        out_specs=pl.BlockSpec((None, nb, tn), lambda l, j: (l, 0, j)),
        compiler_params=_cparams(("arbitrary", "arbitrary")),
        name="adaln_mod",
    )(c_all, w_ada, b_ada.reshape(DEPTH, 1, 6 * D_MODEL))


def _rope_t(xt, c, s):
    n = c.shape[0]
    x1, x2 = xt[:n], xt[n:]
    return jnp.concatenate([x1 * c - x2 * s, x1 * s + x2 * c], axis=0)


def _axial_t(xt, cr, sr, cc, sc):
    d2 = xt.shape[0] // 2
    return jnp.concatenate([_rope_t(xt[:d2], cr, sr), _rope_t(xt[d2:], cc, sc)], axis=0)


def _pre_attn_kernel(x_ref, mod_ref, gmix_ref, tab_ref, wint_ref, gqa_ref, gka_ref, gcq_ref,
                     wuqt_ref, gckv_ref, wukvt_ref, q_ref, k_ref, v_ref):
    x = x_ref[...]
    tm = x.shape[0]
    mod = mod_ref[...]
    h = _rms_rows(x, gmix_ref[...]) * (1.0 + mod[1:2]) + mod[0:1]
    zt = _dot_nt(wint_ref[...], h.astype(BF16))

    tab = tab_ref[...]
    ra = (tab[0:16], tab[16:32], tab[32:48], tab[48:64])
    rb = (tab[64:72], tab[72:80], tab[80:88], tab[88:96])
    zeros64 = jnp.zeros((HEAD_DIM, tm), F32)
    zeros32 = jnp.zeros((QK_PAD - D_NOPE - D_ROPE, tm), F32)

    o_q, o_k, o_v, o_cq, o_ckv, o_kr = IN_OFFS[:6]
    gqa = gqa_ref[...]
    for hh in range(N_HEADS_A):
        qh = _axial_t(_rms_cols(zt[o_q + HEAD_DIM * hh:o_q + HEAD_DIM * (hh + 1)], gqa), *ra) * SCALE_A
        parts = [qh, zeros64] if hh // (N_HEADS_A // N_KV_A) == 0 else [zeros64, qh]
        q_ref[hh] = jnp.concatenate(parts, axis=0).astype(BF16)
    gka = gka_ref[...]
    kts = [jnp.concatenate(
        [_axial_t(_rms_cols(zt[o_k + HEAD_DIM * g:o_k + HEAD_DIM * (g + 1)], gka), *ra)
         for g in range(N_KV_A)], axis=0)]
    ones16 = jnp.ones((V_ROWS - D_V, tm), F32)
    for g in range(N_KV_A):
        v_ref[g] = jnp.concatenate(
            [zt[o_v + HEAD_DIM * g:o_v + HEAD_DIM * (g + 1)], ones16], axis=0).astype(BF16)

    cqn = _rms_cols(zt[o_cq:o_ckv], gcq_ref[...]).astype(BF16)
    qbt = _dot(wuqt_ref[...], cqn)
    dq = D_NOPE + D_ROPE
    for hh in range(N_HEADS_B):
        nope = qbt[dq * hh:dq * hh + D_NOPE]
        rp = _axial_t(qbt[dq * hh + D_NOPE:dq * (hh + 1)], *rb)
        q_ref[N_HEADS_A + hh] = (jnp.concatenate([nope, rp, zeros32], axis=0) * SCALE_B).astype(BF16)
    ckvn = _rms_cols(zt[o_ckv:o_kr], gckv_ref[...]).astype(BF16)
    kvt = _dot(wukvt_ref[...], ckvn)
    kr = _axial_t(zt[o_kr:o_kr + D_ROPE], *rb)
    dkv = D_NOPE + D_V
    for hh in range(N_HEADS_B):
        kts.append(jnp.concatenate([kvt[dkv * hh:dkv * hh + D_NOPE], kr, zeros32], axis=0))
        v_ref[N_KV_A + hh] = jnp.concatenate(
            [kvt[dkv * hh + D_NOPE:dkv * (hh + 1)], ones16], axis=0).astype(BF16)

    k_all = jnp.concatenate(kts, axis=0).T
    for i in range(N_KSLOTS):
        k_ref[i] = k_all[:, QK_PAD * i:QK_PAD * (i + 1)].astype(BF16)


def _pre_attn(x, mod_l, g_mix, tab, wint, gqa, gka, gcq, wuqt, gckv, wukvt):
    B, S, _ = x.shape
    tm = TOKEN_TILE
    nt = S // tm
    const = lambda b, i: (0, 0)
    return pl.pallas_call(
        _pre_attn_kernel,
        out_shape=(
            jax.ShapeDtypeStruct((B, N_HEADS, QK_PAD, S), BF16),
            jax.ShapeDtypeStruct((B, N_KSLOTS, S, QK_PAD), BF16),
            jax.ShapeDtypeStruct((B, N_VSLOTS, nt, V_ROWS, tm), BF16),
        ),
        grid=(B, nt),
        in_specs=[
            pl.BlockSpec((None, tm, D_MODEL), lambda b, i: (b, i, 0)),
            pl.BlockSpec((None, 6, D_MODEL), lambda b, i: (b, 0, 0)),
            pl.BlockSpec((1, D_MODEL), const),
            pl.BlockSpec((96, tm), lambda b, i: (0, i)),
            pl.BlockSpec((IN_WIDTH, D_MODEL), const),
            pl.BlockSpec((HEAD_DIM, 1), const),
            pl.BlockSpec((HEAD_DIM, 1), const),
            pl.BlockSpec((D_CQ, 1), const),
            pl.BlockSpec((N_HEADS_B * (D_NOPE + D_ROPE), D_CQ), const),
            pl.BlockSpec((D_CKV, 1), const),
            pl.BlockSpec((N_HEADS_B * (D_NOPE + D_V), D_CKV), const),
        ],
        out_specs=(
            pl.BlockSpec((None, N_HEADS, QK_PAD, tm), lambda b, i: (b, 0, 0, i)),
            pl.BlockSpec((None, N_KSLOTS, tm, QK_PAD), lambda b, i: (b, 0, i, 0)),
            pl.BlockSpec((None, N_VSLOTS, None, V_ROWS, tm), lambda b, i: (b, 0, i, 0, 0)),
        ),
        compiler_params=_cparams(("parallel", "parallel")),
        name="pre_attn",
    )(x, mod_l, g_mix, tab, wint, gqa, gka, gcq, wuqt, gckv, wukvt)


def _attn_kernel(q_ref, k_ref, v_ref, o_ref, sa_scr, sb_scr, *, n_chunks, tk, trip):
    qt = q_ref[...]
    tq = qt.shape[1]
    bufs = (sa_scr, sb_scr)

    def scores(c, s_ref):
        off = c * tk if isinstance(c, int) else pl.multiple_of(c * tk, tk)
        s = _dot(k_ref[pl.ds(off, tk), :], qt)
        s_ref[...] = s
        return jnp.max(s, axis=0, keepdims=True)

    def update(c, s_ref, cm, m, acc):
        m_new = jnp.maximum(m, cm)
        p = jnp.exp2(s_ref[...] - m_new).astype(BF16)
        return m_new, acc * jnp.exp2(m - m_new) + _dot(v_ref[c], p)

    def steps(c0, count, cm, m, acc, last):
        for u in range(count):
            if not (last and u == count - 1):
                cm_next = scores(c0 + u + 1, bufs[(u + 1) % 2])
            m, acc = update(c0 + u, bufs[u % 2], cm, m, acc)
            cm = cm_next
        return cm, m, acc

    def body(i, carry):
        return steps(i * trip, trip, *carry, last=False)

    m0 = jnp.full((1, tq), NEG_BIG, F32)
    acc0 = jnp.zeros((V_ROWS, tq), F32)
    n_trips = n_chunks // trip - 1
    carry = lax.fori_loop(0, n_trips, body, (scores(0, sa_scr), m0, acc0))
    _, _, acc = steps(n_trips * trip, trip, *carry, last=True)
    o_ref[...] = (acc[:D_V] * (1.0 / acc[D_V:D_V + 1])).astype(o_ref.dtype)


def _attention(qt, k, vt):
    B, _, _, S = qt.shape
    n_chunks, tk = vt.shape[2], vt.shape[4]
    tq = Q_TILE
    kslot = lambda h: jnp.where(h < N_HEADS_A, 0, h - (N_HEADS_A - 1))
    vslot = lambda h: jnp.where(h < N_HEADS_A, h // (N_HEADS_A // N_KV_A), h - (N_HEADS_A - N_KV_A))
    return pl.pallas_call(
        functools.partial(_attn_kernel, n_chunks=n_chunks, tk=tk,
                          trip=4 if n_chunks % 4 == 0 and n_chunks >= 8 else 2),
        out_shape=jax.ShapeDtypeStruct((B, N_HEADS, D_V, S), BF16),
        grid=(B, N_HEADS, S // tq),
        in_specs=[
            pl.BlockSpec((None, None, QK_PAD, tq), lambda b, h, i: (b, h, 0, i)),
            pl.BlockSpec((None, None, S, QK_PAD), lambda b, h, i: (b, kslot(h), 0, 0)),
            pl.BlockSpec((None, None, n_chunks, V_ROWS, tk), lambda b, h, i: (b, vslot(h), 0, 0, 0)),
        ],
        out_specs=pl.BlockSpec((None, None, D_V, tq), lambda b, h, i: (b, h, 0, i)),
        scratch_shapes=[pltpu.VMEM((tk, tq), F32), pltpu.VMEM((tk, tq), F32)],
        compiler_params=_cparams(("parallel", "parallel", "arbitrary")),
        name="attn_sweep",
    )(qt, k, vt)


def _post_attn_kernel(o_ref, x_ref, mod_ref, wo_ref, out_ref):
    tm = x_ref.shape[0]
    ot = o_ref[...].reshape(N_HEADS * D_V, tm)
    m = _dot_tn(ot, wo_ref[...])
    out_ref[...] = x_ref[...] + mod_ref[...][2:3] * m


def _post_attn(ot, x, mod_l, wo):
    B, S, _ = x.shape
    tm = TOKEN_TILE
    return pl.pallas_call(
        _post_attn_kernel,
        out_shape=jax.ShapeDtypeStruct(x.shape, F32),
        grid=(B, S // tm),
        in_specs=[
            pl.BlockSpec((None, N_HEADS, D_V, tm), lambda b, i: (b, 0, 0, i)),
            pl.BlockSpec((None, tm, D_MODEL), lambda b, i: (b, i, 0)),
            pl.BlockSpec((None, 6, D_MODEL), lambda b, i: (b, 0, 0)),
            pl.BlockSpec((D_MODEL, D_MODEL), lambda b, i: (0, 0)),
        ],
        out_specs=pl.BlockSpec((None, tm, D_MODEL), lambda b, i: (b, i, 0)),
        compiler_params=_cparams(("parallel", "parallel")),
        name="post_attn",
    )(ot, x, mod_l, wo)


def _f1_kernel(x_ref, mod_ref, gmix_ref, fc_ref, m1_ref, twr_ref, twi_ref, y_ref, z_scr):
    n1 = x_ref.shape[0]
    x = x_ref[...].reshape(n1 * FFT_T, D_MODEL)
    mod = mod_ref[...]
    h = _rms_rows(x, gmix_ref[...]) * (1.0 + mod[1:2]) + mod[0:1]
    hb = h.astype(BF16)
    fc = fc_ref[...]
    for g in range(N_FGROUPS):
        zg = _dot(hb[:, FGROUP * g:FGROUP * (g + 1)], fc)
        z_scr[:, :, FGROUP * g:FGROUP * (g + 1)] = zg[:, :FGROUP].reshape(n1, FFT_T, FGROUP)
        z_scr[:, :, D_MODEL + FGROUP * g:D_MODEL + FGROUP * (g + 1)] = zg[:, FGROUP:].reshape(n1, FFT_T, FGROUP)
    m1 = m1_ref[...]
    for j in range(FFT_T):
        zj = z_scr[:, j, :]
        st = jnp.concatenate([zj[:, :D_MODEL], zj[:, D_MODEL:]], axis=0).astype(BF16)
        y = _dot(m1, st)
        yr, yi = y[:n1], y[n1:]
        tr = jnp.tile(twr_ref[j], (1, D_MODEL // 128))
        ti = jnp.tile(twi_ref[j], (1, D_MODEL // 128))
        y_ref[:, j, 0:D_MODEL] = yr * tr - yi * ti
        y_ref[:, j, D_MODEL:2 * D_MODEL] = yr * ti + yi * tr


def _f2_kernel(y_ref, x_ref, mod_ref, cs_ref, wc_ref, o_ref, f_scr):
    cs = cs_ref[...]
    for j in range(FFT_T):
        yk = y_ref[j]
        st = jnp.concatenate([yk[:, :D_MODEL], yk[:, D_MODEL:]], axis=0).astype(BF16)
        f_scr[FFT_N2 * j:FFT_N2 * (j + 1), :] = _dot(cs, st).astype(BF16)
    m = _dot(f_scr[...], wc_ref[...])
    gt1 = mod_ref[...][2:3]
    for j in range(FFT_T):
        o_ref[:, j, :] = x_ref[:, j, :] + gt1 * m[FFT_N2 * j:FFT_N2 * (j + 1)]


def _dft_constants(S):
    n1 = S // FFT_N2
    k = np.arange(FGROUP)
    ang = 2.0 * np.pi * np.outer(k, k) / FGROUP
    fc = np.concatenate([np.cos(ang), -np.sin(ang)], axis=1) / math.sqrt(FGROUP)
    a = np.arange(n1)
    ang1 = 2.0 * np.pi * np.outer(a, a) / n1
    c1, s1 = np.cos(ang1), np.sin(ang1)
    m1 = np.block([[c1, s1], [-s1, c1]]) / math.sqrt(n1)
    b = np.arange(FFT_N2)
    ang2 = 2.0 * np.pi * np.outer(b, b) / FFT_N2
    cs = np.concatenate([np.cos(ang2), np.sin(ang2)], axis=1) / math.sqrt(FFT_N2)
    angt = 2.0 * np.pi * np.outer(b, a) / S
    twr = np.broadcast_to(np.cos(angt)[:, :, None], (FFT_N2, n1, 128))
    twi = np.broadcast_to(-np.sin(angt)[:, :, None], (FFT_N2, n1, 128))
    return (jnp.asarray(fc, BF16), jnp.asarray(m1, BF16), jnp.asarray(cs, BF16),
            jnp.asarray(twr, F32), jnp.asarray(twi, F32))


def _fourier_layer(x, mod_l, g_mix, wc):
    B, S, _ = x.shape
    n1 = S // FFT_N2
    fc, m1, cs, twr, twi = _dft_constants(S)
    const2 = lambda b, i: (0, 0)
    y = pl.pallas_call(
        _f1_kernel,
        out_shape=jax.ShapeDtypeStruct((B, n1, FFT_N2, 2 * D_MODEL), F32),
        grid=(B, FFT_N2 // FFT_T),
        in_specs=[
            pl.BlockSpec((None, n1, FFT_T, D_MODEL), lambda b, i: (b, 0, i, 0)),
            pl.BlockSpec((None, 6, D_MODEL), lambda b, i: (b, 0, 0)),
            pl.BlockSpec((1, D_MODEL), const2),
            pl.BlockSpec((FGROUP, 2 * FGROUP), const2),
            pl.BlockSpec((2 * n1, 2 * n1), const2),
            pl.BlockSpec((FFT_T, n1, 128), lambda b, i: (i, 0, 0)),
            pl.BlockSpec((FFT_T, n1, 128), lambda b, i: (i, 0, 0)),
        ],
        out_specs=pl.BlockSpec((None, n1, FFT_T, 2 * D_MODEL), lambda b, i: (b, 0, i, 0)),
        scratch_shapes=[pltpu.VMEM((n1, FFT_T, 2 * D_MODEL), F32)],
        compiler_params=_cparams(("parallel", "parallel")),
        name="fourier_stage1",
    )(x.reshape(B, n1, FFT_N2, D_MODEL), mod_l, g_mix, fc, m1, twr, twi)
    out = pl.pallas_call(
        _f2_kernel,
        out_shape=jax.ShapeDtypeStruct((B, FFT_N2, n1, D_MODEL), F32),
        grid=(B, n1 // FFT_T),
        in_specs=[
            pl.BlockSpec((None, FFT_T, FFT_N2, 2 * D_MODEL), lambda b, i: (b, i, 0, 0)),
            pl.BlockSpec((None, FFT_N2, FFT_T, D_MODEL), lambda b, i: (b, 0, i, 0)),
            pl.BlockSpec((None, 6, D_MODEL), lambda b, i: (b, 0, 0)),
            pl.BlockSpec((FFT_N2, 2 * FFT_N2), const2),
            pl.BlockSpec((D_MODEL, D_MODEL), const2),
        ],
        out_specs=pl.BlockSpec((None, FFT_N2, FFT_T, D_MODEL), lambda b, i: (b, 0, i, 0)),
        scratch_shapes=[pltpu.VMEM((FFT_T * FFT_N2, D_MODEL), BF16)],
        compiler_params=_cparams(("parallel", "parallel")),
        name="fourier_stage2",
    )(y, x.reshape(B, FFT_N2, n1, D_MODEL), mod_l, cs, wc)
    return out.reshape(B, S, D_MODEL)


def _route_t(s, sb):
    rows = [sb[e:e + 1] for e in range(N_EXPERTS)]
    gscore = []
    for g in range(N_EGROUPS):
        a, b, c, d = rows[4 * g:4 * g + 4]
        gscore.append(jnp.maximum(jnp.maximum(jnp.maximum(a + b, a + c), jnp.maximum(a + d, b + c)),
                                  jnp.maximum(b + d, c + d)))
    best = gscore[0]
    gsel = jnp.zeros_like(best)
    for g in range(1, N_EGROUPS):
        upd = gscore[g] > best
        gsel = jnp.where(upd, float(g), gsel)
        best = jnp.where(upd, gscore[g], best)
    picked = []
    for e in range(N_EXPERTS):
        g = e // EXPERTS_PER_GROUP
        rank = jnp.zeros_like(best)
        for j in range(4 * g, 4 * g + 4):
            if j == e:
                continue
            ahead = (rows[j] >= rows[e]) if j < e else (rows[j] > rows[e])
            rank = rank + jnp.where(ahead, 1.0, 0.0)
        sel = jnp.where(rank < 1.5, 1.0, 0.0) * jnp.where(gsel == float(g), 1.0, 0.0)
        picked.append(sel * s[e:e + 1])
    total = picked[0]
    for e in range(1, N_EXPERTS):
        total = total + picked[e]
    inv = 1.0 / total
    return [p * inv for p in picked]


def _moe_kernel(x_ref, mod_ref, gffn_ref, wr_ref, bias_ref, wg_ref, wu_ref, wd_ref, gfin_ref,
                out_ref, hb_scr, gate_scr, acc_scr, *, final):
    g = pl.program_id(2)
    tm = x_ref.shape[0]

    @pl.when(g == 0)
    def _():
        mod = mod_ref[...]
        t = _rms_rows(x_ref[...], gffn_ref[...]) * (1.0 + mod[4:5]) + mod[3:4]
        t_hi = t.astype(BF16)
        t_lo = (t - t_hi.astype(F32)).astype(BF16)
        wr = wr_ref[...]
        a = _dot_nt(wr, t_hi)
        b = _dot_nt(wr[0:N_EXPERTS], t_lo)
        logits = a[0:N_EXPERTS] + a[N_EXPERTS:2 * N_EXPERTS] + b
        s = _sigmoid(logits)
        gates = _route_t(s, s + bias_ref[...])
        zpad = jnp.zeros((128 - EXPERTS_PER_GROUP, tm), F32)
        for gg in range(N_EGROUPS):
            gt = jnp.concatenate(gates[4 * gg:4 * gg + 4] + [zpad], axis=0)
            gate_scr[gg] = gt.T
        hb_scr[...] = t_hi
        acc_scr[...] = jnp.zeros_like(acc_scr)

    hb = hb_scr[...]
    u1 = _dot(hb, wg_ref[...])
    u2 = _dot(hb, wu_ref[...])
    he = u1 * _sigmoid(u1) * u2
    gcols = gate_scr[g]
    he = jnp.concatenate(
        [he[:, D_EXPERT * j:D_EXPERT * (j + 1)] * gcols[:, j:j + 1] for j in range(EXPERTS_PER_GROUP)],
        axis=1)
    acc_scr[...] += _dot(he.astype(BF16), wd_ref[...])

    @pl.when(g == N_EGROUPS - 1)
    def _():
        y = x_ref[...] + mod_ref[...][5:6] * acc_scr[...]
        if final:
            y = _rms_rows(y, gfin_ref[...])
        out_ref[...] = y


def _moe(x, mod_l, g_ffn, wr, bias, wg, wu, wd, g_final, final):
    B, S, _ = x.shape
    tm = MOE_TILE
    const = lambda b, i, g: (0, 0)
    wspec = pl.BlockSpec((None, D_MODEL, D_MODEL), lambda b, i, g: (g, 0, 0))
    return pl.pallas_call(
        functools.partial(_moe_kernel, final=final),
        out_shape=jax.ShapeDtypeStruct(x.shape, F32),
        grid=(B, S // tm, N_EGROUPS),
        in_specs=[
            pl.BlockSpec((None, tm, D_MODEL), lambda b, i, g: (b, i, 0)),
            pl.BlockSpec((None, 6, D_MODEL), lambda b, i, g: (b, 0, 0)),
            pl.BlockSpec((1, D_MODEL), const),
            pl.BlockSpec((2 * N_EXPERTS, D_MODEL), const),
            pl.BlockSpec((N_EXPERTS, 1), const),
            wspec, wspec, wspec,
            pl.BlockSpec((1, D_MODEL), const),
        ],
        out_specs=pl.BlockSpec((None, tm, D_MODEL), lambda b, i, g: (b, i, 0)),
        scratch_shapes=[
            pltpu.VMEM((tm, D_MODEL), BF16),
            pltpu.VMEM((N_EGROUPS, tm, 128), F32),
            pltpu.VMEM((tm, D_MODEL), F32),
        ],
        compiler_params=_cparams(("parallel", "parallel", "arbitrary")),
        name="moe",
    )(x, mod_l, g_ffn, wr, bias, wg, wu, wd, g_final)


def _rope_tables(S):
    pos = jnp.arange(S, dtype=jnp.int32)
    rows = (pos // GRID_W).astype(F32)[None, :]
    cols = (pos % GRID_W).astype(F32)[None, :]
    tabs = []
    for half in (HEAD_DIM // 2, D_ROPE // 2):
        inv = (ROPE_THETA ** (-jnp.arange(0, half, 2, dtype=F32) / half))[:, None]
        for p in (rows, cols):
            ang = inv * p
            tabs += [jnp.cos(ang), jnp.sin(ang)]
    return jnp.concatenate(tabs, axis=0)


def _group_experts(w, transpose_cols):
    e, a, b = w.shape
    w = w.reshape(N_EGROUPS, EXPERTS_PER_GROUP, a, b)
    if transpose_cols:
        return w.transpose(0, 2, 1, 3).reshape(N_EGROUPS, a, EXPERTS_PER_GROUP * b).astype(BF16)
    return w.reshape(N_EGROUPS, EXPERTS_PER_GROUP * a, b).astype(BF16)


def _trunk(x, mod, p):
    B, S, _ = x.shape
    tab = _rope_tables(S)
    for l in range(DEPTH):
        mod_l = mod[l]
        g_mix = p["g_mix"][l][None, :]
        if l % 2 == 0:
            i = l // 2
            qt, k, vt = _pre_attn(x, mod_l, g_mix, tab, p["wint"][i], p["g_qa"][i][:, None],
                                  p["g_ka"][i][:, None], p["g_cq"][i][:, None], p["wuqt"][i],
                                  p["g_ckv"][i][:, None], p["wukvt"][i])
            ot = _attention(qt, k, vt)
            x = _post_attn(ot, x, mod_l, p["wo"][i])
        else:
            x = _fourier_layer(x, mod_l, g_mix, p["wc"][l // 2])
        x = _moe(x, mod_l, p["g_ffn"][l][None, :], p["wr"], p["bias"], p["wg"][l], p["wu"][l],
                 p["wd"][l], p["g_final"], final=(l == DEPTH - 1))
    return x


def kernel(x_prompt, x_sample, c_prompt, c_sample, w_ada, b_ada, g_mix, g_ffn, w_in, g_qa, g_ka, g_cq,
           w_uq, g_ckv, w_ukv, w_o_attn, w_fourier, w_router, router_bias, w_gate, w_up, w_down, g_final):
    nbp = x_prompt.shape[0]
    mod = _modulation(jnp.concatenate([c_prompt, c_sample], axis=0), w_ada, b_ada)
    mod = mod.reshape(DEPTH, -1, 6, D_MODEL)
    wrt = w_router.T
    wr_hi = wrt.astype(BF16)
    wr_lo = (wrt - wr_hi.astype(F32)).astype(BF16)
    p = {
        "g_mix": g_mix, "g_ffn": g_ffn, "g_qa": g_qa, "g_ka": g_ka, "g_cq": g_cq, "g_ckv": g_ckv,
        "wint": jnp.swapaxes(w_in, 1, 2).astype(BF16),
        "wuqt": jnp.swapaxes(w_uq, 1, 2).astype(BF16),
        "wukvt": jnp.swapaxes(w_ukv, 1, 2).astype(BF16),
        "wo": w_o_attn.astype(BF16),
        "wc": w_fourier.astype(BF16),
        "wr": jnp.concatenate([wr_hi, wr_lo], axis=0),
        "bias": router_bias[:, None],
        "wg": [_group_experts(w_gate[l], True) for l in range(DEPTH)],
        "wu": [_group_experts(w_up[l], True) for l in range(DEPTH)],
        "wd": [_group_experts(w_down[l], False) for l in range(DEPTH)],
        "g_final": g_final[None, :],
    }
    y_prompt = _trunk(x_prompt, mod[:, :nbp], p)
    y_sample = _trunk(x_sample, mod[:, nbp:], p)
    return (y_prompt, y_sample)
```

```python
import functools
import math

import numpy as np
import jax
import jax.numpy as jnp
from jax import lax
from jax.experimental import pallas as pl
from jax.experimental.pallas import tpu as pltpu

F32 = jnp.float32
BF16 = jnp.bfloat16

D_MODEL = 1024
DEPTH = 4
GRID_W = 64
HEAD_DIM = 64
N_HEADS_A = 8
N_KV_A = 2
N_HEADS_B = 8
D_NOPE = 64
D_ROPE = 32
D_V = 64
D_CQ = 384
D_CKV = 256
N_FGROUPS = 4
FGROUP = D_MODEL // N_FGROUPS
ROPE_THETA = 10000.0
N_EXPERTS = 16
N_EGROUPS = 4
EXPERTS_PER_GROUP = 4
D_EXPERT = 256
EPS = 1e-6
IN_SIZES = (N_HEADS_A * HEAD_DIM, N_KV_A * HEAD_DIM, N_KV_A * HEAD_DIM, D_CQ, D_CKV, D_ROPE)
IN_WIDTH = sum(IN_SIZES)
IN_OFFS = tuple(int(v) for v in np.cumsum((0,) + IN_SIZES))

N_HEADS = N_HEADS_A + N_HEADS_B
N_KSLOTS = 1 + N_HEADS_B
N_VSLOTS = N_KV_A + N_HEADS_B
QK_PAD = 128
LOG2E = 1.4426950408889634
SCALE_A = HEAD_DIM ** -0.5 * LOG2E
SCALE_B = (D_NOPE + D_ROPE) ** -0.5 * LOG2E
NEG_BIG = -1e30

FFT_N2 = 128
FFT_T = 8

TOKEN_TILE = 512
Q_TILE = 512
V_ROWS = D_V + 16
MOE_TILE = 512
VMEM_LIMIT = 56 * 1024 * 1024


def _cparams(sem, flags=None):
    return pltpu.CompilerParams(dimension_semantics=sem, vmem_limit_bytes=VMEM_LIMIT, flags=flags)


def _sigmoid(x):
    return 1.0 / (1.0 + jnp.exp(-x))


def _rms_rows(x, g):
    ms = jnp.mean(x * x, axis=-1, keepdims=True)
    return x * lax.rsqrt(ms + EPS) * g


def _rms_cols(xt, gcol):
    ms = jnp.mean(xt * xt, axis=0, keepdims=True)
    return xt * lax.rsqrt(ms + EPS) * gcol


def _dot(a, b):
    return jnp.dot(a, b, preferred_element_type=F32)


def _dot_nt(a, b):
    return lax.dot_general(a, b, (((1,), (1,)), ((), ())), preferred_element_type=F32)


def _dot_tn(a, b):
    return lax.dot_general(a, b, (((0,), (0,)), ((), ())), preferred_element_type=F32)


def _mod_kernel(c_ref, w_ref, b_ref, o_ref):
    c = c_ref[...]
    ca = c * _sigmoid(c)
    o_ref[...] = _dot(ca, w_ref[...]) + b_ref[...]


def _modulation(c_all, w_ada, b_ada):
    nb = c_all.shape[0]
    tn = 1536
    return pl.pallas_call(
        _mod_kernel,
        out_shape=jax.ShapeDtypeStruct((DEPTH, nb, 6 * D_MODEL), F32),
        grid=(DEPTH, 6 * D_MODEL // tn),
        in_specs=[
            pl.BlockSpec((nb, D_MODEL), lambda l, j: (0, 0)),
            pl.BlockSpec((None, D_MODEL, tn), lambda l, j: (l, 0, j)),
            pl.BlockSpec((None, 1, tn), lambda l, j: (l, 0, j)),
        ],
        out_specs=pl.BlockSpec((None, nb, tn), lambda l, j: (l, 0, j)),
        compiler_params=_cparams(("arbitrary", "arbitrary")),
        name="adaln_mod",
    )(c_all, w_ada, b_ada.reshape(DEPTH, 1, 6 * D_MODEL))


def _rope_t(xt, c, s):
    n = c.shape[0]
    x1, x2 = xt[:n], xt[n:]
    return jnp.concatenate([x1 * c - x2 * s, x1 * s + x2 * c], axis=0)


def _axial_t(xt, cr, sr, cc, sc):
    d2 = xt.shape[0] // 2
    return jnp.concatenate([_rope_t(xt[:d2], cr, sr), _rope_t(xt[d2:], cc, sc)], axis=0)


def _pre_attn_kernel(x_ref, mod_ref, gmix_ref, tab_ref, wint_ref, gqa_ref, gka_ref, gcq_ref,
                     wuqt_ref, gckv_ref, wukvt_ref, q_ref, k_ref, v_ref):
    x = x_ref[...]
    tm = x.shape[0]
    mod = mod_ref[...]
    h = _rms_rows(x, gmix_ref[...]) * (1.0 + mod[1:2]) + mod[0:1]
    zt = _dot_nt(wint_ref[...], h.astype(BF16))

    tab = tab_ref[...]
    ra = (tab[0:16], tab[16:32], tab[32:48], tab[48:64])
    rb = (tab[64:72], tab[72:80], tab[80:88], tab[88:96])
    zeros64 = jnp.zeros((HEAD_DIM, tm), F32)
    zeros32 = jnp.zeros((QK_PAD - D_NOPE - D_ROPE, tm), F32)

    o_q, o_k, o_v, o_cq, o_ckv, o_kr = IN_OFFS[:6]
    gqa = gqa_ref[...]
    for hh in range(N_HEADS_A):
        qh = _axial_t(_rms_cols(zt[o_q + HEAD_DIM * hh:o_q + HEAD_DIM * (hh + 1)], gqa), *ra) * SCALE_A
        parts = [qh, zeros64] if hh // (N_HEADS_A // N_KV_A) == 0 else [zeros64, qh]
        q_ref[hh] = jnp.concatenate(parts, axis=0).astype(BF16)
    gka = gka_ref[...]
    kts = [jnp.concatenate(
        [_axial_t(_rms_cols(zt[o_k + HEAD_DIM * g:o_k + HEAD_DIM * (g + 1)], gka), *ra)
         for g in range(N_KV_A)], axis=0)]
    ones16 = jnp.ones((V_ROWS - D_V, tm), F32)
    for g in range(N_KV_A):
        v_ref[g] = jnp.concatenate(
            [zt[o_v + HEAD_DIM * g:o_v + HEAD_DIM * (g + 1)], ones16], axis=0).astype(BF16)

    cqn = _rms_cols(zt[o_cq:o_ckv], gcq_ref[...]).astype(BF16)
    qbt = _dot(wuqt_ref[...], cqn)
    dq = D_NOPE + D_ROPE
    for hh in range(N_HEADS_B):
        nope = qbt[dq * hh:dq * hh + D_NOPE]
        rp = _axial_t(qbt[dq * hh + D_NOPE:dq * (hh + 1)], *rb)
        q_ref[N_HEADS_A + hh] = (jnp.concatenate([nope, rp, zeros32], axis=0) * SCALE_B).astype(BF16)
    ckvn = _rms_cols(zt[o_ckv:o_kr], gckv_ref[...]).astype(BF16)
    kvt = _dot(wukvt_ref[...], ckvn)
    kr = _axial_t(zt[o_kr:o_kr + D_ROPE], *rb)
    dkv = D_NOPE + D_V
    for hh in range(N_HEADS_B):
        kts.append(jnp.concatenate([kvt[dkv * hh:dkv * hh + D_NOPE], kr, zeros32], axis=0))
        v_ref[N_KV_A + hh] = jnp.concatenate(
            [kvt[dkv * hh + D_NOPE:dkv * (hh + 1)], ones16], axis=0).astype(BF16)

    k_all = jnp.concatenate(kts, axis=0).T
    for i in range(N_KSLOTS):
        k_ref[i] = k_all[:, QK_PAD * i:QK_PAD * (i + 1)].astype(BF16)


def _pre_attn(x, mod_l, g_mix, tab, wint, gqa, gka, gcq, wuqt, gckv, wukvt):
    B, S, _ = x.shape
    tm = TOKEN_TILE
    nt = S // tm
    const = lambda b, i: (0, 0)
    return pl.pallas_call(
        _pre_attn_kernel,
        out_shape=(
            jax.ShapeDtypeStruct((B, N_HEADS, QK_PAD, S), BF16),
            jax.ShapeDtypeStruct((B, N_KSLOTS, S, QK_PAD), BF16),
            jax.ShapeDtypeStruct((B, N_VSLOTS, nt, V_ROWS, tm), BF16),
        ),
        grid=(B, nt),
        in_specs=[
            pl.BlockSpec((None, tm, D_MODEL), lambda b, i: (b, i, 0)),
            pl.BlockSpec((None, 6, D_MODEL), lambda b, i: (b, 0, 0)),
            pl.BlockSpec((1, D_MODEL), const),
            pl.BlockSpec((96, tm), lambda b, i: (0, i)),
            pl.BlockSpec((IN_WIDTH, D_MODEL), const),
            pl.BlockSpec((HEAD_DIM, 1), const),
            pl.BlockSpec((HEAD_DIM, 1), const),
            pl.BlockSpec((D_CQ, 1), const),
            pl.BlockSpec((N_HEADS_B * (D_NOPE + D_ROPE), D_CQ), const),
            pl.BlockSpec((D_CKV, 1), const),
            pl.BlockSpec((N_HEADS_B * (D_NOPE + D_V), D_CKV), const),
        ],
        out_specs=(
            pl.BlockSpec((None, N_HEADS, QK_PAD, tm), lambda b, i: (b, 0, 0, i)),
            pl.BlockSpec((None, N_KSLOTS, tm, QK_PAD), lambda b, i: (b, 0, i, 0)),
            pl.BlockSpec((None, N_VSLOTS, None, V_ROWS, tm), lambda b, i: (b, 0, i, 0, 0)),
        ),
        compiler_params=_cparams(("parallel", "parallel")),
        name="pre_attn",
    )(x, mod_l, g_mix, tab, wint, gqa, gka, gcq, wuqt, gckv, wukvt)


def _attn_kernel(q_ref, k_ref, v_ref, o_ref, sa_scr, sb_scr, *, n_chunks, tk, trip):
    qt = q_ref[...]
    tq = qt.shape[1]
    bufs = (sa_scr, sb_scr)

    def scores(c, s_ref):
        off = c * tk if isinstance(c, int) else pl.multiple_of(c * tk, tk)
        s = _dot(k_ref[pl.ds(off, tk), :], qt)
        s_ref[:, 0:tq] = s
        return jnp.max(s, axis=0, keepdims=True)

    def update(c, s_ref, cm, m, acc):
        m_new = jnp.maximum(m, cm)
        p = jnp.exp2(s_ref[:, 0:tq] - m_new).astype(BF16)
        return m_new, acc * jnp.exp2(m - m_new) + _dot(v_ref[c], p)

    def steps(c0, count, cm, m, acc, last):
        for u in range(count):
            if not (last and u == count - 1):
                cm_next = scores(c0 + u + 1, bufs[(u + 1) % 2])
            m, acc = update(c0 + u, bufs[u % 2], cm, m, acc)
            cm = cm_next
        return cm, m, acc

    def body(i, carry):
        return steps(i * trip, trip, *carry, last=False)

    m0 = jnp.full((1, tq), NEG_BIG, F32)
    acc0 = jnp.zeros((V_ROWS, tq), F32)
    n_trips = n_chunks // trip - 1
    carry = lax.fori_loop(0, n_trips, body, (scores(0, sa_scr), m0, acc0))
    _, _, acc = steps(n_trips * trip, trip, *carry, last=True)
    o_ref[...] = (acc[:D_V] * (1.0 / acc[D_V:D_V + 1])).astype(o_ref.dtype)


def _attention(qt, k, vt):
    B, _, _, S = qt.shape
    n_chunks, tk = vt.shape[2], vt.shape[4]
    tq = Q_TILE
    kslot = lambda h: jnp.where(h < N_HEADS_A, 0, h - (N_HEADS_A - 1))
    vslot = lambda h: jnp.where(h < N_HEADS_A, h // (N_HEADS_A // N_KV_A), h - (N_HEADS_A - N_KV_A))
    return pl.pallas_call(
        functools.partial(_attn_kernel, n_chunks=n_chunks, tk=tk,
                          trip=4 if n_chunks % 4 == 0 and n_chunks >= 8 else 2),
        out_shape=jax.ShapeDtypeStruct((B, N_HEADS, D_V, S), BF16),
        grid=(B, N_HEADS, S // tq),
        in_specs=[
            pl.BlockSpec((None, None, QK_PAD, tq), lambda b, h, i: (b, h, 0, i)),
            pl.BlockSpec((None, None, S, QK_PAD), lambda b, h, i: (b, kslot(h), 0, 0)),
            pl.BlockSpec((None, None, n_chunks, V_ROWS, tk), lambda b, h, i: (b, vslot(h), 0, 0, 0)),
        ],
        out_specs=pl.BlockSpec((None, None, D_V, tq), lambda b, h, i: (b, h, 0, i)),
        scratch_shapes=[pltpu.VMEM((tk, tq + 128), F32), pltpu.VMEM((tk, tq + 128), F32)],
        compiler_params=_cparams(("parallel", "parallel", "arbitrary")),
        name="attn_sweep",
    )(qt, k, vt)


def _post_attn_kernel(o_ref, x_ref, mod_ref, wo_ref, out_ref):
    tm = x_ref.shape[0]
    ot = o_ref[...].reshape(N_HEADS * D_V, tm)
    m = _dot_tn(ot, wo_ref[...])
    out_ref[...] = x_ref[...] + mod_ref[...][2:3] * m


def _post_attn(ot, x, mod_l, wo):
    B, S, _ = x.shape
    tm = TOKEN_TILE
    return pl.pallas_call(
        _post_attn_kernel,
        out_shape=jax.ShapeDtypeStruct(x.shape, F32),
        grid=(B, S // tm),
        in_specs=[
            pl.BlockSpec((None, N_HEADS, D_V, tm), lambda b, i: (b, 0, 0, i)),
            pl.BlockSpec((None, tm, D_MODEL), lambda b, i: (b, i, 0)),
            pl.BlockSpec((None, 6, D_MODEL), lambda b, i: (b, 0, 0)),
            pl.BlockSpec((D_MODEL, D_MODEL), lambda b, i: (0, 0)),
        ],
        out_specs=pl.BlockSpec((None, tm, D_MODEL), lambda b, i: (b, i, 0)),
        compiler_params=_cparams(("parallel", "parallel")),
        name="post_attn",
    )(ot, x, mod_l, wo)


def _f1_kernel(x_ref, mod_ref, gmix_ref, fc_ref, m1_ref, twr_ref, twi_ref, y_ref, z_scr):
    n1 = x_ref.shape[0]
    x = x_ref[...].reshape(n1 * FFT_T, D_MODEL)
    mod = mod_ref[...]
    h = _rms_rows(x, gmix_ref[...]) * (1.0 + mod[1:2]) + mod[0:1]
    hb = h.astype(BF16)
    fc = fc_ref[...]
    for g in range(N_FGROUPS):
        zg = _dot(hb[:, FGROUP * g:FGROUP * (g + 1)], fc)
        z_scr[:, :, FGROUP * g:FGROUP * (g + 1)] = zg[:, :FGROUP].reshape(n1, FFT_T, FGROUP)
        z_scr[:, :, D_MODEL + FGROUP * g:D_MODEL + FGROUP * (g + 1)] = zg[:, FGROUP:].reshape(n1, FFT_T, FGROUP)
    m1 = m1_ref[...]
    for j in range(FFT_T):
        zj = z_scr[:, j, :]
        st = jnp.concatenate([zj[:, :D_MODEL], zj[:, D_MODEL:]], axis=0).astype(BF16)
        y = _dot(m1, st)
        yr, yi = y[:n1], y[n1:]
        tr = jnp.tile(twr_ref[j], (1, D_MODEL // 128))
        ti = jnp.tile(twi_ref[j], (1, D_MODEL // 128))
        y_ref[:, j, 0:D_MODEL] = yr * tr - yi * ti
        y_ref[:, j, D_MODEL:2 * D_MODEL] = yr * ti + yi * tr


def _f2_kernel(y_ref, x_ref, mod_ref, cs_ref, wc_ref, o_ref, f_scr):
    cs = cs_ref[...]
    for j in range(FFT_T):
        yk = y_ref[j]
        st = jnp.concatenate([yk[:, :D_MODEL], yk[:, D_MODEL:]], axis=0).astype(BF16)
        f_scr[FFT_N2 * j:FFT_N2 * (j + 1), :] = _dot(cs, st).astype(BF16)
    m = _dot(f_scr[...], wc_ref[...])
    gt1 = mod_ref[...][2:3]
    for j in range(FFT_T):
        o_ref[:, j, :] = x_ref[:, j, :] + gt1 * m[FFT_N2 * j:FFT_N2 * (j + 1)]


def _dft_constants(S):
    n1 = S // FFT_N2
    k = np.arange(FGROUP)
    ang = 2.0 * np.pi * np.outer(k, k) / FGROUP
    fc = np.concatenate([np.cos(ang), -np.sin(ang)], axis=1) / math.sqrt(FGROUP)
    a = np.arange(n1)
    ang1 = 2.0 * np.pi * np.outer(a, a) / n1
    c1, s1 = np.cos(ang1), np.sin(ang1)
    m1 = np.block([[c1, s1], [-s1, c1]]) / math.sqrt(n1)
    b = np.arange(FFT_N2)
    ang2 = 2.0 * np.pi * np.outer(b, b) / FFT_N2
    cs = np.concatenate([np.cos(ang2), np.sin(ang2)], axis=1) / math.sqrt(FFT_N2)
    angt = 2.0 * np.pi * np.outer(b, a) / S
    twr = np.broadcast_to(np.cos(angt)[:, :, None], (FFT_N2, n1, 128))
    twi = np.broadcast_to(-np.sin(angt)[:, :, None], (FFT_N2, n1, 128))
    return (jnp.asarray(fc, BF16), jnp.asarray(m1, BF16), jnp.asarray(cs, BF16),
            jnp.asarray(twr, F32), jnp.asarray(twi, F32))


def _fourier_layer(x, mod_l, g_mix, wc):
    B, S, _ = x.shape
    n1 = S // FFT_N2
    fc, m1, cs, twr, twi = _dft_constants(S)
    const2 = lambda b, i: (0, 0)
    y = pl.pallas_call(
        _f1_kernel,
        out_shape=jax.ShapeDtypeStruct((B, n1, FFT_N2, 2 * D_MODEL), F32),
        grid=(B, FFT_N2 // FFT_T),
        in_specs=[
            pl.BlockSpec((None, n1, FFT_T, D_MODEL), lambda b, i: (b, 0, i, 0)),
            pl.BlockSpec((None, 6, D_MODEL), lambda b, i: (b, 0, 0)),
            pl.BlockSpec((1, D_MODEL), const2),
            pl.BlockSpec((FGROUP, 2 * FGROUP), const2),
            pl.BlockSpec((2 * n1, 2 * n1), const2),
            pl.BlockSpec((FFT_T, n1, 128), lambda b, i: (i, 0, 0)),
            pl.BlockSpec((FFT_T, n1, 128), lambda b, i: (i, 0, 0)),
        ],
        out_specs=pl.BlockSpec((None, n1, FFT_T, 2 * D_MODEL), lambda b, i: (b, 0, i, 0)),
        scratch_shapes=[pltpu.VMEM((n1, FFT_T, 2 * D_MODEL), F32)],
        compiler_params=_cparams(("parallel", "parallel")),
        name="fourier_stage1",
    )(x.reshape(B, n1, FFT_N2, D_MODEL), mod_l, g_mix, fc, m1, twr, twi)
    out = pl.pallas_call(
        _f2_kernel,
        out_shape=jax.ShapeDtypeStruct((B, FFT_N2, n1, D_MODEL), F32),
        grid=(B, n1 // FFT_T),
        in_specs=[
            pl.BlockSpec((None, FFT_T, FFT_N2, 2 * D_MODEL), lambda b, i: (b, i, 0, 0)),
            pl.BlockSpec((None, FFT_N2, FFT_T, D_MODEL), lambda b, i: (b, 0, i, 0)),
            pl.BlockSpec((None, 6, D_MODEL), lambda b, i: (b, 0, 0)),
            pl.BlockSpec((FFT_N2, 2 * FFT_N2), const2),
            pl.BlockSpec((D_MODEL, D_MODEL), const2),
        ],
        out_specs=pl.BlockSpec((None, FFT_N2, FFT_T, D_MODEL), lambda b, i: (b, 0, i, 0)),
        scratch_shapes=[pltpu.VMEM((FFT_T * FFT_N2, D_MODEL), BF16)],
        compiler_params=_cparams(("parallel", "parallel")),
        name="fourier_stage2",
    )(y, x.reshape(B, FFT_N2, n1, D_MODEL), mod_l, cs, wc)
    return out.reshape(B, S, D_MODEL)


def _route_t(s, sb):
    rows = [sb[e:e + 1] for e in range(N_EXPERTS)]
    gscore = []
    for g in range(N_EGROUPS):
        a, b, c, d = rows[4 * g:4 * g + 4]
        gscore.append(jnp.maximum(jnp.maximum(jnp.maximum(a + b, a + c), jnp.maximum(a + d, b + c)),
                                  jnp.maximum(b + d, c + d)))
    best = gscore[0]
    gsel = jnp.zeros_like(best)
    for g in range(1, N_EGROUPS):
        upd = gscore[g] > best
        gsel = jnp.where(upd, float(g), gsel)
        best = jnp.where(upd, gscore[g], best)
    picked = []
    for e in range(N_EXPERTS):
        g = e // EXPERTS_PER_GROUP
        rank = jnp.zeros_like(best)
        for j in range(4 * g, 4 * g + 4):
            if j == e:
                continue
            ahead = (rows[j] >= rows[e]) if j < e else (rows[j] > rows[e])
            rank = rank + jnp.where(ahead, 1.0, 0.0)
        sel = jnp.where(rank < 1.5, 1.0, 0.0) * jnp.where(gsel == float(g), 1.0, 0.0)
        picked.append(sel * s[e:e + 1])
    total = picked[0]
    for e in range(1, N_EXPERTS):
        total = total + picked[e]
    inv = 1.0 / total
    return [p * inv for p in picked]


def _moe_kernel(x_ref, mod_ref, gffn_ref, wr_ref, bias_ref, wg_ref, wu_ref, wd_ref, gfin_ref,
                out_ref, hb_scr, gate_scr, acc_scr, *, final):
    g = pl.program_id(2)
    tm = x_ref.shape[0]

    @pl.when(g == 0)
    def _():
        mod = mod_ref[...]
        t = _rms_rows(x_ref[...], gffn_ref[...]) * (1.0 + mod[4:5]) + mod[3:4]
        t_hi = t.astype(BF16)
        t_lo = (t - t_hi.astype(F32)).astype(BF16)
        wr = wr_ref[...]
        a = _dot_nt(wr, t_hi)
        b = _dot_nt(wr[0:N_EXPERTS], t_lo)
        logits = a[0:N_EXPERTS] + a[N_EXPERTS:2 * N_EXPERTS] + b
        s = _sigmoid(logits)
        gates = _route_t(s, s + bias_ref[...])
        zpad = jnp.zeros((128 - EXPERTS_PER_GROUP, tm), F32)
        for gg in range(N_EGROUPS):
            gt = jnp.concatenate(gates[4 * gg:4 * gg + 4] + [zpad], axis=0)
            gate_scr[gg] = gt.T
        hb_scr[...] = t_hi
        acc_scr[...] = jnp.zeros_like(acc_scr)

    hb = hb_scr[...]
    u1 = _dot(hb, wg_ref[...])
    u2 = _dot(hb, wu_ref[...])
    he = u1 * _sigmoid(u1) * u2
    gcols = gate_scr[g]
    he = jnp.concatenate(
        [he[:, D_EXPERT * j:D_EXPERT * (j + 1)] * gcols[:, j:j + 1] for j in range(EXPERTS_PER_GROUP)],
        axis=1)
    acc_scr[...] += _dot(he.astype(BF16), wd_ref[...])

    @pl.when(g == N_EGROUPS - 1)
    def _():
        y = x_ref[...] + mod_ref[...][5:6] * acc_scr[...]
        if final:
            y = _rms_rows(y, gfin_ref[...])
        out_ref[...] = y


def _moe(x, mod_l, g_ffn, wr, bias, wg, wu, wd, g_final, final):
    B, S, _ = x.shape
    tm = MOE_TILE
    const = lambda b, i, g: (0, 0)
    wspec = pl.BlockSpec((None, D_MODEL, D_MODEL), lambda b, i, g: (g, 0, 0))
    return pl.pallas_call(
        functools.partial(_moe_kernel, final=final),
        out_shape=jax.ShapeDtypeStruct(x.shape, F32),
        grid=(B, S // tm, N_EGROUPS),
        in_specs=[
            pl.BlockSpec((None, tm, D_MODEL), lambda b, i, g: (b, i, 0)),
            pl.BlockSpec((None, 6, D_MODEL), lambda b, i, g: (b, 0, 0)),
            pl.BlockSpec((1, D_MODEL), const),
            pl.BlockSpec((2 * N_EXPERTS, D_MODEL), const),
            pl.BlockSpec((N_EXPERTS, 1), const),
            wspec, wspec, wspec,
            pl.BlockSpec((1, D_MODEL), const),
        ],
        out_specs=pl.BlockSpec((None, tm, D_MODEL), lambda b, i, g: (b, i, 0)),
        scratch_shapes=[
            pltpu.VMEM((tm, D_MODEL), BF16),
            pltpu.VMEM((N_EGROUPS, tm, 128), F32),
            pltpu.VMEM((tm, D_MODEL), F32),
        ],
        compiler_params=_cparams(("parallel", "parallel", "arbitrary")),
        name="moe",
    )(x, mod_l, g_ffn, wr, bias, wg, wu, wd, g_final)


def _rope_tables(S):
    pos = jnp.arange(S, dtype=jnp.int32)
    rows = (pos // GRID_W).astype(F32)[None, :]
    cols = (pos % GRID_W).astype(F32)[None, :]
    tabs = []
    for half in (HEAD_DIM // 2, D_ROPE // 2):
        inv = (ROPE_THETA ** (-jnp.arange(0, half, 2, dtype=F32) / half))[:, None]
        for p in (rows, cols):
            ang = inv * p
            tabs += [jnp.cos(ang), jnp.sin(ang)]
    return jnp.concatenate(tabs, axis=0)


def _group_experts(w, transpose_cols):
    e, a, b = w.shape
    w = w.reshape(N_EGROUPS, EXPERTS_PER_GROUP, a, b)
    if transpose_cols:
        return w.transpose(0, 2, 1, 3).reshape(N_EGROUPS, a, EXPERTS_PER_GROUP * b).astype(BF16)
    return w.reshape(N_EGROUPS, EXPERTS_PER_GROUP * a, b).astype(BF16)


def _trunk(x, mod, p):
    B, S, _ = x.shape
    tab = _rope_tables(S)
    for l in range(DEPTH):
        mod_l = mod[l]
        g_mix = p["g_mix"][l][None, :]
        if l % 2 == 0:
            i = l // 2
            qt, k, vt = _pre_attn(x, mod_l, g_mix, tab, p["wint"][i], p["g_qa"][i][:, None],
                                  p["g_ka"][i][:, None], p["g_cq"][i][:, None], p["wuqt"][i],
                                  p["g_ckv"][i][:, None], p["wukvt"][i])
            ot = _attention(qt, k, vt)
            x = _post_attn(ot, x, mod_l, p["wo"][i])
        else:
            x = _fourier_layer(x, mod_l, g_mix, p["wc"][l // 2])
        x = _moe(x, mod_l, p["g_ffn"][l][None, :], p["wr"], p["bias"], p["wg"][l], p["wu"][l],
                 p["wd"][l], p["g_final"], final=(l == DEPTH - 1))
    return x


def kernel(x_prompt, x_sample, c_prompt, c_sample, w_ada, b_ada, g_mix, g_ffn, w_in, g_qa, g_ka, g_cq,
           w_uq, g_ckv, w_ukv, w_o_attn, w_fourier, w_router, router_bias, w_gate, w_up, w_down, g_final):
    nbp = x_prompt.shape[0]
    mod = _modulation(jnp.concatenate([c_prompt, c_sample], axis=0), w_ada, b_ada)
    mod = mod.reshape(DEPTH, -1, 6, D_MODEL)
    wrt = w_router.T
    wr_hi = wrt.astype(BF16)
    wr_lo = (wrt - wr_hi.astype(F32)).astype(BF16)
    p = {
        "g_mix": g_mix, "g_ffn": g_ffn, "g_qa": g_qa, "g_ka": g_ka, "g_cq": g_cq, "g_ckv": g_ckv,
        "wint": jnp.swapaxes(w_in, 1, 2).astype(BF16),
        "wuqt": jnp.swapaxes(w_uq, 1, 2).astype(BF16),
        "wukvt": jnp.swapaxes(w_ukv, 1, 2).astype(BF16),
        "wo": w_o_attn.astype(BF16),
        "wc": w_fourier.astype(BF16),
        "wr": jnp.concatenate([wr_hi, wr_lo], axis=0),
        "bias": router_bias[:, None],
        "wg": [_group_experts(w_gate[l], True) for l in range(DEPTH)],
        "wu": [_group_experts(w_up[l], True) for l in range(DEPTH)],
        "wd": [_group_experts(w_down[l], False) for l in range(DEPTH)],
        "g_final": g_final[None, :],
    }
    y_prompt = _trunk(x_prompt, mod[:, :nbp], p)
    y_sample = _trunk(x_sample, mod[:, nbp:], p)
    return (y_prompt, y_sample)
```

```python
import functools
import math

import numpy as np
import jax
import jax.numpy as jnp
from jax import lax
from jax.experimental import pallas as pl
from jax.experimental.pallas import tpu as pltpu

F32 = jnp.float32
BF16 = jnp.bfloat16
QK_DTYPE = jnp.float8_e4m3fn

D_MODEL = 1024
DEPTH = 4
GRID_W = 64
HEAD_DIM = 64
N_HEADS_A = 8
N_KV_A = 2
N_HEADS_B = 8
D_NOPE = 64
D_ROPE = 32
D_V = 64
D_CQ = 384
D_CKV = 256
N_FGROUPS = 4
FGROUP = D_MODEL // N_FGROUPS
ROPE_THETA = 10000.0
N_EXPERTS = 16
N_EGROUPS = 4
EXPERTS_PER_GROUP = 4
D_EXPERT = 256
EPS = 1e-6
IN_SIZES = (N_HEADS_A * HEAD_DIM, N_KV_A * HEAD_DIM, N_KV_A * HEAD_DIM, D_CQ, D_CKV, D_ROPE)
IN_WIDTH = sum(IN_SIZES)
IN_OFFS = tuple(int(v) for v in np.cumsum((0,) + IN_SIZES))

N_HEADS = N_HEADS_A + N_HEADS_B
N_KSLOTS = 1 + N_HEADS_B
N_VSLOTS = N_KV_A + N_HEADS_B
QK_PAD = 128
LOG2E = 1.4426950408889634
SCALE_A = HEAD_DIM ** -0.5 * LOG2E
SCALE_B = (D_NOPE + D_ROPE) ** -0.5 * LOG2E
NEG_BIG = -1e30

FFT_N2 = 128
FFT_T = 8

TOKEN_TILE = 512
Q_TILE = 512
V_ROWS = D_V + 16
MOE_TILE = 512
VMEM_LIMIT = 56 * 1024 * 1024


def _cparams(sem, flags=None):
    return pltpu.CompilerParams(dimension_semantics=sem, vmem_limit_bytes=VMEM_LIMIT, flags=flags)


def _sigmoid(x):
    return 1.0 / (1.0 + jnp.exp(-x))


def _rms_rows(x, g):
    ms = jnp.mean(x * x, axis=-1, keepdims=True)
    return x * lax.rsqrt(ms + EPS) * g


def _rms_cols(xt, gcol):
    ms = jnp.mean(xt * xt, axis=0, keepdims=True)
    return xt * lax.rsqrt(ms + EPS) * gcol


def _dot(a, b):
    return jnp.dot(a, b, preferred_element_type=F32)


def _dot_nt(a, b):
    return lax.dot_general(a, b, (((1,), (1,)), ((), ())), preferred_element_type=F32)


def _dot_tn(a, b):
    return lax.dot_general(a, b, (((0,), (0,)), ((), ())), preferred_element_type=F32)


def _mod_kernel(c_ref, w_ref, b_ref, o_ref):
    c = c_ref[...]
    ca = c * _sigmoid(c)
    o_ref[...] = _dot(ca, w_ref[...]) + b_ref[...]


def _modulation(c_all, w_ada, b_ada):
    nb = c_all.shape[0]
    tn = 1536
    return pl.pallas_call(
        _mod_kernel,
        out_shape=jax.ShapeDtypeStruct((DEPTH, nb, 6 * D_MODEL), F32),
        grid=(DEPTH, 6 * D_MODEL // tn),
        in_specs=[
            pl.BlockSpec((nb, D_MODEL), lambda l, j: (0, 0)),
            pl.BlockSpec((None, D_MODEL, tn), lambda l, j: (l, 0, j)),
            pl.BlockSpec((None, 1, tn), lambda l, j: (l, 0, j)),
        ],
        out_specs=pl.BlockSpec((None, nb, tn), lambda l, j: (l, 0, j)),
        compiler_params=_cparams(("arbitrary", "arbitrary")),
        name="adaln_mod",
    )(c_all, w_ada, b_ada.reshape(DEPTH, 1, 6 * D_MODEL))


def _rope_t(xt, c, s):
    n = c.shape[0]
    x1, x2 = xt[:n], xt[n:]
    return jnp.concatenate([x1 * c - x2 * s, x1 * s + x2 * c], axis=0)


def _axial_t(xt, cr, sr, cc, sc):
    d2 = xt.shape[0] // 2
    return jnp.concatenate([_rope_t(xt[:d2], cr, sr), _rope_t(xt[d2:], cc, sc)], axis=0)


def _pre_attn_kernel(x_ref, mod_ref, gmix_ref, tab_ref, wint_ref, gqa_ref, gka_ref, gcq_ref,
                     wuqt_ref, gckv_ref, wukvt_ref, q_ref, k_ref, v_ref):
    x = x_ref[...]
    tm = x.shape[0]
    mod = mod_ref[...]
    h = _rms_rows(x, gmix_ref[...]) * (1.0 + mod[1:2]) + mod[0:1]
    zt = _dot_nt(wint_ref[...], h.astype(BF16))

    tab = tab_ref[...]
    ra = (tab[0:16], tab[16:32], tab[32:48], tab[48:64])
    rb = (tab[64:72], tab[72:80], tab[80:88], tab[88:96])
    zeros64 = jnp.zeros((HEAD_DIM, tm), F32)
    zeros32 = jnp.zeros((QK_PAD - D_NOPE - D_ROPE, tm), F32)

    o_q, o_k, o_v, o_cq, o_ckv, o_kr = IN_OFFS[:6]
    gqa = gqa_ref[...]
    for hh in range(N_HEADS_A):
        qh = _axial_t(_rms_cols(zt[o_q + HEAD_DIM * hh:o_q + HEAD_DIM * (hh + 1)], gqa), *ra) * SCALE_A
        parts = [qh, zeros64] if hh // (N_HEADS_A // N_KV_A) == 0 else [zeros64, qh]
        q_ref[hh] = jnp.concatenate(parts, axis=0).astype(QK_DTYPE)
    gka = gka_ref[...]
    kts = [jnp.concatenate(
        [_axial_t(_rms_cols(zt[o_k + HEAD_DIM * g:o_k + HEAD_DIM * (g + 1)], gka), *ra)
         for g in range(N_KV_A)], axis=0)]
    ones16 = jnp.ones((V_ROWS - D_V, tm), F32)
    for g in range(N_KV_A):
        v_ref[g] = jnp.concatenate(
            [zt[o_v + HEAD_DIM * g:o_v + HEAD_DIM * (g + 1)], ones16], axis=0).astype(BF16)

    cqn = _rms_cols(zt[o_cq:o_ckv], gcq_ref[...]).astype(BF16)
    qbt = _dot(wuqt_ref[...], cqn)
    dq = D_NOPE + D_ROPE
    for hh in range(N_HEADS_B):
        nope = qbt[dq * hh:dq * hh + D_NOPE]
        rp = _axial_t(qbt[dq * hh + D_NOPE:dq * (hh + 1)], *rb)
        q_ref[N_HEADS_A + hh] = (jnp.concatenate([nope, rp, zeros32], axis=0) * SCALE_B).astype(QK_DTYPE)
    ckvn = _rms_cols(zt[o_ckv:o_kr], gckv_ref[...]).astype(BF16)
    kvt = _dot(wukvt_ref[...], ckvn)
    kr = _axial_t(zt[o_kr:o_kr + D_ROPE], *rb)
    dkv = D_NOPE + D_V
    for hh in range(N_HEADS_B):
        kts.append(jnp.concatenate([kvt[dkv * hh:dkv * hh + D_NOPE], kr, zeros32], axis=0))
        v_ref[N_KV_A + hh] = jnp.concatenate(
            [kvt[dkv * hh + D_NOPE:dkv * (hh + 1)], ones16], axis=0).astype(BF16)

    k_all = jnp.concatenate(kts, axis=0).T
    for i in range(N_KSLOTS):
        k_ref[i] = k_all[:, QK_PAD * i:QK_PAD * (i + 1)].astype(QK_DTYPE)


def _pre_attn(x, mod_l, g_mix, tab, wint, gqa, gka, gcq, wuqt, gckv, wukvt):
    B, S, _ = x.shape
    tm = TOKEN_TILE
    nt = S // tm
    const = lambda b, i: (0, 0)
    return pl.pallas_call(
        _pre_attn_kernel,
        out_shape=(
            jax.ShapeDtypeStruct((B, N_HEADS, QK_PAD, S), QK_DTYPE),
            jax.ShapeDtypeStruct((B, N_KSLOTS, S, QK_PAD), QK_DTYPE),
            jax.ShapeDtypeStruct((B, N_VSLOTS, nt, V_ROWS, tm), BF16),
        ),
        grid=(B, nt),
        in_specs=[
            pl.BlockSpec((None, tm, D_MODEL), lambda b, i: (b, i, 0)),
            pl.BlockSpec((None, 6, D_MODEL), lambda b, i: (b, 0, 0)),
            pl.BlockSpec((1, D_MODEL), const),
            pl.BlockSpec((96, tm), lambda b, i: (0, i)),
            pl.BlockSpec((IN_WIDTH, D_MODEL), const),
            pl.BlockSpec((HEAD_DIM, 1), const),
            pl.BlockSpec((HEAD_DIM, 1), const),
            pl.BlockSpec((D_CQ, 1), const),
            pl.BlockSpec((N_HEADS_B * (D_NOPE + D_ROPE), D_CQ), const),
            pl.BlockSpec((D_CKV, 1), const),
            pl.BlockSpec((N_HEADS_B * (D_NOPE + D_V), D_CKV), const),
        ],
        out_specs=(
            pl.BlockSpec((None, N_HEADS, QK_PAD, tm), lambda b, i: (b, 0, 0, i)),
            pl.BlockSpec((None, N_KSLOTS, tm, QK_PAD), lambda b, i: (b, 0, i, 0)),
            pl.BlockSpec((None, N_VSLOTS, None, V_ROWS, tm), lambda b, i: (b, 0, i, 0, 0)),
        ),
        compiler_params=_cparams(("parallel", "parallel")),
        name="pre_attn",
    )(x, mod_l, g_mix, tab, wint, gqa, gka, gcq, wuqt, gckv, wukvt)


def _attn_kernel(q_ref, k_ref, v_ref, o_ref, sa_scr, sb_scr, *, n_chunks, tk, trip):
    qt = q_ref[...]
    tq = qt.shape[1]
    bufs = (sa_scr, sb_scr)

    def scores(c, s_ref):
        off = c * tk if isinstance(c, int) else pl.multiple_of(c * tk, tk)
        s = _dot(k_ref[pl.ds(off, tk), :], qt)
        s_ref[:, 0:tq] = s
        return jnp.max(s, axis=0, keepdims=True)

    def update(c, s_ref, cm, m, acc):
        m_new = jnp.maximum(m, cm)
        p = jnp.exp2(s_ref[:, 0:tq] - m_new).astype(BF16)
        return m_new, acc * jnp.exp2(m - m_new) + _dot(v_ref[c], p)

    def steps(c0, count, cm, m, acc, last):
        for u in range(count):
            if not (last and u == count - 1):
                cm_next = scores(c0 + u + 1, bufs[(u + 1) % 2])
            m, acc = update(c0 + u, bufs[u % 2], cm, m, acc)
            cm = cm_next
        return cm, m, acc

    def body(i, carry):
        return steps(i * trip, trip, *carry, last=False)

    m0 = jnp.full((1, tq), NEG_BIG, F32)
    acc0 = jnp.zeros((V_ROWS, tq), F32)
    n_trips = n_chunks // trip - 1
    carry = lax.fori_loop(0, n_trips, body, (scores(0, sa_scr), m0, acc0))
    _, _, acc = steps(n_trips * trip, trip, *carry, last=True)
    o_ref[...] = (acc[:D_V] * (1.0 / acc[D_V:D_V + 1])).astype(o_ref.dtype)


def _attention(qt, k, vt):
    B, _, _, S = qt.shape
    n_chunks, tk = vt.shape[2], vt.shape[4]
    tq = Q_TILE
    kslot = lambda h: jnp.where(h < N_HEADS_A, 0, h - (N_HEADS_A - 1))
    vslot = lambda h: jnp.where(h < N_HEADS_A, h // (N_HEADS_A // N_KV_A), h - (N_HEADS_A - N_KV_A))
    return pl.pallas_call(
        functools.partial(_attn_kernel, n_chunks=n_chunks, tk=tk,
                          trip=4 if n_chunks % 4 == 0 and n_chunks >= 8 else 2),
        out_shape=jax.ShapeDtypeStruct((B, N_HEADS, D_V, S), BF16),
        grid=(B, N_HEADS, S // tq),
        in_specs=[
            pl.BlockSpec((None, None, QK_PAD, tq), lambda b, h, i: (b, h, 0, i)),
            pl.BlockSpec((None, None, S, QK_PAD), lambda b, h, i: (b, kslot(h), 0, 0)),
            pl.BlockSpec((None, None, n_chunks, V_ROWS, tk), lambda b, h, i: (b, vslot(h), 0, 0, 0)),
        ],
        out_specs=pl.BlockSpec((None, None, D_V, tq), lambda b, h, i: (b, h, 0, i)),
        scratch_shapes=[pltpu.VMEM((tk, tq + 128), F32), pltpu.VMEM((tk, tq + 128), F32)],
        compiler_params=_cparams(("parallel", "parallel", "arbitrary")),
        name="attn_sweep",
    )(qt, k, vt)


def _post_attn_kernel(o_ref, x_ref, mod_ref, wo_ref, out_ref):
    tm = x_ref.shape[0]
    ot = o_ref[...].reshape(N_HEADS * D_V, tm)
    m = _dot_tn(ot, wo_ref[...])
    out_ref[...] = x_ref[...] + mod_ref[...][2:3] * m


def _post_attn(ot, x, mod_l, wo):
    B, S, _ = x.shape
    tm = TOKEN_TILE
    return pl.pallas_call(
        _post_attn_kernel,
        out_shape=jax.ShapeDtypeStruct(x.shape, F32),
        grid=(B, S // tm),
        in_specs=[
            pl.BlockSpec((None, N_HEADS, D_V, tm), lambda b, i: (b, 0, 0, i)),
            pl.BlockSpec((None, tm, D_MODEL), lambda b, i: (b, i, 0)),
            pl.BlockSpec((None, 6, D_MODEL), lambda b, i: (b, 0, 0)),
            pl.BlockSpec((D_MODEL, D_MODEL), lambda b, i: (0, 0)),
        ],
        out_specs=pl.BlockSpec((None, tm, D_MODEL), lambda b, i: (b, i, 0)),
        compiler_params=_cparams(("parallel", "parallel")),
        name="post_attn",
    )(ot, x, mod_l, wo)


def _f1_kernel(x_ref, mod_ref, gmix_ref, fc_ref, m1_ref, twr_ref, twi_ref, y_ref, z_scr):
    n1 = x_ref.shape[0]
    x = x_ref[...].reshape(n1 * FFT_T, D_MODEL)
    mod = mod_ref[...]
    h = _rms_rows(x, gmix_ref[...]) * (1.0 + mod[1:2]) + mod[0:1]
    hb = h.astype(BF16)
    fc = fc_ref[...]
    for g in range(N_FGROUPS):
        zg = _dot(hb[:, FGROUP * g:FGROUP * (g + 1)], fc)
        z_scr[:, :, FGROUP * g:FGROUP * (g + 1)] = zg[:, :FGROUP].reshape(n1, FFT_T, FGROUP)
        z_scr[:, :, D_MODEL + FGROUP * g:D_MODEL + FGROUP * (g + 1)] = zg[:, FGROUP:].reshape(n1, FFT_T, FGROUP)
    m1 = m1_ref[...]
    for j in range(FFT_T):
        zj = z_scr[:, j, :]
        st = jnp.concatenate([zj[:, :D_MODEL], zj[:, D_MODEL:]], axis=0).astype(BF16)
        y = _dot(m1, st)
        yr, yi = y[:n1], y[n1:]
        tr = jnp.tile(twr_ref[j], (1, D_MODEL // 128))
        ti = jnp.tile(twi_ref[j], (1, D_MODEL // 128))
        y_ref[:, j, 0:D_MODEL] = yr * tr - yi * ti
        y_ref[:, j, D_MODEL:2 * D_MODEL] = yr * ti + yi * tr


def _f2_kernel(y_ref, x_ref, mod_ref, cs_ref, wc_ref, o_ref, f_scr):
    cs = cs_ref[...]
    for j in range(FFT_T):
        yk = y_ref[j]
        st = jnp.concatenate([yk[:, :D_MODEL], yk[:, D_MODEL:]], axis=0).astype(BF16)
        f_scr[FFT_N2 * j:FFT_N2 * (j + 1), :] = _dot(cs, st).astype(BF16)
    m = _dot(f_scr[...], wc_ref[...])
    gt1 = mod_ref[...][2:3]
    for j in range(FFT_T):
        o_ref[:, j, :] = x_ref[:, j, :] + gt1 * m[FFT_N2 * j:FFT_N2 * (j + 1)]


def _dft_constants(S):
    n1 = S // FFT_N2
    k = np.arange(FGROUP)
    ang = 2.0 * np.pi * np.outer(k, k) / FGROUP
    fc = np.concatenate([np.cos(ang), -np.sin(ang)], axis=1) / math.sqrt(FGROUP)
    a = np.arange(n1)
    ang1 = 2.0 * np.pi * np.outer(a, a) / n1
    c1, s1 = np.cos(ang1), np.sin(ang1)
    m1 = np.block([[c1, s1], [-s1, c1]]) / math.sqrt(n1)
    b = np.arange(FFT_N2)
    ang2 = 2.0 * np.pi * np.outer(b, b) / FFT_N2
    cs = np.concatenate([np.cos(ang2), np.sin(ang2)], axis=1) / math.sqrt(FFT_N2)
    angt = 2.0 * np.pi * np.outer(b, a) / S
    twr = np.broadcast_to(np.cos(angt)[:, :, None], (FFT_N2, n1, 128))
    twi = np.broadcast_to(-np.sin(angt)[:, :, None], (FFT_N2, n1, 128))
    return (jnp.asarray(fc, BF16), jnp.asarray(m1, BF16), jnp.asarray(cs, BF16),
            jnp.asarray(twr, F32), jnp.asarray(twi, F32))


def _fourier_layer(x, mod_l, g_mix, wc):
    B, S, _ = x.shape
    n1 = S // FFT_N2
    fc, m1, cs, twr, twi = _dft_constants(S)
    const2 = lambda b, i: (0, 0)
    y = pl.pallas_call(
        _f1_kernel,
        out_shape=jax.ShapeDtypeStruct((B, n1, FFT_N2, 2 * D_MODEL), F32),
        grid=(B, FFT_N2 // FFT_T),
        in_specs=[
            pl.BlockSpec((None, n1, FFT_T, D_MODEL), lambda b, i: (b, 0, i, 0)),
            pl.BlockSpec((None, 6, D_MODEL), lambda b, i: (b, 0, 0)),
            pl.BlockSpec((1, D_MODEL), const2),
            pl.BlockSpec((FGROUP, 2 * FGROUP), const2),
            pl.BlockSpec((2 * n1, 2 * n1), const2),
            pl.BlockSpec((FFT_T, n1, 128), lambda b, i: (i, 0, 0)),
            pl.BlockSpec((FFT_T, n1, 128), lambda b, i: (i, 0, 0)),
        ],
        out_specs=pl.BlockSpec((None, n1, FFT_T, 2 * D_MODEL), lambda b, i: (b, 0, i, 0)),
        scratch_shapes=[pltpu.VMEM((n1, FFT_T, 2 * D_MODEL), F32)],
        compiler_params=_cparams(("parallel", "parallel")),
        name="fourier_stage1",
    )(x.reshape(B, n1, FFT_N2, D_MODEL), mod_l, g_mix, fc, m1, twr, twi)
    out = pl.pallas_call(
        _f2_kernel,
        out_shape=jax.ShapeDtypeStruct((B, FFT_N2, n1, D_MODEL), F32),
        grid=(B, n1 // FFT_T),
        in_specs=[
            pl.BlockSpec((None, FFT_T, FFT_N2, 2 * D_MODEL), lambda b, i: (b, i, 0, 0)),
            pl.BlockSpec((None, FFT_N2, FFT_T, D_MODEL), lambda b, i: (b, 0, i, 0)),
            pl.BlockSpec((None, 6, D_MODEL), lambda b, i: (b, 0, 0)),
            pl.BlockSpec((FFT_N2, 2 * FFT_N2), const2),
            pl.BlockSpec((D_MODEL, D_MODEL), const2),
        ],
        out_specs=pl.BlockSpec((None, FFT_N2, FFT_T, D_MODEL), lambda b, i: (b, 0, i, 0)),
        scratch_shapes=[pltpu.VMEM((FFT_T * FFT_N2, D_MODEL), BF16)],
        compiler_params=_cparams(("parallel", "parallel")),
        name="fourier_stage2",
    )(y, x.reshape(B, FFT_N2, n1, D_MODEL), mod_l, cs, wc)
    return out.reshape(B, S, D_MODEL)


def _route_t(s, sb):
    rows = [sb[e:e + 1] for e in range(N_EXPERTS)]
    gscore = []
    for g in range(N_EGROUPS):
        a, b, c, d = rows[4 * g:4 * g + 4]
        gscore.append(jnp.maximum(jnp.maximum(jnp.maximum(a + b, a + c), jnp.maximum(a + d, b + c)),
                                  jnp.maximum(b + d, c + d)))
    best = gscore[0]
    gsel = jnp.zeros_like(best)
    for g in range(1, N_EGROUPS):
        upd = gscore[g] > best
        gsel = jnp.where(upd, float(g), gsel)
        best = jnp.where(upd, gscore[g], best)
    picked = []
    for e in range(N_EXPERTS):
        g = e // EXPERTS_PER_GROUP
        rank = jnp.zeros_like(best)
        for j in range(4 * g, 4 * g + 4):
            if j == e:
                continue
            ahead = (rows[j] >= rows[e]) if j < e else (rows[j] > rows[e])
            rank = rank + jnp.where(ahead, 1.0, 0.0)
        sel = jnp.where(rank < 1.5, 1.0, 0.0) * jnp.where(gsel == float(g), 1.0, 0.0)
        picked.append(sel * s[e:e + 1])
    total = picked[0]
    for e in range(1, N_EXPERTS):
        total = total + picked[e]
    inv = 1.0 / total
    return [p * inv for p in picked]


def _moe_kernel(x_ref, mod_ref, gffn_ref, wr_ref, bias_ref, wg_ref, wu_ref, wd_ref, gfin_ref,
                out_ref, hb_scr, gate_scr, acc_scr, *, final):
    g = pl.program_id(2)
    tm = x_ref.shape[0]

    @pl.when(g == 0)
    def _():
        mod = mod_ref[...]
        t = _rms_rows(x_ref[...], gffn_ref[...]) * (1.0 + mod[4:5]) + mod[3:4]
        t_hi = t.astype(BF16)
        t_lo = (t - t_hi.astype(F32)).astype(BF16)
        wr = wr_ref[...]
        a = _dot_nt(wr, t_hi)
        b = _dot_nt(wr[0:N_EXPERTS], t_lo)
        logits = a[0:N_EXPERTS] + a[N_EXPERTS:2 * N_EXPERTS] + b
        s = _sigmoid(logits)
        gates = _route_t(s, s + bias_ref[...])
        zpad = jnp.zeros((128 - EXPERTS_PER_GROUP, tm), F32)
        for gg in range(N_EGROUPS):
            gt = jnp.concatenate(gates[4 * gg:4 * gg + 4] + [zpad], axis=0)
            gate_scr[gg] = gt.T
        hb_scr[...] = t_hi
        acc_scr[...] = jnp.zeros_like(acc_scr)

    hb = hb_scr[...]
    u1 = _dot(hb, wg_ref[...])
    u2 = _dot(hb, wu_ref[...])
    he = u1 * _sigmoid(u1) * u2
    gcols = gate_scr[g]
    he = jnp.concatenate(
        [he[:, D_EXPERT * j:D_EXPERT * (j + 1)] * gcols[:, j:j + 1] for j in range(EXPERTS_PER_GROUP)],
        axis=1)
    acc_scr[...] += _dot(he.astype(BF16), wd_ref[...])

    @pl.when(g == N_EGROUPS - 1)
    def _():
        y = x_ref[...] + mod_ref[...][5:6] * acc_scr[...]
        if final:
            y = _rms_rows(y, gfin_ref[...])
        out_ref[...] = y


def _moe(x, mod_l, g_ffn, wr, bias, wg, wu, wd, g_final, final):
    B, S, _ = x.shape
    tm = MOE_TILE
    const = lambda b, i, g: (0, 0)
    wspec = pl.BlockSpec((None, D_MODEL, D_MODEL), lambda b, i, g: (g, 0, 0))
    return pl.pallas_call(
        functools.partial(_moe_kernel, final=final),
        out_shape=jax.ShapeDtypeStruct(x.shape, F32),
        grid=(B, S // tm, N_EGROUPS),
        in_specs=[
            pl.BlockSpec((None, tm, D_MODEL), lambda b, i, g: (b, i, 0)),
            pl.BlockSpec((None, 6, D_MODEL), lambda b, i, g: (b, 0, 0)),
            pl.BlockSpec((1, D_MODEL), const),
            pl.BlockSpec((2 * N_EXPERTS, D_MODEL), const),
            pl.BlockSpec((N_EXPERTS, 1), const),
            wspec, wspec, wspec,
            pl.BlockSpec((1, D_MODEL), const),
        ],
        out_specs=pl.BlockSpec((None, tm, D_MODEL), lambda b, i, g: (b, i, 0)),
        scratch_shapes=[
            pltpu.VMEM((tm, D_MODEL), BF16),
            pltpu.VMEM((N_EGROUPS, tm, 128), F32),
            pltpu.VMEM((tm, D_MODEL), F32),
        ],
        compiler_params=_cparams(("parallel", "parallel", "arbitrary")),
        name="moe",
    )(x, mod_l, g_ffn, wr, bias, wg, wu, wd, g_final)


def _rope_tables(S):
    pos = jnp.arange(S, dtype=jnp.int32)
    rows = (pos // GRID_W).astype(F32)[None, :]
    cols = (pos % GRID_W).astype(F32)[None, :]
    tabs = []
    for half in (HEAD_DIM // 2, D_ROPE // 2):
        inv = (ROPE_THETA ** (-jnp.arange(0, half, 2, dtype=F32) / half))[:, None]
        for p in (rows, cols):
            ang = inv * p
            tabs += [jnp.cos(ang), jnp.sin(ang)]
    return jnp.concatenate(tabs, axis=0)


def _group_experts(w, transpose_cols):
    e, a, b = w.shape
    w = w.reshape(N_EGROUPS, EXPERTS_PER_GROUP, a, b)
    if transpose_cols:
        return w.transpose(0, 2, 1, 3).reshape(N_EGROUPS, a, EXPERTS_PER_GROUP * b).astype(BF16)
    return w.reshape(N_EGROUPS, EXPERTS_PER_GROUP * a, b).astype(BF16)


def _trunk(x, mod, p):
    B, S, _ = x.shape
    tab = _rope_tables(S)
    for l in range(DEPTH):
        mod_l = mod[l]
        g_mix = p["g_mix"][l][None, :]
        if l % 2 == 0:
            i = l // 2
            qt, k, vt = _pre_attn(x, mod_l, g_mix, tab, p["wint"][i], p["g_qa"][i][:, None],
                                  p["g_ka"][i][:, None], p["g_cq"][i][:, None], p["wuqt"][i],
                                  p["g_ckv"][i][:, None], p["wukvt"][i])
            ot = _attention(qt, k, vt)
            x = _post_attn(ot, x, mod_l, p["wo"][i])
        else:
            x = _fourier_layer(x, mod_l, g_mix, p["wc"][l // 2])
        x = _moe(x, mod_l, p["g_ffn"][l][None, :], p["wr"], p["bias"], p["wg"][l], p["wu"][l],
                 p["wd"][l], p["g_final"], final=(l == DEPTH - 1))
    return x


def kernel(x_prompt, x_sample, c_prompt, c_sample, w_ada, b_ada, g_mix, g_ffn, w_in, g_qa, g_ka, g_cq,
           w_uq, g_ckv, w_ukv, w_o_attn, w_fourier, w_router, router_bias, w_gate, w_up, w_down, g_final):
    nbp = x_prompt.shape[0]
    mod = _modulation(jnp.concatenate([c_prompt, c_sample], axis=0), w_ada, b_ada)
    mod = mod.reshape(DEPTH, -1, 6, D_MODEL)
    wrt = w_router.T
    wr_hi = wrt.astype(BF16)
    wr_lo = (wrt - wr_hi.astype(F32)).astype(BF16)
    p = {
        "g_mix": g_mix, "g_ffn": g_ffn, "g_qa": g_qa, "g_ka": g_ka, "g_cq": g_cq, "g_ckv": g_ckv,
        "wint": jnp.swapaxes(w_in, 1, 2).astype(BF16),
        "wuqt": jnp.swapaxes(w_uq, 1, 2).astype(BF16),
        "wukvt": jnp.swapaxes(w_ukv, 1, 2).astype(BF16),
        "wo": w_o_attn.astype(BF16),
        "wc": w_fourier.astype(BF16),
        "wr": jnp.concatenate([wr_hi, wr_lo], axis=0),
        "bias": router_bias[:, None],
        "wg": [_group_experts(w_gate[l], True) for l in range(DEPTH)],
        "wu": [_group_experts(w_up[l], True) for l in range(DEPTH)],
        "wd": [_group_experts(w_down[l], False) for l in range(DEPTH)],
        "g_final": g_final[None, :],
    }
    y_prompt = _trunk(x_prompt, mod[:, :nbp], p)
    y_sample = _trunk(x_sample, mod[:, nbp:], p)
    return (y_prompt, y_sample)
```

```python
import functools
import math

import numpy as np
import jax
import jax.numpy as jnp
from jax import lax
from jax.experimental import pallas as pl
from jax.experimental.pallas import tpu as pltpu

F32 = jnp.float32
BF16 = jnp.bfloat16
QK_DTYPE = jnp.float8_e4m3fn

D_MODEL = 1024
DEPTH = 4
GRID_W = 64
HEAD_DIM = 64
N_HEADS_A = 8
N_KV_A = 2
N_HEADS_B = 8
D_NOPE = 64
D_ROPE = 32
D_V = 64
D_CQ = 384
D_CKV = 256
N_FGROUPS = 4
FGROUP = D_MODEL // N_FGROUPS
ROPE_THETA = 10000.0
N_EXPERTS = 16
N_EGROUPS = 4
EXPERTS_PER_GROUP = 4
D_EXPERT = 256
EPS = 1e-6
IN_SIZES = (N_HEADS_A * HEAD_DIM, N_KV_A * HEAD_DIM, N_KV_A * HEAD_DIM, D_CQ, D_CKV, D_ROPE)
IN_WIDTH = sum(IN_SIZES)
IN_OFFS = tuple(int(v) for v in np.cumsum((0,) + IN_SIZES))

N_HEADS = N_HEADS_A + N_HEADS_B
N_KSLOTS = 1 + N_HEADS_B
N_VSLOTS = N_KV_A + N_HEADS_B
QK_PAD = 128
LOG2E = 1.4426950408889634
SCALE_A = HEAD_DIM ** -0.5 * LOG2E
SCALE_B = (D_NOPE + D_ROPE) ** -0.5 * LOG2E
NEG_BIG = -1e30

FFT_N2 = 128
FFT_T = 8

TOKEN_TILE = 512
Q_TILE = 512
V_ROWS = D_V + 16
MOE_TILE = 512
MOE_CHUNK = 128
MOE_ROWS = MOE_TILE + N_EGROUPS * MOE_CHUNK
VMEM_LIMIT = 56 * 1024 * 1024


def _cparams(sem, flags=None):
    return pltpu.CompilerParams(dimension_semantics=sem, vmem_limit_bytes=VMEM_LIMIT, flags=flags)


def _sigmoid(x):
    return 1.0 / (1.0 + jnp.exp(-x))


def _rms_rows(x, g):
    ms = jnp.mean(x * x, axis=-1, keepdims=True)
    return x * lax.rsqrt(ms + EPS) * g


def _rms_cols(xt, gcol):
    ms = jnp.mean(xt * xt, axis=0, keepdims=True)
    return xt * lax.rsqrt(ms + EPS) * gcol


def _dot(a, b):
    return jnp.dot(a, b, preferred_element_type=F32)


def _dot_nt(a, b):
    return lax.dot_general(a, b, (((1,), (1,)), ((), ())), preferred_element_type=F32)


def _dot_tn(a, b):
    return lax.dot_general(a, b, (((0,), (0,)), ((), ())), preferred_element_type=F32)


def _mod_kernel(c_ref, w_ref, b_ref, o_ref):
    c = c_ref[...]
    ca = c * _sigmoid(c)
    o_ref[...] = _dot(ca, w_ref[...]) + b_ref[...]


def _modulation(c_all, w_ada, b_ada):
    nb = c_all.shape[0]
    tn = 1536
    return pl.pallas_call(
        _mod_kernel,
        out_shape=jax.ShapeDtypeStruct((DEPTH, nb, 6 * D_MODEL), F32),
        grid=(DEPTH, 6 * D_MODEL // tn),
        in_specs=[
            pl.BlockSpec((nb, D_MODEL), lambda l, j: (0, 0)),
            pl.BlockSpec((None, D_MODEL, tn), lambda l, j: (l, 0, j)),
            pl.BlockSpec((None, 1, tn), lambda l, j: (l, 0, j)),
        ],
        out_specs=pl.BlockSpec((None, nb, tn), lambda l, j: (l, 0, j)),
        compiler_params=_cparams(("arbitrary", "arbitrary")),
        name="adaln_mod",
    )(c_all, w_ada, b_ada.reshape(DEPTH, 1, 6 * D_MODEL))


def _rope_t(xt, c, s):
    n = c.shape[0]
    x1, x2 = xt[:n], xt[n:]
    return jnp.concatenate([x1 * c - x2 * s, x1 * s + x2 * c], axis=0)


def _axial_t(xt, cr, sr, cc, sc):
    d2 = xt.shape[0] // 2
    return jnp.concatenate([_rope_t(xt[:d2], cr, sr), _rope_t(xt[d2:], cc, sc)], axis=0)


def _pre_attn_kernel(x_ref, mod_ref, gmix_ref, tab_ref, wint_ref, gqa_ref, gka_ref, gcq_ref,
                     wuqt_ref, gckv_ref, wukvt_ref, q_ref, k_ref, v_ref):
    x = x_ref[...]
    tm = x.shape[0]
    mod = mod_ref[...]
    h = _rms_rows(x, gmix_ref[...]) * (1.0 + mod[1:2]) + mod[0:1]
    zt = _dot_nt(wint_ref[...], h.astype(BF16))

    tab = tab_ref[...]
    ra = (tab[0:16], tab[16:32], tab[32:48], tab[48:64])
    rb = (tab[64:72], tab[72:80], tab[80:88], tab[88:96])
    zeros64 = jnp.zeros((HEAD_DIM, tm), F32)
    zeros32 = jnp.zeros((QK_PAD - D_NOPE - D_ROPE, tm), F32)

    o_q, o_k, o_v, o_cq, o_ckv, o_kr = IN_OFFS[:6]
    gqa = gqa_ref[...]
    for hh in range(N_HEADS_A):
        qh = _axial_t(_rms_cols(zt[o_q + HEAD_DIM * hh:o_q + HEAD_DIM * (hh + 1)], gqa), *ra) * SCALE_A
        parts = [qh, zeros64] if hh // (N_HEADS_A // N_KV_A) == 0 else [zeros64, qh]
        q_ref[hh] = jnp.concatenate(parts, axis=0).astype(QK_DTYPE)
    gka = gka_ref[...]
    kts = [jnp.concatenate(
        [_axial_t(_rms_cols(zt[o_k + HEAD_DIM * g:o_k + HEAD_DIM * (g + 1)], gka), *ra)
         for g in range(N_KV_A)], axis=0)]
    ones16 = jnp.ones((V_ROWS - D_V, tm), F32)
    for g in range(N_KV_A):
        v_ref[g] = jnp.concatenate(
            [zt[o_v + HEAD_DIM * g:o_v + HEAD_DIM * (g + 1)], ones16], axis=0).astype(BF16)

    cqn = _rms_cols(zt[o_cq:o_ckv], gcq_ref[...]).astype(BF16)
    qbt = _dot(wuqt_ref[...], cqn)
    dq = D_NOPE + D_ROPE
    for hh in range(N_HEADS_B):
        nope = qbt[dq * hh:dq * hh + D_NOPE]
        rp = _axial_t(qbt[dq * hh + D_NOPE:dq * (hh + 1)], *rb)
        q_ref[N_HEADS_A + hh] = (jnp.concatenate([nope, rp, zeros32], axis=0) * SCALE_B).astype(QK_DTYPE)
    ckvn = _rms_cols(zt[o_ckv:o_kr], gckv_ref[...]).astype(BF16)
    kvt = _dot(wukvt_ref[...], ckvn)
    kr = _axial_t(zt[o_kr:o_kr + D_ROPE], *rb)
    dkv = D_NOPE + D_V
    for hh in range(N_HEADS_B):
        kts.append(jnp.concatenate([kvt[dkv * hh:dkv * hh + D_NOPE], kr, zeros32], axis=0))
        v_ref[N_KV_A + hh] = jnp.concatenate(
            [kvt[dkv * hh + D_NOPE:dkv * (hh + 1)], ones16], axis=0).astype(BF16)

    k_all = jnp.concatenate(kts, axis=0).T
    for i in range(N_KSLOTS):
        k_ref[i] = k_all[:, QK_PAD * i:QK_PAD * (i + 1)].astype(QK_DTYPE)


def _pre_attn(x, mod_l, g_mix, tab, wint, gqa, gka, gcq, wuqt, gckv, wukvt):
    B, S, _ = x.shape
    tm = TOKEN_TILE
    nt = S // tm
    const = lambda b, i: (0, 0)
    return pl.pallas_call(
        _pre_attn_kernel,
        out_shape=(
            jax.ShapeDtypeStruct((B, N_HEADS, QK_PAD, S), QK_DTYPE),
            jax.ShapeDtypeStruct((B, N_KSLOTS, S, QK_PAD), QK_DTYPE),
            jax.ShapeDtypeStruct((B, N_VSLOTS, nt, V_ROWS, tm), BF16),
        ),
        grid=(B, nt),
        in_specs=[
            pl.BlockSpec((None, tm, D_MODEL), lambda b, i: (b, i, 0)),
            pl.BlockSpec((None, 6, D_MODEL), lambda b, i: (b, 0, 0)),
            pl.BlockSpec((1, D_MODEL), const),
            pl.BlockSpec((96, tm), lambda b, i: (0, i)),
            pl.BlockSpec((IN_WIDTH, D_MODEL), const),
            pl.BlockSpec((HEAD_DIM, 1), const),
            pl.BlockSpec((HEAD_DIM, 1), const),
            pl.BlockSpec((D_CQ, 1), const),
            pl.BlockSpec((N_HEADS_B * (D_NOPE + D_ROPE), D_CQ), const),
            pl.BlockSpec((D_CKV, 1), const),
            pl.BlockSpec((N_HEADS_B * (D_NOPE + D_V), D_CKV), const),
        ],
        out_specs=(
            pl.BlockSpec((None, N_HEADS, QK_PAD, tm), lambda b, i: (b, 0, 0, i)),
            pl.BlockSpec((None, N_KSLOTS, tm, QK_PAD), lambda b, i: (b, 0, i, 0)),
            pl.BlockSpec((None, N_VSLOTS, None, V_ROWS, tm), lambda b, i: (b, 0, i, 0, 0)),
        ),
        compiler_params=_cparams(("parallel", "parallel")),
        name="pre_attn",
    )(x, mod_l, g_mix, tab, wint, gqa, gka, gcq, wuqt, gckv, wukvt)


def _attn_kernel(q_ref, k_ref, v_ref, o_ref, sa_scr, sb_scr, *, n_chunks, tk, trip):
    qt = q_ref[...]
    tq = qt.shape[1]
    bufs = (sa_scr, sb_scr)

    def scores(c, s_ref):
        off = c * tk if isinstance(c, int) else pl.multiple_of(c * tk, tk)
        s = _dot(k_ref[pl.ds(off, tk), :], qt)
        s_ref[:, 0:tq] = s
        return jnp.max(s, axis=0, keepdims=True)

    def update(c, s_ref, cm, m, acc):
        m_new = jnp.maximum(m, cm)
        p = jnp.exp2(s_ref[:, 0:tq] - m_new).astype(BF16)
        return m_new, acc * jnp.exp2(m - m_new) + _dot(v_ref[c], p)

    def steps(c0, count, cm, m, acc, last):
        for u in range(count):
            if not (last and u == count - 1):
                cm_next = scores(c0 + u + 1, bufs[(u + 1) % 2])
            m, acc = update(c0 + u, bufs[u % 2], cm, m, acc)
            cm = cm_next
        return cm, m, acc

    def body(i, carry):
        return steps(i * trip, trip, *carry, last=False)

    m0 = jnp.full((1, tq), NEG_BIG, F32)
    acc0 = jnp.zeros((V_ROWS, tq), F32)
    n_trips = n_chunks // trip - 1
    carry = lax.fori_loop(0, n_trips, body, (scores(0, sa_scr), m0, acc0))
    _, _, acc = steps(n_trips * trip, trip, *carry, last=True)
    o_ref[...] = (acc[:D_V] * (1.0 / acc[D_V:D_V + 1])).astype(o_ref.dtype)


def _attention(qt, k, vt):
    B, _, _, S = qt.shape
    n_chunks, tk = vt.shape[2], vt.shape[4]
    tq = Q_TILE
    kslot = lambda h: jnp.where(h < N_HEADS_A, 0, h - (N_HEADS_A - 1))
    vslot = lambda h: jnp.where(h < N_HEADS_A, h // (N_HEADS_A // N_KV_A), h - (N_HEADS_A - N_KV_A))
    return pl.pallas_call(
        functools.partial(_attn_kernel, n_chunks=n_chunks, tk=tk,
                          trip=4 if n_chunks % 4 == 0 and n_chunks >= 8 else 2),
        out_shape=jax.ShapeDtypeStruct((B, N_HEADS, D_V, S), BF16),
        grid=(B, N_HEADS, S // tq),
        in_specs=[
            pl.BlockSpec((None, None, QK_PAD, tq), lambda b, h, i: (b, h, 0, i)),
            pl.BlockSpec((None, None, S, QK_PAD), lambda b, h, i: (b, kslot(h), 0, 0)),
            pl.BlockSpec((None, None, n_chunks, V_ROWS, tk), lambda b, h, i: (b, vslot(h), 0, 0, 0)),
        ],
        out_specs=pl.BlockSpec((None, None, D_V, tq), lambda b, h, i: (b, h, 0, i)),
        scratch_shapes=[pltpu.VMEM((tk, tq + 128), F32), pltpu.VMEM((tk, tq + 128), F32)],
        compiler_params=_cparams(("parallel", "parallel", "arbitrary")),
        name="attn_sweep",
    )(qt, k, vt)


def _post_attn_kernel(o_ref, x_ref, mod_ref, wo_ref, out_ref):
    tm = x_ref.shape[0]
    ot = o_ref[...].reshape(N_HEADS * D_V, tm)
    m = _dot_tn(ot, wo_ref[...])
    out_ref[...] = x_ref[...] + mod_ref[...][2:3] * m


def _post_attn(ot, x, mod_l, wo):
    B, S, _ = x.shape
    tm = TOKEN_TILE
    return pl.pallas_call(
        _post_attn_kernel,
        out_shape=jax.ShapeDtypeStruct(x.shape, F32),
        grid=(B, S // tm),
        in_specs=[
            pl.BlockSpec((None, N_HEADS, D_V, tm), lambda b, i: (b, 0, 0, i)),
            pl.BlockSpec((None, tm, D_MODEL), lambda b, i: (b, i, 0)),
            pl.BlockSpec((None, 6, D_MODEL), lambda b, i: (b, 0, 0)),
            pl.BlockSpec((D_MODEL, D_MODEL), lambda b, i: (0, 0)),
        ],
        out_specs=pl.BlockSpec((None, tm, D_MODEL), lambda b, i: (b, i, 0)),
        compiler_params=_cparams(("parallel", "parallel")),
        name="post_attn",
    )(ot, x, mod_l, wo)


def _f1_kernel(x_ref, mod_ref, gmix_ref, fc_ref, m1_ref, twr_ref, twi_ref, y_ref, z_scr):
    n1 = x_ref.shape[0]
    x = x_ref[...].reshape(n1 * FFT_T, D_MODEL)
    mod = mod_ref[...]
    h = _rms_rows(x, gmix_ref[...]) * (1.0 + mod[1:2]) + mod[0:1]
    hb = h.astype(BF16)
    fc = fc_ref[...]
    for g in range(N_FGROUPS):
        zg = _dot(hb[:, FGROUP * g:FGROUP * (g + 1)], fc)
        z_scr[:, :, FGROUP * g:FGROUP * (g + 1)] = zg[:, :FGROUP].reshape(n1, FFT_T, FGROUP)
        z_scr[:, :, D_MODEL + FGROUP * g:D_MODEL + FGROUP * (g + 1)] = zg[:, FGROUP:].reshape(n1, FFT_T, FGROUP)
    m1 = m1_ref[...]
    for j in range(FFT_T):
        zj = z_scr[:, j, :]
        st = jnp.concatenate([zj[:, :D_MODEL], zj[:, D_MODEL:]], axis=0).astype(BF16)
        y = _dot(m1, st)
        yr, yi = y[:n1], y[n1:]
        tr = jnp.tile(twr_ref[j], (1, D_MODEL // 128))
        ti = jnp.tile(twi_ref[j], (1, D_MODEL // 128))
        y_ref[:, j, 0:D_MODEL] = yr * tr - yi * ti
        y_ref[:, j, D_MODEL:2 * D_MODEL] = yr * ti + yi * tr


def _f2_kernel(y_ref, x_ref, mod_ref, cs_ref, wc_ref, o_ref, f_scr):
    cs = cs_ref[...]
    for j in range(FFT_T):
        yk = y_ref[j]
        st = jnp.concatenate([yk[:, :D_MODEL], yk[:, D_MODEL:]], axis=0).astype(BF16)
        f_scr[FFT_N2 * j:FFT_N2 * (j + 1), :] = _dot(cs, st).astype(BF16)
    m = _dot(f_scr[...], wc_ref[...])
    gt1 = mod_ref[...][2:3]
    for j in range(FFT_T):
        o_ref[:, j, :] = x_ref[:, j, :] + gt1 * m[FFT_N2 * j:FFT_N2 * (j + 1)]


def _dft_constants(S):
    n1 = S // FFT_N2
    k = np.arange(FGROUP)
    ang = 2.0 * np.pi * np.outer(k, k) / FGROUP
    fc = np.concatenate([np.cos(ang), -np.sin(ang)], axis=1) / math.sqrt(FGROUP)
    a = np.arange(n1)
    ang1 = 2.0 * np.pi * np.outer(a, a) / n1
    c1, s1 = np.cos(ang1), np.sin(ang1)
    m1 = np.block([[c1, s1], [-s1, c1]]) / math.sqrt(n1)
    b = np.arange(FFT_N2)
    ang2 = 2.0 * np.pi * np.outer(b, b) / FFT_N2
    cs = np.concatenate([np.cos(ang2), np.sin(ang2)], axis=1) / math.sqrt(FFT_N2)
    angt = 2.0 * np.pi * np.outer(b, a) / S
    twr = np.broadcast_to(np.cos(angt)[:, :, None], (FFT_N2, n1, 128))
    twi = np.broadcast_to(-np.sin(angt)[:, :, None], (FFT_N2, n1, 128))
    return (jnp.asarray(fc, BF16), jnp.asarray(m1, BF16), jnp.asarray(cs, BF16),
            jnp.asarray(twr, F32), jnp.asarray(twi, F32))


def _fourier_layer(x, mod_l, g_mix, wc):
    B, S, _ = x.shape
    n1 = S // FFT_N2
    fc, m1, cs, twr, twi = _dft_constants(S)
    const2 = lambda b, i: (0, 0)
    y = pl.pallas_call(
        _f1_kernel,
        out_shape=jax.ShapeDtypeStruct((B, n1, FFT_N2, 2 * D_MODEL), F32),
        grid=(B, FFT_N2 // FFT_T),
        in_specs=[
            pl.BlockSpec((None, n1, FFT_T, D_MODEL), lambda b, i: (b, 0, i, 0)),
            pl.BlockSpec((None, 6, D_MODEL), lambda b, i: (b, 0, 0)),
            pl.BlockSpec((1, D_MODEL), const2),
            pl.BlockSpec((FGROUP, 2 * FGROUP), const2),
            pl.BlockSpec((2 * n1, 2 * n1), const2),
            pl.BlockSpec((FFT_T, n1, 128), lambda b, i: (i, 0, 0)),
            pl.BlockSpec((FFT_T, n1, 128), lambda b, i: (i, 0, 0)),
        ],
        out_specs=pl.BlockSpec((None, n1, FFT_T, 2 * D_MODEL), lambda b, i: (b, 0, i, 0)),
        scratch_shapes=[pltpu.VMEM((n1, FFT_T, 2 * D_MODEL), F32)],
        compiler_params=_cparams(("parallel", "parallel")),
        name="fourier_stage1",
    )(x.reshape(B, n1, FFT_N2, D_MODEL), mod_l, g_mix, fc, m1, twr, twi)
    out = pl.pallas_call(
        _f2_kernel,
        out_shape=jax.ShapeDtypeStruct((B, FFT_N2, n1, D_MODEL), F32),
        grid=(B, n1 // FFT_T),
        in_specs=[
            pl.BlockSpec((None, FFT_T, FFT_N2, 2 * D_MODEL), lambda b, i: (b, i, 0, 0)),
            pl.BlockSpec((None, FFT_N2, FFT_T, D_MODEL), lambda b, i: (b, 0, i, 0)),
            pl.BlockSpec((None, 6, D_MODEL), lambda b, i: (b, 0, 0)),
            pl.BlockSpec((FFT_N2, 2 * FFT_N2), const2),
            pl.BlockSpec((D_MODEL, D_MODEL), const2),
        ],
        out_specs=pl.BlockSpec((None, FFT_N2, FFT_T, D_MODEL), lambda b, i: (b, 0, i, 0)),
        scratch_shapes=[pltpu.VMEM((FFT_T * FFT_N2, D_MODEL), BF16)],
        compiler_params=_cparams(("parallel", "parallel")),
        name="fourier_stage2",
    )(y, x.reshape(B, FFT_N2, n1, D_MODEL), mod_l, cs, wc)
    return out.reshape(B, S, D_MODEL)


def _route_t(s, sb):
    rows = [sb[e:e + 1] for e in range(N_EXPERTS)]
    gscore = []
    for g in range(N_EGROUPS):
        a, b, c, d = rows[4 * g:4 * g + 4]
        gscore.append(jnp.maximum(jnp.maximum(jnp.maximum(a + b, a + c), jnp.maximum(a + d, b + c)),
                                  jnp.maximum(b + d, c + d)))
    best = gscore[0]
    gsel = jnp.zeros_like(best)
    for g in range(1, N_EGROUPS):
        upd = gscore[g] > best
        gsel = jnp.where(upd, float(g), gsel)
        best = jnp.where(upd, gscore[g], best)
    picked = []
    for e in range(N_EXPERTS):
        g = e // EXPERTS_PER_GROUP
        rank = jnp.zeros_like(best)
        for j in range(4 * g, 4 * g + 4):
            if j == e:
                continue
            ahead = (rows[j] >= rows[e]) if j < e else (rows[j] > rows[e])
            rank = rank + jnp.where(ahead, 1.0, 0.0)
        sel = jnp.where(rank < 1.5, 1.0, 0.0) * jnp.where(gsel == float(g), 1.0, 0.0)
        picked.append(sel * s[e:e + 1])
    total = picked[0]
    for e in range(1, N_EXPERTS):
        total = total + picked[e]
    inv = 1.0 / total
    return [p * inv for p in picked], gsel


def _moe_kernel(x_ref, mod_ref, gffn_ref, wr_ref, bias_ref, wg_ref, wu_ref, wd_ref, gfin_ref,
                out_ref, srt_scr, y_scr, *, final):
    tm = x_ref.shape[0]
    x = x_ref[...]
    mod = mod_ref[...]
    t = _rms_rows(x, gffn_ref[...]) * (1.0 + mod[4:5]) + mod[3:4]
    t_hi = t.astype(BF16)
    t_lo = (t - t_hi.astype(F32)).astype(BF16)
    wr = wr_ref[...]
    a = _dot_nt(wr, t_hi)
    b = _dot_nt(wr[0:N_EXPERTS], t_lo)
    logits = a[0:N_EXPERTS] + a[N_EXPERTS:2 * N_EXPERTS] + b
    s = _sigmoid(logits)
    gates, gsel = _route_t(s, s + bias_ref[...])

    masks = [jnp.where(gsel == float(g), 1.0, 0.0) for g in range(N_EGROUPS)]
    m8 = jnp.concatenate(masks + [jnp.zeros((8 - N_EGROUPS, tm), F32)], axis=0)
    earlier = jnp.where(lax.broadcasted_iota(jnp.int32, (tm, tm), 0)
                        < lax.broadcasted_iota(jnp.int32, (tm, tm), 1), 1.0, 0.0).astype(BF16)
    ranks = _dot(m8.astype(BF16), earlier)
    counts = jnp.sum(m8, axis=1, keepdims=True)
    n_chunks = jnp.floor((counts + (MOE_CHUNK - 1)) * (1.0 / MOE_CHUNK))
    starts = []
    start = jnp.zeros((1, 1), F32)
    for g in range(N_EGROUPS):
        starts.append(start)
        start = start + n_chunks[g:g + 1] * MOE_CHUNK
    pos = masks[0] * (starts[0] + ranks[0:1])
    for g in range(1, N_EGROUPS):
        pos = pos + masks[g] * (starts[g] + ranks[g:g + 1])
    perm = jnp.where(lax.broadcasted_iota(jnp.int32, (MOE_ROWS, tm), 0).astype(F32) == pos,
                     1.0, 0.0).astype(BF16)

    g4 = []
    for j in range(EXPERTS_PER_GROUP):
        r = masks[0] * gates[j]
        for g in range(1, N_EGROUPS):
            r = r + masks[g] * gates[EXPERTS_PER_GROUP * g + j]
        g4.append(r)
    g4t = jnp.concatenate(g4 + [jnp.zeros((128 - EXPERTS_PER_GROUP, tm), F32)], axis=0).T
    src = jnp.concatenate([t_hi, g4t.astype(BF16)], axis=1)
    srt_scr[...] = _dot(perm, src).astype(BF16)
    y_scr[...] = jnp.zeros_like(y_scr)

    for g in range(N_EGROUPS):
        base = starts[g][0, 0].astype(jnp.int32)

        def chunk(j, carry, g=g, base=base):
            off = pl.multiple_of(base + j * MOE_CHUNK, MOE_CHUNK)
            xs = srt_scr[pl.ds(off, MOE_CHUNK), 0:D_MODEL]
            gs = srt_scr[pl.ds(off, MOE_CHUNK), D_MODEL:D_MODEL + 128].astype(F32)
            u1 = _dot(xs, wg_ref[g])
            u2 = _dot(xs, wu_ref[g])
            he = u1 * _sigmoid(u1) * u2
            he = jnp.concatenate(
                [he[:, D_EXPERT * e:D_EXPERT * (e + 1)] * gs[:, e:e + 1] for e in range(EXPERTS_PER_GROUP)],
                axis=1)
            y_scr[pl.ds(off, MOE_CHUNK), :] = _dot(he.astype(BF16), wd_ref[g]).astype(BF16)
            return carry

        lax.fori_loop(0, n_chunks[g, 0].astype(jnp.int32), chunk, 0)

    y = x + mod[5:6] * _dot_tn(perm, y_scr[...])
    if final:
        y = _rms_rows(y, gfin_ref[...])
    out_ref[...] = y


def _moe(x, mod_l, g_ffn, wr, bias, wg, wu, wd, g_final, final):
    B, S, _ = x.shape
    tm = MOE_TILE
    const = lambda b, i: (0, 0)
    wspec = pl.BlockSpec((N_EGROUPS, D_MODEL, D_MODEL), lambda b, i: (0, 0, 0),
                         pipeline_mode=pl.Buffered(1))
    return pl.pallas_call(
        functools.partial(_moe_kernel, final=final),
        out_shape=jax.ShapeDtypeStruct(x.shape, F32),
        grid=(B, S // tm),
        in_specs=[
            pl.BlockSpec((None, tm, D_MODEL), lambda b, i: (b, i, 0)),
            pl.BlockSpec((None, 6, D_MODEL), lambda b, i: (b, 0, 0)),
            pl.BlockSpec((1, D_MODEL), const),
            pl.BlockSpec((2 * N_EXPERTS, D_MODEL), const),
            pl.BlockSpec((N_EXPERTS, 1), const),
            wspec, wspec, wspec,
            pl.BlockSpec((1, D_MODEL), const),
        ],
        out_specs=pl.BlockSpec((None, tm, D_MODEL), lambda b, i: (b, i, 0)),
        scratch_shapes=[
            pltpu.VMEM((MOE_ROWS, D_MODEL + 128), BF16),
            pltpu.VMEM((MOE_ROWS, D_MODEL), BF16),
        ],
        compiler_params=_cparams(("parallel", "parallel")),
        name="moe",
    )(x, mod_l, g_ffn, wr, bias, wg, wu, wd, g_final)


def _rope_tables(S):
    pos = jnp.arange(S, dtype=jnp.int32)
    rows = (pos // GRID_W).astype(F32)[None, :]
    cols = (pos % GRID_W).astype(F32)[None, :]
    tabs = []
    for half in (HEAD_DIM // 2, D_ROPE // 2):
        inv = (ROPE_THETA ** (-jnp.arange(0, half, 2, dtype=F32) / half))[:, None]
        for p in (rows, cols):
            ang = inv * p
            tabs += [jnp.cos(ang), jnp.sin(ang)]
    return jnp.concatenate(tabs, axis=0)


def _group_experts(w, transpose_cols):
    e, a, b = w.shape
    w = w.reshape(N_EGROUPS, EXPERTS_PER_GROUP, a, b)
    if transpose_cols:
        return w.transpose(0, 2, 1, 3).reshape(N_EGROUPS, a, EXPERTS_PER_GROUP * b).astype(BF16)
    return w.reshape(N_EGROUPS, EXPERTS_PER_GROUP * a, b).astype(BF16)


def _trunk(x, mod, p):
    B, S, _ = x.shape
    tab = _rope_tables(S)
    for l in range(DEPTH):
        mod_l = mod[l]
        g_mix = p["g_mix"][l][None, :]
        if l % 2 == 0:
            i = l // 2
            qt, k, vt = _pre_attn(x, mod_l, g_mix, tab, p["wint"][i], p["g_qa"][i][:, None],
                                  p["g_ka"][i][:, None], p["g_cq"][i][:, None], p["wuqt"][i],
                                  p["g_ckv"][i][:, None], p["wukvt"][i])
            ot = _attention(qt, k, vt)
            x = _post_attn(ot, x, mod_l, p["wo"][i])
        else:
            x = _fourier_layer(x, mod_l, g_mix, p["wc"][l // 2])
        x = _moe(x, mod_l, p["g_ffn"][l][None, :], p["wr"], p["bias"], p["wg"][l], p["wu"][l],
                 p["wd"][l], p["g_final"], final=(l == DEPTH - 1))
    return x


def kernel(x_prompt, x_sample, c_prompt, c_sample, w_ada, b_ada, g_mix, g_ffn, w_in, g_qa, g_ka, g_cq,
           w_uq, g_ckv, w_ukv, w_o_attn, w_fourier, w_router, router_bias, w_gate, w_up, w_down, g_final):
    nbp = x_prompt.shape[0]
    mod = _modulation(jnp.concatenate([c_prompt, c_sample], axis=0), w_ada, b_ada)
    mod = mod.reshape(DEPTH, -1, 6, D_MODEL)
    wrt = w_router.T
    wr_hi = wrt.astype(BF16)
    wr_lo = (wrt - wr_hi.astype(F32)).astype(BF16)
    p = {
        "g_mix": g_mix, "g_ffn": g_ffn, "g_qa": g_qa, "g_ka": g_ka, "g_cq": g_cq, "g_ckv": g_ckv,
        "wint": jnp.swapaxes(w_in, 1, 2).astype(BF16),
        "wuqt": jnp.swapaxes(w_uq, 1, 2).astype(BF16),
        "wukvt": jnp.swapaxes(w_ukv, 1, 2).astype(BF16),
        "wo": w_o_attn.astype(BF16),
        "wc": w_fourier.astype(BF16),
        "wr": jnp.concatenate([wr_hi, wr_lo], axis=0),
        "bias": router_bias[:, None],
        "wg": [_group_experts(w_gate[l], True) for l in range(DEPTH)],
        "wu": [_group_experts(w_up[l], True) for l in range(DEPTH)],
        "wd": [_group_experts(w_down[l], False) for l in range(DEPTH)],
        "g_final": g_final[None, :],
    }
    y_prompt = _trunk(x_prompt, mod[:, :nbp], p)
    y_sample = _trunk(x_sample, mod[:, nbp:], p)
    return (y_prompt, y_sample)
```

```python
import functools
import math

import numpy as np
import jax
import jax.numpy as jnp
from jax import lax
from jax.experimental import pallas as pl
from jax.experimental.pallas import tpu as pltpu

F32 = jnp.float32
BF16 = jnp.bfloat16
QK_DTYPE = jnp.float8_e4m3fn

D_MODEL = 1024
DEPTH = 4
GRID_W = 64
HEAD_DIM = 64
N_HEADS_A = 8
N_KV_A = 2
N_HEADS_B = 8
D_NOPE = 64
D_ROPE = 32
D_V = 64
D_CQ = 384
D_CKV = 256
N_FGROUPS = 4
FGROUP = D_MODEL // N_FGROUPS
ROPE_THETA = 10000.0
N_EXPERTS = 16
N_EGROUPS = 4
EXPERTS_PER_GROUP = 4
D_EXPERT = 256
EPS = 1e-6
IN_SIZES = (N_HEADS_A * HEAD_DIM, N_KV_A * HEAD_DIM, N_KV_A * HEAD_DIM, D_CQ, D_CKV, D_ROPE)
IN_WIDTH = sum(IN_SIZES)
IN_OFFS = tuple(int(v) for v in np.cumsum((0,) + IN_SIZES))

N_HEADS = N_HEADS_A + N_HEADS_B
N_KSLOTS = 1 + N_HEADS_B
N_VSLOTS = N_KV_A + N_HEADS_B
QK_PAD = 128
LOG2E = 1.4426950408889634
SCALE_A = HEAD_DIM ** -0.5 * LOG2E
SCALE_B = (D_NOPE + D_ROPE) ** -0.5 * LOG2E
NEG_BIG = -1e30

FFT_N2 = 128
FFT_T = 8

TOKEN_TILE = 512
Q_TILE = 512
V_ROWS = D_V + 16
MOE_TILE = 512
MOE_CHUNK = 128
MOE_ROWS = MOE_TILE + N_EGROUPS * MOE_CHUNK
VMEM_LIMIT = 56 * 1024 * 1024


def _cparams(sem, flags=None):
    return pltpu.CompilerParams(dimension_semantics=sem, vmem_limit_bytes=VMEM_LIMIT, flags=flags)


def _sigmoid(x):
    return 1.0 / (1.0 + jnp.exp(-x))


def _rms_rows(x, g):
    ms = jnp.mean(x * x, axis=-1, keepdims=True)
    return x * lax.rsqrt(ms + EPS) * g


def _rms_cols(xt, gcol):
    ms = jnp.mean(xt * xt, axis=0, keepdims=True)
    return xt * lax.rsqrt(ms + EPS) * gcol


def _dot(a, b):
    return jnp.dot(a, b, preferred_element_type=F32)


def _dot_nt(a, b):
    return lax.dot_general(a, b, (((1,), (1,)), ((), ())), preferred_element_type=F32)


def _dot_tn(a, b):
    return lax.dot_general(a, b, (((0,), (0,)), ((), ())), preferred_element_type=F32)


def _mod_kernel(c_ref, w_ref, b_ref, o_ref):
    c = c_ref[...]
    ca = c * _sigmoid(c)
    o_ref[...] = _dot(ca, w_ref[...]) + b_ref[...]


def _modulation(c_all, w_ada, b_ada):
    nb = c_all.shape[0]
    tn = 1536
    return pl.pallas_call(
        _mod_kernel,
        out_shape=jax.ShapeDtypeStruct((DEPTH, nb, 6 * D_MODEL), F32),
        grid=(DEPTH, 6 * D_MODEL // tn),
        in_specs=[
            pl.BlockSpec((nb, D_MODEL), lambda l, j: (0, 0)),
            pl.BlockSpec((None, D_MODEL, tn), lambda l, j: (l, 0, j)),
            pl.BlockSpec((None, 1, tn), lambda l, j: (l, 0, j)),
        ],
        out_specs=pl.BlockSpec((None, nb, tn), lambda l, j: (l, 0, j)),
        compiler_params=_cparams(("arbitrary", "arbitrary")),
        name="adaln_mod",
    )(c_all, w_ada, b_ada.reshape(DEPTH, 1, 6 * D_MODEL))


def _rope_t(xt, c, s):
    n = c.shape[0]
    x1, x2 = xt[:n], xt[n:]
    return jnp.concatenate([x1 * c - x2 * s, x1 * s + x2 * c], axis=0)


def _axial_t(xt, cr, sr, cc, sc):
    d2 = xt.shape[0] // 2
    return jnp.concatenate([_rope_t(xt[:d2], cr, sr), _rope_t(xt[d2:], cc, sc)], axis=0)


def _pre_attn_kernel(x_ref, mod_ref, gmix_ref, tab_ref, wint_ref, gqa_ref, gka_ref, gcq_ref,
                     wuqt_ref, gckv_ref, wukvt_ref, q_ref, k_ref, v_ref):
    x = x_ref[...]
    tm = x.shape[0]
    mod = mod_ref[...]
    h = _rms_rows(x, gmix_ref[...]) * (1.0 + mod[1:2]) + mod[0:1]
    zt = _dot_nt(wint_ref[...], h.astype(BF16))

    tab = tab_ref[...]
    ra = (tab[0:16], tab[16:32], tab[32:48], tab[48:64])
    rb = (tab[64:72], tab[72:80], tab[80:88], tab[88:96])
    zeros64 = jnp.zeros((HEAD_DIM, tm), F32)
    zeros32 = jnp.zeros((QK_PAD - D_NOPE - D_ROPE, tm), F32)

    o_q, o_k, o_v, o_cq, o_ckv, o_kr = IN_OFFS[:6]
    gqa = gqa_ref[...]
    for hh in range(N_HEADS_A):
        qh = _axial_t(_rms_cols(zt[o_q + HEAD_DIM * hh:o_q + HEAD_DIM * (hh + 1)], gqa), *ra) * SCALE_A
        parts = [qh, zeros64] if hh // (N_HEADS_A // N_KV_A) == 0 else [zeros64, qh]
        q_ref[hh] = jnp.concatenate(parts, axis=0).astype(QK_DTYPE)
    gka = gka_ref[...]
    kts = [jnp.concatenate(
        [_axial_t(_rms_cols(zt[o_k + HEAD_DIM * g:o_k + HEAD_DIM * (g + 1)], gka), *ra)
         for g in range(N_KV_A)], axis=0)]
    ones16 = jnp.ones((V_ROWS - D_V, tm), F32)
    for g in range(N_KV_A):
        v_ref[g] = jnp.concatenate(
            [zt[o_v + HEAD_DIM * g:o_v + HEAD_DIM * (g + 1)], ones16], axis=0).astype(BF16)

    cqn = _rms_cols(zt[o_cq:o_ckv], gcq_ref[...]).astype(BF16)
    qbt = _dot(wuqt_ref[...], cqn)
    dq = D_NOPE + D_ROPE
    for hh in range(N_HEADS_B):
        nope = qbt[dq * hh:dq * hh + D_NOPE]
        rp = _axial_t(qbt[dq * hh + D_NOPE:dq * (hh + 1)], *rb)
        q_ref[N_HEADS_A + hh] = (jnp.concatenate([nope, rp, zeros32], axis=0) * SCALE_B).astype(QK_DTYPE)
    ckvn = _rms_cols(zt[o_ckv:o_kr], gckv_ref[...]).astype(BF16)
    kvt = _dot(wukvt_ref[...], ckvn)
    kr = _axial_t(zt[o_kr:o_kr + D_ROPE], *rb)
    dkv = D_NOPE + D_V
    for hh in range(N_HEADS_B):
        kts.append(jnp.concatenate([kvt[dkv * hh:dkv * hh + D_NOPE], kr, zeros32], axis=0))
        v_ref[N_KV_A + hh] = jnp.concatenate(
            [kvt[dkv * hh + D_NOPE:dkv * (hh + 1)], ones16], axis=0).astype(BF16)

    k_all = jnp.concatenate(kts, axis=0).T
    for i in range(N_KSLOTS):
        k_ref[i] = k_all[:, QK_PAD * i:QK_PAD * (i + 1)].astype(QK_DTYPE)


def _pre_attn(x, mod_l, g_mix, tab, wint, gqa, gka, gcq, wuqt, gckv, wukvt):
    B, S, _ = x.shape
    tm = TOKEN_TILE
    nt = S // tm
    const = lambda b, i: (0, 0)
    return pl.pallas_call(
        _pre_attn_kernel,
        out_shape=(
            jax.ShapeDtypeStruct((B, N_HEADS, QK_PAD, S), QK_DTYPE),
            jax.ShapeDtypeStruct((B, N_KSLOTS, S, QK_PAD), QK_DTYPE),
            jax.ShapeDtypeStruct((B, N_VSLOTS, nt, V_ROWS, tm), BF16),
        ),
        grid=(B, nt),
        in_specs=[
            pl.BlockSpec((None, tm, D_MODEL), lambda b, i: (b, i, 0)),
            pl.BlockSpec((None, 6, D_MODEL), lambda b, i: (b, 0, 0)),
            pl.BlockSpec((1, D_MODEL), const),
            pl.BlockSpec((96, tm), lambda b, i: (0, i)),
            pl.BlockSpec((IN_WIDTH, D_MODEL), const),
            pl.BlockSpec((HEAD_DIM, 1), const),
            pl.BlockSpec((HEAD_DIM, 1), const),
            pl.BlockSpec((D_CQ, 1), const),
            pl.BlockSpec((N_HEADS_B * (D_NOPE + D_ROPE), D_CQ), const),
            pl.BlockSpec((D_CKV, 1), const),
            pl.BlockSpec((N_HEADS_B * (D_NOPE + D_V), D_CKV), const),
        ],
        out_specs=(
            pl.BlockSpec((None, N_HEADS, QK_PAD, tm), lambda b, i: (b, 0, 0, i)),
            pl.BlockSpec((None, N_KSLOTS, tm, QK_PAD), lambda b, i: (b, 0, i, 0)),
            pl.BlockSpec((None, N_VSLOTS, None, V_ROWS, tm), lambda b, i: (b, 0, i, 0, 0)),
        ),
        compiler_params=_cparams(("parallel", "parallel")),
        name="pre_attn",
    )(x, mod_l, g_mix, tab, wint, gqa, gka, gcq, wuqt, gckv, wukvt)


def _attn_kernel(q_ref, k_ref, v_ref, o_ref, sa_scr, sb_scr, *, n_chunks, tk, trip):
    qt = q_ref[...]
    tq = qt.shape[1]
    bufs = (sa_scr, sb_scr)

    def scores(c, s_ref):
        off = c * tk if isinstance(c, int) else pl.multiple_of(c * tk, tk)
        s = _dot(k_ref[pl.ds(off, tk), :], qt)
        s_ref[:, 0:tq] = s
        return jnp.max(s, axis=0, keepdims=True)

    def update(c, s_ref, cm, m, acc):
        m_new = jnp.maximum(m, cm)
        p = jnp.exp2(s_ref[:, 0:tq] - m_new).astype(BF16)
        return m_new, acc * jnp.exp2(m - m_new) + _dot(v_ref[c], p)

    def steps(c0, count, cm, m, acc, last):
        for u in range(count):
            if not (last and u == count - 1):
                cm_next = scores(c0 + u + 1, bufs[(u + 1) % 2])
            m, acc = update(c0 + u, bufs[u % 2], cm, m, acc)
            cm = cm_next
        return cm, m, acc

    def body(i, carry):
        return steps(i * trip, trip, *carry, last=False)

    m0 = jnp.full((1, tq), NEG_BIG, F32)
    acc0 = jnp.zeros((V_ROWS, tq), F32)
    n_trips = (n_chunks - 2) // trip
    carry = lax.fori_loop(0, n_trips, body, (scores(0, sa_scr), m0, acc0))
    _, _, acc = steps(n_trips * trip, n_chunks - n_trips * trip, *carry, last=True)
    o_ref[...] = (acc[:D_V] * (1.0 / acc[D_V:D_V + 1])).astype(o_ref.dtype)


def _attention(qt, k, vt):
    B, _, _, S = qt.shape
    n_chunks, tk = vt.shape[2], vt.shape[4]
    tq = Q_TILE
    kslot = lambda h: jnp.where(h < N_HEADS_A, 0, h - (N_HEADS_A - 1))
    vslot = lambda h: jnp.where(h < N_HEADS_A, h // (N_HEADS_A // N_KV_A), h - (N_HEADS_A - N_KV_A))
    return pl.pallas_call(
        functools.partial(_attn_kernel, n_chunks=n_chunks, tk=tk,
                          trip=6 if n_chunks >= 8 else 2),
        out_shape=jax.ShapeDtypeStruct((B, N_HEADS, D_V, S), BF16),
        grid=(B, N_HEADS, S // tq),
        in_specs=[
            pl.BlockSpec((None, None, QK_PAD, tq), lambda b, h, i: (b, h, 0, i)),
            pl.BlockSpec((None, None, S, QK_PAD), lambda b, h, i: (b, kslot(h), 0, 0)),
            pl.BlockSpec((None, None, n_chunks, V_ROWS, tk), lambda b, h, i: (b, vslot(h), 0, 0, 0)),
        ],
        out_specs=pl.BlockSpec((None, None, D_V, tq), lambda b, h, i: (b, h, 0, i)),
        scratch_shapes=[pltpu.VMEM((tk, tq + 128), F32), pltpu.VMEM((tk, tq + 128), F32)],
        compiler_params=_cparams(("parallel", "parallel", "arbitrary")),
        name="attn_sweep",
    )(qt, k, vt)


def _post_attn_kernel(o_ref, x_ref, mod_ref, wo_ref, out_ref):
    tm = x_ref.shape[0]
    ot = o_ref[...].reshape(N_HEADS * D_V, tm)
    m = _dot_tn(ot, wo_ref[...])
    out_ref[...] = x_ref[...] + mod_ref[...][2:3] * m


def _post_attn(ot, x, mod_l, wo):
    B, S, _ = x.shape
    tm = TOKEN_TILE
    return pl.pallas_call(
        _post_attn_kernel,
        out_shape=jax.ShapeDtypeStruct(x.shape, F32),
        grid=(B, S // tm),
        in_specs=[
            pl.BlockSpec((None, N_HEADS, D_V, tm), lambda b, i: (b, 0, 0, i)),
            pl.BlockSpec((None, tm, D_MODEL), lambda b, i: (b, i, 0)),
            pl.BlockSpec((None, 6, D_MODEL), lambda b, i: (b, 0, 0)),
            pl.BlockSpec((D_MODEL, D_MODEL), lambda b, i: (0, 0)),
        ],
        out_specs=pl.BlockSpec((None, tm, D_MODEL), lambda b, i: (b, i, 0)),
        compiler_params=_cparams(("parallel", "parallel")),
        name="post_attn",
    )(ot, x, mod_l, wo)


def _f1_kernel(x_ref, mod_ref, gmix_ref, fc_ref, m1_ref, twr_ref, twi_ref, y_ref, z_scr):
    n1 = x_ref.shape[0]
    x = x_ref[...].reshape(n1 * FFT_T, D_MODEL)
    mod = mod_ref[...]
    h = _rms_rows(x, gmix_ref[...]) * (1.0 + mod[1:2]) + mod[0:1]
    hb = h.astype(BF16)
    fc = fc_ref[...]
    for g in range(N_FGROUPS):
        zg = _dot(hb[:, FGROUP * g:FGROUP * (g + 1)], fc)
        z_scr[:, :, FGROUP * g:FGROUP * (g + 1)] = zg[:, :FGROUP].reshape(n1, FFT_T, FGROUP)
        z_scr[:, :, D_MODEL + FGROUP * g:D_MODEL + FGROUP * (g + 1)] = zg[:, FGROUP:].reshape(n1, FFT_T, FGROUP)
    m1 = m1_ref[...]
    for j in range(FFT_T):
        zj = z_scr[:, j, :]
        st = jnp.concatenate([zj[:, :D_MODEL], zj[:, D_MODEL:]], axis=0).astype(BF16)
        y = _dot(m1, st)
        yr, yi = y[:n1], y[n1:]
        tr = jnp.tile(twr_ref[j], (1, D_MODEL // 128))
        ti = jnp.tile(twi_ref[j], (1, D_MODEL // 128))
        y_ref[:, j, 0:D_MODEL] = yr * tr - yi * ti
        y_ref[:, j, D_MODEL:2 * D_MODEL] = yr * ti + yi * tr


def _f2_kernel(y_ref, x_ref, mod_ref, cs_ref, wc_ref, o_ref, f_scr):
    cs = cs_ref[...]
    for j in range(FFT_T):
        yk = y_ref[j]
        st = jnp.concatenate([yk[:, :D_MODEL], yk[:, D_MODEL:]], axis=0).astype(BF16)
        f_scr[FFT_N2 * j:FFT_N2 * (j + 1), :] = _dot(cs, st).astype(BF16)
    m = _dot(f_scr[...], wc_ref[...])
    gt1 = mod_ref[...][2:3]
    for j in range(FFT_T):
        o_ref[:, j, :] = x_ref[:, j, :] + gt1 * m[FFT_N2 * j:FFT_N2 * (j + 1)]


def _dft_constants(S):
    n1 = S // FFT_N2
    k = np.arange(FGROUP)
    ang = 2.0 * np.pi * np.outer(k, k) / FGROUP
    fc = np.concatenate([np.cos(ang), -np.sin(ang)], axis=1) / math.sqrt(FGROUP)
    a = np.arange(n1)
    ang1 = 2.0 * np.pi * np.outer(a, a) / n1
    c1, s1 = np.cos(ang1), np.sin(ang1)
    m1 = np.block([[c1, s1], [-s1, c1]]) / math.sqrt(n1)
    b = np.arange(FFT_N2)
    ang2 = 2.0 * np.pi * np.outer(b, b) / FFT_N2
    cs = np.concatenate([np.cos(ang2), np.sin(ang2)], axis=1) / math.sqrt(FFT_N2)
    angt = 2.0 * np.pi * np.outer(b, a) / S
    twr = np.broadcast_to(np.cos(angt)[:, :, None], (FFT_N2, n1, 128))
    twi = np.broadcast_to(-np.sin(angt)[:, :, None], (FFT_N2, n1, 128))
    return (jnp.asarray(fc, BF16), jnp.asarray(m1, BF16), jnp.asarray(cs, BF16),
            jnp.asarray(twr, F32), jnp.asarray(twi, F32))


def _fourier_layer(x, mod_l, g_mix, wc):
    B, S, _ = x.shape
    n1 = S // FFT_N2
    fc, m1, cs, twr, twi = _dft_constants(S)
    const2 = lambda b, i: (0, 0)
    y = pl.pallas_call(
        _f1_kernel,
        out_shape=jax.ShapeDtypeStruct((B, n1, FFT_N2, 2 * D_MODEL), F32),
        grid=(B, FFT_N2 // FFT_T),
        in_specs=[
            pl.BlockSpec((None, n1, FFT_T, D_MODEL), lambda b, i: (b, 0, i, 0)),
            pl.BlockSpec((None, 6, D_MODEL), lambda b, i: (b, 0, 0)),
            pl.BlockSpec((1, D_MODEL), const2),
            pl.BlockSpec((FGROUP, 2 * FGROUP), const2),
            pl.BlockSpec((2 * n1, 2 * n1), const2),
            pl.BlockSpec((FFT_T, n1, 128), lambda b, i: (i, 0, 0)),
            pl.BlockSpec((FFT_T, n1, 128), lambda b, i: (i, 0, 0)),
        ],
        out_specs=pl.BlockSpec((None, n1, FFT_T, 2 * D_MODEL), lambda b, i: (b, 0, i, 0)),
        scratch_shapes=[pltpu.VMEM((n1, FFT_T, 2 * D_MODEL), F32)],
        compiler_params=_cparams(("parallel", "parallel")),
        name="fourier_stage1",
    )(x.reshape(B, n1, FFT_N2, D_MODEL), mod_l, g_mix, fc, m1, twr, twi)
    out = pl.pallas_call(
        _f2_kernel,
        out_shape=jax.ShapeDtypeStruct((B, FFT_N2, n1, D_MODEL), F32),
        grid=(B, n1 // FFT_T),
        in_specs=[
            pl.BlockSpec((None, FFT_T, FFT_N2, 2 * D_MODEL), lambda b, i: (b, i, 0, 0)),
            pl.BlockSpec((None, FFT_N2, FFT_T, D_MODEL), lambda b, i: (b, 0, i, 0)),
            pl.BlockSpec((None, 6, D_MODEL), lambda b, i: (b, 0, 0)),
            pl.BlockSpec((FFT_N2, 2 * FFT_N2), const2),
            pl.BlockSpec((D_MODEL, D_MODEL), const2),
        ],
        out_specs=pl.BlockSpec((None, FFT_N2, FFT_T, D_MODEL), lambda b, i: (b, 0, i, 0)),
        scratch_shapes=[pltpu.VMEM((FFT_T * FFT_N2, D_MODEL), BF16)],
        compiler_params=_cparams(("parallel", "parallel")),
        name="fourier_stage2",
    )(y, x.reshape(B, FFT_N2, n1, D_MODEL), mod_l, cs, wc)
    return out.reshape(B, S, D_MODEL)


def _route_t(s, sb):
    rows = [sb[e:e + 1] for e in range(N_EXPERTS)]
    gscore = []
    for g in range(N_EGROUPS):
        a, b, c, d = rows[4 * g:4 * g + 4]
        gscore.append(jnp.maximum(jnp.maximum(jnp.maximum(a + b, a + c), jnp.maximum(a + d, b + c)),
                                  jnp.maximum(b + d, c + d)))
    best = gscore[0]
    gsel = jnp.zeros_like(best)
    for g in range(1, N_EGROUPS):
        upd = gscore[g] > best
        gsel = jnp.where(upd, float(g), gsel)
        best = jnp.where(upd, gscore[g], best)
    picked = []
    for e in range(N_EXPERTS):
        g = e // EXPERTS_PER_GROUP
        rank = jnp.zeros_like(best)
        for j in range(4 * g, 4 * g + 4):
            if j == e:
                continue
            ahead = (rows[j] >= rows[e]) if j < e else (rows[j] > rows[e])
            rank = rank + jnp.where(ahead, 1.0, 0.0)
        sel = jnp.where(rank < 1.5, 1.0, 0.0) * jnp.where(gsel == float(g), 1.0, 0.0)
        picked.append(sel * s[e:e + 1])
    total = picked[0]
    for e in range(1, N_EXPERTS):
        total = total + picked[e]
    inv = 1.0 / total
    return [p * inv for p in picked], gsel


def _moe_kernel(x_ref, mod_ref, gffn_ref, wr_ref, bias_ref, wg_ref, wu_ref, wd_ref, gfin_ref,
                out_ref, srt_scr, y_scr, *, final):
    tm = x_ref.shape[0]
    x = x_ref[...]
    mod = mod_ref[...]
    t = _rms_rows(x, gffn_ref[...]) * (1.0 + mod[4:5]) + mod[3:4]
    t_hi = t.astype(BF16)
    t_lo = (t - t_hi.astype(F32)).astype(BF16)
    wr = wr_ref[...]
    a = _dot_nt(wr, t_hi)
    b = _dot_nt(wr[0:N_EXPERTS], t_lo)
    logits = a[0:N_EXPERTS] + a[N_EXPERTS:2 * N_EXPERTS] + b
    s = _sigmoid(logits)
    gates, gsel = _route_t(s, s + bias_ref[...])

    masks = [jnp.where(gsel == float(g), 1.0, 0.0) for g in range(N_EGROUPS)]
    m8 = jnp.concatenate(masks + [jnp.zeros((8 - N_EGROUPS, tm), F32)], axis=0)
    earlier = jnp.where(lax.broadcasted_iota(jnp.int32, (tm, tm), 0)
                        < lax.broadcasted_iota(jnp.int32, (tm, tm), 1), 1.0, 0.0).astype(BF16)
    ranks = _dot(m8.astype(BF16), earlier)
    counts = jnp.sum(m8, axis=1, keepdims=True)
    n_chunks = jnp.floor((counts + (MOE_CHUNK - 1)) * (1.0 / MOE_CHUNK))
    starts = []
    start = jnp.zeros((1, 1), F32)
    for g in range(N_EGROUPS):
        starts.append(start)
        start = start + n_chunks[g:g + 1] * MOE_CHUNK
    pos = masks[0] * (starts[0] + ranks[0:1])
    for g in range(1, N_EGROUPS):
        pos = pos + masks[g] * (starts[g] + ranks[g:g + 1])
    perm = jnp.where(lax.broadcasted_iota(jnp.int32, (MOE_ROWS, tm), 0).astype(F32) == pos,
                     1.0, 0.0).astype(BF16)

    g4 = []
    for j in range(EXPERTS_PER_GROUP):
        r = masks[0] * gates[j]
        for g in range(1, N_EGROUPS):
            r = r + masks[g] * gates[EXPERTS_PER_GROUP * g + j]
        g4.append(r)
    g4t = jnp.concatenate(g4 + [jnp.zeros((128 - EXPERTS_PER_GROUP, tm), F32)], axis=0).T
    src = jnp.concatenate([t_hi, g4t.astype(BF16)], axis=1)
    srt_scr[...] = _dot(perm, src).astype(BF16)
    y_scr[...] = jnp.zeros_like(y_scr)

    for g in range(N_EGROUPS):
        base = starts[g][0, 0].astype(jnp.int32)

        def chunk(j, carry, g=g, base=base):
            off = pl.multiple_of(base + j * MOE_CHUNK, MOE_CHUNK)
            xs = srt_scr[pl.ds(off, MOE_CHUNK), 0:D_MODEL]
            gs = srt_scr[pl.ds(off, MOE_CHUNK), D_MODEL:D_MODEL + 128].astype(F32)
            u1 = _dot(xs, wg_ref[g])
            u2 = _dot(xs, wu_ref[g])
            he = u1 * _sigmoid(u1) * u2
            he = jnp.concatenate(
                [he[:, D_EXPERT * e:D_EXPERT * (e + 1)] * gs[:, e:e + 1] for e in range(EXPERTS_PER_GROUP)],
                axis=1)
            y_scr[pl.ds(off, MOE_CHUNK), :] = _dot(he.astype(BF16), wd_ref[g]).astype(BF16)
            return carry

        lax.fori_loop(0, n_chunks[g, 0].astype(jnp.int32), chunk, 0)

    y = x + mod[5:6] * _dot_tn(perm, y_scr[...])
    if final:
        y = _rms_rows(y, gfin_ref[...])
    out_ref[...] = y


def _moe(x, mod_l, g_ffn, wr, bias, wg, wu, wd, g_final, final):
    B, S, _ = x.shape
    tm = MOE_TILE
    const = lambda b, i: (0, 0)
    wspec = pl.BlockSpec((N_EGROUPS, D_MODEL, D_MODEL), lambda b, i: (0, 0, 0),
                         pipeline_mode=pl.Buffered(1))
    return pl.pallas_call(
        functools.partial(_moe_kernel, final=final),
        out_shape=jax.ShapeDtypeStruct(x.shape, F32),
        grid=(B, S // tm),
        in_specs=[
            pl.BlockSpec((None, tm, D_MODEL), lambda b, i: (b, i, 0)),
            pl.BlockSpec((None, 6, D_MODEL), lambda b, i: (b, 0, 0)),
            pl.BlockSpec((1, D_MODEL), const),
            pl.BlockSpec((2 * N_EXPERTS, D_MODEL), const),
            pl.BlockSpec((N_EXPERTS, 1), const),
            wspec, wspec, wspec,
            pl.BlockSpec((1, D_MODEL), const),
        ],
        out_specs=pl.BlockSpec((None, tm, D_MODEL), lambda b, i: (b, i, 0)),
        scratch_shapes=[
            pltpu.VMEM((MOE_ROWS, D_MODEL + 128), BF16),
            pltpu.VMEM((MOE_ROWS, D_MODEL), BF16),
        ],
        compiler_params=_cparams(("parallel", "parallel")),
        name="moe",
    )(x, mod_l, g_ffn, wr, bias, wg, wu, wd, g_final)


def _rope_tables(S):
    pos = jnp.arange(S, dtype=jnp.int32)
    rows = (pos // GRID_W).astype(F32)[None, :]
    cols = (pos % GRID_W).astype(F32)[None, :]
    tabs = []
    for half in (HEAD_DIM // 2, D_ROPE // 2):
        inv = (ROPE_THETA ** (-jnp.arange(0, half, 2, dtype=F32) / half))[:, None]
        for p in (rows, cols):
            ang = inv * p
            tabs += [jnp.cos(ang), jnp.sin(ang)]
    return jnp.concatenate(tabs, axis=0)


def _group_experts(w, transpose_cols):
    e, a, b = w.shape
    w = w.reshape(N_EGROUPS, EXPERTS_PER_GROUP, a, b)
    if transpose_cols:
        return w.transpose(0, 2, 1, 3).reshape(N_EGROUPS, a, EXPERTS_PER_GROUP * b).astype(BF16)
    return w.reshape(N_EGROUPS, EXPERTS_PER_GROUP * a, b).astype(BF16)


def _trunk(x, mod, p):
    B, S, _ = x.shape
    tab = _rope_tables(S)
    for l in range(DEPTH):
        mod_l = mod[l]
        g_mix = p["g_mix"][l][None, :]
        if l % 2 == 0:
            i = l // 2
            qt, k, vt = _pre_attn(x, mod_l, g_mix, tab, p["wint"][i], p["g_qa"][i][:, None],
                                  p["g_ka"][i][:, None], p["g_cq"][i][:, None], p["wuqt"][i],
                                  p["g_ckv"][i][:, None], p["wukvt"][i])
            ot = _attention(qt, k, vt)
            x = _post_attn(ot, x, mod_l, p["wo"][i])
        else:
            x = _fourier_layer(x, mod_l, g_mix, p["wc"][l // 2])
        x = _moe(x, mod_l, p["g_ffn"][l][None, :], p["wr"], p["bias"], p["wg"][l], p["wu"][l],
                 p["wd"][l], p["g_final"], final=(l == DEPTH - 1))
    return x


def kernel(x_prompt, x_sample, c_prompt, c_sample, w_ada, b_ada, g_mix, g_ffn, w_in, g_qa, g_ka, g_cq,
           w_uq, g_ckv, w_ukv, w_o_attn, w_fourier, w_router, router_bias, w_gate, w_up, w_down, g_final):
    nbp = x_prompt.shape[0]
    mod = _modulation(jnp.concatenate([c_prompt, c_sample], axis=0), w_ada, b_ada)
    mod = mod.reshape(DEPTH, -1, 6, D_MODEL)
    wrt = w_router.T
    wr_hi = wrt.astype(BF16)
    wr_lo = (wrt - wr_hi.astype(F32)).astype(BF16)
    p = {
        "g_mix": g_mix, "g_ffn": g_ffn, "g_qa": g_qa, "g_ka": g_ka, "g_cq": g_cq, "g_ckv": g_ckv,
        "wint": jnp.swapaxes(w_in, 1, 2).astype(BF16),
        "wuqt": jnp.swapaxes(w_uq, 1, 2).astype(BF16),
        "wukvt": jnp.swapaxes(w_ukv, 1, 2).astype(BF16),
        "wo": w_o_attn.astype(BF16),
        "wc": w_fourier.astype(BF16),
        "wr": jnp.concatenate([wr_hi, wr_lo], axis=0),
        "bias": router_bias[:, None],
        "wg": [_group_experts(w_gate[l], True) for l in range(DEPTH)],
        "wu": [_group_experts(w_up[l], True) for l in range(DEPTH)],
        "wd": [_group_experts(w_down[l], False) for l in range(DEPTH)],
        "g_final": g_final[None, :],
    }
    y_prompt = _trunk(x_prompt, mod[:, :nbp], p)
    y_sample = _trunk(x_sample, mod[:, nbp:], p)
    return (y_prompt, y_sample)
```

```python
import functools
import math

import numpy as np
import jax
import jax.numpy as jnp
from jax import lax
from jax.experimental import pallas as pl
from jax.experimental.pallas import tpu as pltpu

F32 = jnp.float32
BF16 = jnp.bfloat16
QK_DTYPE = jnp.float8_e4m3fn
QK_MAX = float(jnp.finfo(QK_DTYPE).max)
LANES = 128
SUBLANES = 8

D_MODEL = 1024
DEPTH = 4
GRID_W = 64
HEAD_DIM = 64
N_HEADS_A = 8
N_KV_A = 2
N_HEADS_B = 8
D_NOPE = 64
D_ROPE = 32
D_V = 64
D_CQ = 384
D_CKV = 256
N_FGROUPS = 4
FGROUP = D_MODEL // N_FGROUPS
ROPE_THETA = 10000.0
N_EXPERTS = 16
N_EGROUPS = 4
EXPERTS_PER_GROUP = 4
D_EXPERT = 256
EPS = 1e-6
IN_SIZES = (N_HEADS_A * HEAD_DIM, N_KV_A * HEAD_DIM, N_KV_A * HEAD_DIM, D_CQ, D_CKV, D_ROPE)
IN_WIDTH = sum(IN_SIZES)
IN_OFFS = tuple(int(v) for v in np.cumsum((0,) + IN_SIZES))

N_HEADS = N_HEADS_A + N_HEADS_B
N_KSLOTS = 1 + N_HEADS_B
N_VSLOTS = N_KV_A + N_HEADS_B
QK_PAD = LANES
LOG2E = 1.4426950408889634
SCALE_A = HEAD_DIM ** -0.5 * LOG2E
SCALE_B = (D_NOPE + D_ROPE) ** -0.5 * LOG2E
NEG_BIG = -1e30

FFT_N2 = 128
FFT_T = SUBLANES

TOKEN_TILE = 512
Q_TILE = 512
V_ROWS = D_V + 16
MOE_TILE = 512
MOE_CHUNK = 128
MOE_ROWS = MOE_TILE + (N_EGROUPS - 1) * MOE_CHUNK
VMEM_LIMIT = 56 * 1024 * 1024


def _cparams(sem):
    return pltpu.CompilerParams(dimension_semantics=sem, vmem_limit_bytes=VMEM_LIMIT)


def _to_qk(x):
    return jnp.clip(x, -QK_MAX, QK_MAX).astype(QK_DTYPE)


def _sigmoid(x):
    return 1.0 / (1.0 + jnp.exp(-x))


def _rms_rows(x, g):
    ms = jnp.mean(x * x, axis=-1, keepdims=True)
    return x * lax.rsqrt(ms + EPS) * g


def _rms_cols(xt, gcol):
    ms = jnp.mean(xt * xt, axis=0, keepdims=True)
    return xt * lax.rsqrt(ms + EPS) * gcol


def _dot(a, b):
    return jnp.dot(a, b, preferred_element_type=F32)


def _dot_nt(a, b):
    return lax.dot_general(a, b, (((1,), (1,)), ((), ())), preferred_element_type=F32)


def _dot_tn(a, b):
    return lax.dot_general(a, b, (((0,), (0,)), ((), ())), preferred_element_type=F32)


def _mod_kernel(c_ref, w_ref, b_ref, o_ref):
    c = c_ref[...]
    ca = c * _sigmoid(c)
    o_ref[...] = _dot(ca, w_ref[...]) + b_ref[...]


def _modulation(c_all, w_ada, b_ada):
    nb = c_all.shape[0]
    tn = 1536
    return pl.pallas_call(
        _mod_kernel,
        out_shape=jax.ShapeDtypeStruct((DEPTH, nb, 6 * D_MODEL), F32),
        grid=(DEPTH, 6 * D_MODEL // tn),
        in_specs=[
            pl.BlockSpec((nb, D_MODEL), lambda l, j: (0, 0)),
            pl.BlockSpec((None, D_MODEL, tn), lambda l, j: (l, 0, j)),
            pl.BlockSpec((None, 1, tn), lambda l, j: (l, 0, j)),
        ],
        out_specs=pl.BlockSpec((None, nb, tn), lambda l, j: (l, 0, j)),
        compiler_params=_cparams(("arbitrary", "arbitrary")),
        name="adaln_mod",
    )(c_all, w_ada, b_ada.reshape(DEPTH, 1, 6 * D_MODEL))


def _rope_t(xt, c, s):
    n = c.shape[0]
    x1, x2 = xt[:n], xt[n:]
    return jnp.concatenate([x1 * c - x2 * s, x1 * s + x2 * c], axis=0)


def _axial_t(xt, cr, sr, cc, sc):
    d2 = xt.shape[0] // 2
    return jnp.concatenate([_rope_t(xt[:d2], cr, sr), _rope_t(xt[d2:], cc, sc)], axis=0)


def _pre_attn_kernel(x_ref, mod_ref, gmix_ref, tab_ref, wint_ref, gqa_ref, gka_ref, gcq_ref,
                     wuqt_ref, gckv_ref, wukvt_ref, q_ref, k_ref, v_ref):
    x = x_ref[...]
    tm = x.shape[0]
    mod = mod_ref[...]
    h = _rms_rows(x, gmix_ref[...]) * (1.0 + mod[1:2]) + mod[0:1]
    zt = _dot_nt(wint_ref[...], h.astype(BF16))

    tab = tab_ref[...]
    ra = (tab[0:16], tab[16:32], tab[32:48], tab[48:64])
    rb = (tab[64:72], tab[72:80], tab[80:88], tab[88:96])
    zeros64 = jnp.zeros((HEAD_DIM, tm), F32)
    zeros32 = jnp.zeros((QK_PAD - D_NOPE - D_ROPE, tm), F32)

    o_q, o_k, o_v, o_cq, o_ckv, o_kr = IN_OFFS[:6]
    gqa = gqa_ref[...]
    for hh in range(N_HEADS_A):
        qh = _axial_t(_rms_cols(zt[o_q + HEAD_DIM * hh:o_q + HEAD_DIM * (hh + 1)], gqa), *ra) * SCALE_A
        parts = [qh, zeros64] if hh // (N_HEADS_A // N_KV_A) == 0 else [zeros64, qh]
        q_ref[hh] = _to_qk(jnp.concatenate(parts, axis=0))
    gka = gka_ref[...]
    kts = [jnp.concatenate(
        [_axial_t(_rms_cols(zt[o_k + HEAD_DIM * g:o_k + HEAD_DIM * (g + 1)], gka), *ra)
         for g in range(N_KV_A)], axis=0)]
    ones16 = jnp.ones((V_ROWS - D_V, tm), F32)
    for g in range(N_KV_A):
        v_ref[g] = jnp.concatenate(
            [zt[o_v + HEAD_DIM * g:o_v + HEAD_DIM * (g + 1)], ones16], axis=0).astype(BF16)

    cqn = _rms_cols(zt[o_cq:o_ckv], gcq_ref[...]).astype(BF16)
    qbt = _dot(wuqt_ref[...], cqn)
    dq = D_NOPE + D_ROPE
    for hh in range(N_HEADS_B):
        nope = qbt[dq * hh:dq * hh + D_NOPE]
        rp = _axial_t(qbt[dq * hh + D_NOPE:dq * (hh + 1)], *rb)
        q_ref[N_HEADS_A + hh] = _to_qk(jnp.concatenate([nope, rp, zeros32], axis=0) * SCALE_B)
    ckvn = _rms_cols(zt[o_ckv:o_kr], gckv_ref[...]).astype(BF16)
    kvt = _dot(wukvt_ref[...], ckvn)
    kr = _axial_t(zt[o_kr:o_kr + D_ROPE], *rb)
    dkv = D_NOPE + D_V
    for hh in range(N_HEADS_B):
        kts.append(jnp.concatenate([kvt[dkv * hh:dkv * hh + D_NOPE], kr, zeros32], axis=0))
        v_ref[N_KV_A + hh] = jnp.concatenate(
            [kvt[dkv * hh + D_NOPE:dkv * (hh + 1)], ones16], axis=0).astype(BF16)

    k_all = jnp.concatenate(kts, axis=0).T
    for i in range(N_KSLOTS):
        k_ref[i] = _to_qk(k_all[:, QK_PAD * i:QK_PAD * (i + 1)])


def _pre_attn(x, mod_l, g_mix, tab, wint, gqa, gka, gcq, wuqt, gckv, wukvt):
    B, S, _ = x.shape
    tm = TOKEN_TILE
    nt = S // tm
    const = lambda b, i: (0, 0)
    return pl.pallas_call(
        _pre_attn_kernel,
        out_shape=(
            jax.ShapeDtypeStruct((B, N_HEADS, QK_PAD, S), QK_DTYPE),
            jax.ShapeDtypeStruct((B, N_KSLOTS, S, QK_PAD), QK_DTYPE),
            jax.ShapeDtypeStruct((B, N_VSLOTS, nt, V_ROWS, tm), BF16),
        ),
        grid=(B, nt),
        in_specs=[
            pl.BlockSpec((None, tm, D_MODEL), lambda b, i: (b, i, 0)),
            pl.BlockSpec((None, 6, D_MODEL), lambda b, i: (b, 0, 0)),
            pl.BlockSpec((1, D_MODEL), const),
            pl.BlockSpec((96, tm), lambda b, i: (0, i)),
            pl.BlockSpec((IN_WIDTH, D_MODEL), const),
            pl.BlockSpec((HEAD_DIM, 1), const),
            pl.BlockSpec((HEAD_DIM, 1), const),
            pl.BlockSpec((D_CQ, 1), const),
            pl.BlockSpec((N_HEADS_B * (D_NOPE + D_ROPE), D_CQ), const),
            pl.BlockSpec((D_CKV, 1), const),
            pl.BlockSpec((N_HEADS_B * (D_NOPE + D_V), D_CKV), const),
        ],
        out_specs=(
            pl.BlockSpec((None, N_HEADS, QK_PAD, tm), lambda b, i: (b, 0, 0, i)),
            pl.BlockSpec((None, N_KSLOTS, tm, QK_PAD), lambda b, i: (b, 0, i, 0)),
            pl.BlockSpec((None, N_VSLOTS, None, V_ROWS, tm), lambda b, i: (b, 0, i, 0, 0)),
        ),
        compiler_params=_cparams(("parallel", "parallel")),
        name="pre_attn",
    )(x, mod_l, g_mix, tab, wint, gqa, gka, gcq, wuqt, gckv, wukvt)


def _attn_kernel(q_ref, k_ref, v_ref, o_ref, sa_scr, sb_scr, *, n_chunks, tk, trip):
    qt = q_ref[...]
    tq = qt.shape[1]
    bufs = (sa_scr, sb_scr)

    def scores(c, s_ref):
        off = c * tk if isinstance(c, int) else pl.multiple_of(c * tk, tk)
        s = _dot(k_ref[pl.ds(off, tk), :], qt)
        s_ref[...] = s
        return jnp.max(s, axis=0, keepdims=True)

    def update(c, s_ref, cm, m, acc):
        m_new = jnp.maximum(m, cm)
        p = jnp.exp2(s_ref[...] - m_new).astype(BF16)
        return m_new, acc * jnp.exp2(m - m_new) + _dot(v_ref[c], p)

    def steps(c0, count, cm, m, acc, last):
        for u in range(count):
            if not (last and u == count - 1):
                cm_next = scores(c0 + u + 1, bufs[(u + 1) % 2])
            m, acc = update(c0 + u, bufs[u % 2], cm, m, acc)
            cm = cm_next
        return cm, m, acc

    def body(i, carry):
        return steps(i * trip, trip, *carry, last=False)

    m0 = jnp.full((1, tq), NEG_BIG, F32)
    acc0 = jnp.zeros((V_ROWS, tq), F32)
    n_trips = (n_chunks - 2) // trip
    carry = lax.fori_loop(0, n_trips, body, (scores(0, sa_scr), m0, acc0))
    _, _, acc = steps(n_trips * trip, n_chunks - n_trips * trip, *carry, last=True)
    o_ref[...] = (acc[:D_V] * (1.0 / acc[D_V:D_V + 1])).astype(o_ref.dtype)


def _attention(qt, k, vt):
    B, _, _, S = qt.shape
    n_chunks, tk = vt.shape[2], vt.shape[4]
    tq = Q_TILE
    kslot = lambda h: jnp.where(h < N_HEADS_A, 0, h - (N_HEADS_A - 1))
    vslot = lambda h: jnp.where(h < N_HEADS_A, h // (N_HEADS_A // N_KV_A), h - (N_HEADS_A - N_KV_A))
    return pl.pallas_call(
        functools.partial(_attn_kernel, n_chunks=n_chunks, tk=tk,
                          trip=6 if n_chunks >= 8 else 2),
        out_shape=jax.ShapeDtypeStruct((B, N_HEADS, D_V, S), BF16),
        grid=(B, N_HEADS, S // tq),
        in_specs=[
            pl.BlockSpec((None, None, QK_PAD, tq), lambda b, h, i: (b, h, 0, i)),
            pl.BlockSpec((None, None, S, QK_PAD), lambda b, h, i: (b, kslot(h), 0, 0)),
            pl.BlockSpec((None, None, n_chunks, V_ROWS, tk), lambda b, h, i: (b, vslot(h), 0, 0, 0)),
        ],
        out_specs=pl.BlockSpec((None, None, D_V, tq), lambda b, h, i: (b, h, 0, i)),
        scratch_shapes=[pltpu.VMEM((tk, tq), F32), pltpu.VMEM((tk, tq), F32)],
        compiler_params=_cparams(("parallel", "parallel", "arbitrary")),
        name="attn_sweep",
    )(qt, k, vt)


def _post_attn_kernel(o_ref, x_ref, mod_ref, wo_ref, out_ref):
    tm = x_ref.shape[0]
    ot = o_ref[...].reshape(N_HEADS * D_V, tm)
    m = _dot_tn(ot, wo_ref[...])
    out_ref[...] = x_ref[...] + mod_ref[...][2:3] * m


def _post_attn(ot, x, mod_l, wo):
    B, S, _ = x.shape
    tm = TOKEN_TILE
    return pl.pallas_call(
        _post_attn_kernel,
        out_shape=jax.ShapeDtypeStruct(x.shape, F32),
        grid=(B, S // tm),
        in_specs=[
            pl.BlockSpec((None, N_HEADS, D_V, tm), lambda b, i: (b, 0, 0, i)),
            pl.BlockSpec((None, tm, D_MODEL), lambda b, i: (b, i, 0)),
            pl.BlockSpec((None, 6, D_MODEL), lambda b, i: (b, 0, 0)),
            pl.BlockSpec((D_MODEL, D_MODEL), lambda b, i: (0, 0)),
        ],
        out_specs=pl.BlockSpec((None, tm, D_MODEL), lambda b, i: (b, i, 0)),
        compiler_params=_cparams(("parallel", "parallel")),
        name="post_attn",
    )(ot, x, mod_l, wo)


def _f1_kernel(x_ref, mod_ref, gmix_ref, fc_ref, m1_ref, twr_ref, twi_ref, y_ref, z_scr):
    n1 = x_ref.shape[0]
    x = x_ref[...].reshape(n1 * FFT_T, D_MODEL)
    mod = mod_ref[...]
    h = _rms_rows(x, gmix_ref[...]) * (1.0 + mod[1:2]) + mod[0:1]
    hb = h.astype(BF16)
    fc = fc_ref[...]
    for g in range(N_FGROUPS):
        zg = _dot(hb[:, FGROUP * g:FGROUP * (g + 1)], fc)
        z_scr[:, :, FGROUP * g:FGROUP * (g + 1)] = zg[:, :FGROUP].reshape(n1, FFT_T, FGROUP)
        z_scr[:, :, D_MODEL + FGROUP * g:D_MODEL + FGROUP * (g + 1)] = zg[:, FGROUP:].reshape(n1, FFT_T, FGROUP)
    m1 = m1_ref[...]
    for j in range(FFT_T):
        zj = z_scr[:, j, :]
        st = jnp.concatenate([zj[:, :D_MODEL], zj[:, D_MODEL:]], axis=0).astype(BF16)
        y = _dot(m1, st)
        yr, yi = y[:n1], y[n1:]
        tr = jnp.tile(twr_ref[j], (1, D_MODEL // LANES))
        ti = jnp.tile(twi_ref[j], (1, D_MODEL // LANES))
        y_ref[:, j, 0:D_MODEL] = yr * tr - yi * ti
        y_ref[:, j, D_MODEL:2 * D_MODEL] = yr * ti + yi * tr


def _f2_kernel(y_ref, x_ref, mod_ref, cs_ref, wc_ref, o_ref, f_scr):
    cs = cs_ref[...]
    for j in range(FFT_T):
        yk = y_ref[j]
        st = jnp.concatenate([yk[:, :D_MODEL], yk[:, D_MODEL:]], axis=0).astype(BF16)
        f_scr[FFT_N2 * j:FFT_N2 * (j + 1), :] = _dot(cs, st).astype(BF16)
    m = _dot(f_scr[...], wc_ref[...])
    gt1 = mod_ref[...][2:3]
    for j in range(FFT_T):
        o_ref[:, j, :] = x_ref[:, j, :] + gt1 * m[FFT_N2 * j:FFT_N2 * (j + 1)]


def _dft_constants(S):
    n1 = S // FFT_N2
    k = np.arange(FGROUP)
    ang = 2.0 * np.pi * np.outer(k, k) / FGROUP
    fc = np.concatenate([np.cos(ang), -np.sin(ang)], axis=1) / math.sqrt(FGROUP)
    a = np.arange(n1)
    ang1 = 2.0 * np.pi * np.outer(a, a) / n1
    c1, s1 = np.cos(ang1), np.sin(ang1)
    m1 = np.block([[c1, s1], [-s1, c1]]) / math.sqrt(n1)
    b = np.arange(FFT_N2)
    ang2 = 2.0 * np.pi * np.outer(b, b) / FFT_N2
    cs = np.concatenate([np.cos(ang2), np.sin(ang2)], axis=1) / math.sqrt(FFT_N2)
    angt = 2.0 * np.pi * np.outer(b, a) / S
    twr = np.broadcast_to(np.cos(angt)[:, :, None], (FFT_N2, n1, LANES))
    twi = np.broadcast_to(-np.sin(angt)[:, :, None], (FFT_N2, n1, LANES))
    return (jnp.asarray(fc, BF16), jnp.asarray(m1, BF16), jnp.asarray(cs, BF16),
            jnp.asarray(twr, F32), jnp.asarray(twi, F32))


def _fourier_layer(x, mod_l, g_mix, wc):
    B, S, _ = x.shape
    n1 = S // FFT_N2
    fc, m1, cs, twr, twi = _dft_constants(S)
    const2 = lambda b, i: (0, 0)
    y = pl.pallas_call(
        _f1_kernel,
        out_shape=jax.ShapeDtypeStruct((B, n1, FFT_N2, 2 * D_MODEL), F32),
        grid=(B, FFT_N2 // FFT_T),
        in_specs=[
            pl.BlockSpec((None, n1, FFT_T, D_MODEL), lambda b, i: (b, 0, i, 0)),
            pl.BlockSpec((None, 6, D_MODEL), lambda b, i: (b, 0, 0)),
            pl.BlockSpec((1, D_MODEL), const2),
            pl.BlockSpec((FGROUP, 2 * FGROUP), const2),
            pl.BlockSpec((2 * n1, 2 * n1), const2),
            pl.BlockSpec((FFT_T, n1, LANES), lambda b, i: (i, 0, 0)),
            pl.BlockSpec((FFT_T, n1, LANES), lambda b, i: (i, 0, 0)),
        ],
        out_specs=pl.BlockSpec((None, n1, FFT_T, 2 * D_MODEL), lambda b, i: (b, 0, i, 0)),
        scratch_shapes=[pltpu.VMEM((n1, FFT_T, 2 * D_MODEL), F32)],
        compiler_params=_cparams(("parallel", "parallel")),
        name="fourier_stage1",
    )(x.reshape(B, n1, FFT_N2, D_MODEL), mod_l, g_mix, fc, m1, twr, twi)
    out = pl.pallas_call(
        _f2_kernel,
        out_shape=jax.ShapeDtypeStruct((B, FFT_N2, n1, D_MODEL), F32),
        grid=(B, n1 // FFT_T),
        in_specs=[
            pl.BlockSpec((None, FFT_T, FFT_N2, 2 * D_MODEL), lambda b, i: (b, i, 0, 0)),
            pl.BlockSpec((None, FFT_N2, FFT_T, D_MODEL), lambda b, i: (b, 0, i, 0)),
            pl.BlockSpec((None, 6, D_MODEL), lambda b, i: (b, 0, 0)),
            pl.BlockSpec((FFT_N2, 2 * FFT_N2), const2),
            pl.BlockSpec((D_MODEL, D_MODEL), const2),
        ],
        out_specs=pl.BlockSpec((None, FFT_N2, FFT_T, D_MODEL), lambda b, i: (b, 0, i, 0)),
        scratch_shapes=[pltpu.VMEM((FFT_T * FFT_N2, D_MODEL), BF16)],
        compiler_params=_cparams(("parallel", "parallel")),
        name="fourier_stage2",
    )(y, x.reshape(B, FFT_N2, n1, D_MODEL), mod_l, cs, wc)
    return out.reshape(B, S, D_MODEL)


def _route_t(s, sb):
    rows = [sb[e:e + 1] for e in range(N_EXPERTS)]
    gscore = []
    for g in range(N_EGROUPS):
        a, b, c, d = rows[4 * g:4 * g + 4]
        gscore.append(jnp.maximum(jnp.maximum(jnp.maximum(a + b, a + c), jnp.maximum(a + d, b + c)),
                                  jnp.maximum(b + d, c + d)))
    best = gscore[0]
    gsel = jnp.zeros_like(best)
    for g in range(1, N_EGROUPS):
        upd = gscore[g] > best
        gsel = jnp.where(upd, float(g), gsel)
        best = jnp.where(upd, gscore[g], best)
    picked = []
    for e in range(N_EXPERTS):
        g = e // EXPERTS_PER_GROUP
        rank = jnp.zeros_like(best)
        for j in range(4 * g, 4 * g + 4):
            if j == e:
                continue
            ahead = (rows[j] >= rows[e]) if j < e else (rows[j] > rows[e])
            rank = rank + jnp.where(ahead, 1.0, 0.0)
        sel = jnp.where(rank < 1.5, 1.0, 0.0) * jnp.where(gsel == float(g), 1.0, 0.0)
        picked.append(sel * s[e:e + 1])
    total = picked[0]
    for e in range(1, N_EXPERTS):
        total = total + picked[e]
    inv = 1.0 / total
    return [p * inv for p in picked], gsel


def _moe_kernel(x_ref, mod_ref, gffn_ref, wr_ref, bias_ref, wg_ref, wu_ref, wd_ref, gfin_ref,
                out_ref, srt_scr, y_scr, *, final):
    tm = x_ref.shape[0]
    x = x_ref[...]
    mod = mod_ref[...]
    t = _rms_rows(x, gffn_ref[...]) * (1.0 + mod[4:5]) + mod[3:4]
    t_hi = t.astype(BF16)
    t_lo = (t - t_hi.astype(F32)).astype(BF16)
    wr = wr_ref[...]
    a = _dot_nt(wr, t_hi)
    b = _dot_nt(wr[0:N_EXPERTS], t_lo)
    logits = a[0:N_EXPERTS] + a[N_EXPERTS:2 * N_EXPERTS] + b
    s = _sigmoid(logits)
    gates, gsel = _route_t(s, s + bias_ref[...])

    masks = [jnp.where(gsel == float(g), 1.0, 0.0) for g in range(N_EGROUPS)]
    m8 = jnp.concatenate(masks + [jnp.zeros((SUBLANES - N_EGROUPS, tm), F32)], axis=0)
    earlier = jnp.where(lax.broadcasted_iota(jnp.int32, (tm, tm), 0)
                        < lax.broadcasted_iota(jnp.int32, (tm, tm), 1), 1.0, 0.0).astype(BF16)
    ranks = _dot(m8.astype(BF16), earlier)
    counts = jnp.sum(m8, axis=1, keepdims=True)
    n_chunks = jnp.floor((counts + (MOE_CHUNK - 1)) * (1.0 / MOE_CHUNK))
    starts = []
    start = jnp.zeros((1, 1), F32)
    for g in range(N_EGROUPS):
        starts.append(start)
        start = start + n_chunks[g:g + 1] * MOE_CHUNK
    pos = masks[0] * (starts[0] + ranks[0:1])
    for g in range(1, N_EGROUPS):
        pos = pos + masks[g] * (starts[g] + ranks[g:g + 1])
    perm = jnp.where(lax.broadcasted_iota(jnp.int32, (MOE_ROWS, tm), 0).astype(F32) == pos,
                     1.0, 0.0).astype(BF16)

    g4 = []
    for j in range(EXPERTS_PER_GROUP):
        r = masks[0] * gates[j]
        for g in range(1, N_EGROUPS):
            r = r + masks[g] * gates[EXPERTS_PER_GROUP * g + j]
        g4.append(r)
    g4t = jnp.concatenate(g4 + [jnp.zeros((LANES - EXPERTS_PER_GROUP, tm), F32)], axis=0).T
    src = jnp.concatenate([t_hi, g4t.astype(BF16)], axis=1)
    srt_scr[...] = _dot(perm, src).astype(BF16)
    y_scr[...] = jnp.zeros_like(y_scr)

    for g in range(N_EGROUPS):
        base = starts[g][0, 0].astype(jnp.int32)

        def chunk(j, carry, g=g, base=base):
            off = pl.multiple_of(base + j * MOE_CHUNK, MOE_CHUNK)
            xs = srt_scr[pl.ds(off, MOE_CHUNK), 0:D_MODEL]
            gs = srt_scr[pl.ds(off, MOE_CHUNK), D_MODEL:D_MODEL + LANES].astype(F32)
            u1 = _dot(xs, wg_ref[g])
            u2 = _dot(xs, wu_ref[g])
            he = u1 * _sigmoid(u1) * u2
            he = jnp.concatenate(
                [he[:, D_EXPERT * e:D_EXPERT * (e + 1)] * gs[:, e:e + 1] for e in range(EXPERTS_PER_GROUP)],
                axis=1)
            y_scr[pl.ds(off, MOE_CHUNK), :] = _dot(he.astype(BF16), wd_ref[g]).astype(BF16)
            return carry

        lax.fori_loop(0, n_chunks[g, 0].astype(jnp.int32), chunk, 0)

    y = x + mod[5:6] * _dot_tn(perm, y_scr[...])
    if final:
        y = _rms_rows(y, gfin_ref[...])
    out_ref[...] = y


def _moe(x, mod_l, g_ffn, wr, bias, wg, wu, wd, g_final, final):
    B, S, _ = x.shape
    tm = MOE_TILE
    const = lambda b, i: (0, 0)
    wspec = pl.BlockSpec((N_EGROUPS, D_MODEL, D_MODEL), lambda b, i: (0, 0, 0),
                         pipeline_mode=pl.Buffered(1))
    return pl.pallas_call(
        functools.partial(_moe_kernel, final=final),
        out_shape=jax.ShapeDtypeStruct(x.shape, F32),
        grid=(B, S // tm),
        in_specs=[
            pl.BlockSpec((None, tm, D_MODEL), lambda b, i: (b, i, 0)),
            pl.BlockSpec((None, 6, D_MODEL), lambda b, i: (b, 0, 0)),
            pl.BlockSpec((1, D_MODEL), const),
            pl.BlockSpec((2 * N_EXPERTS, D_MODEL), const),
            pl.BlockSpec((N_EXPERTS, 1), const),
            wspec, wspec, wspec,
            pl.BlockSpec((1, D_MODEL), const),
        ],
        out_specs=pl.BlockSpec((None, tm, D_MODEL), lambda b, i: (b, i, 0)),
        scratch_shapes=[
            pltpu.VMEM((MOE_ROWS, D_MODEL + LANES), BF16),
            pltpu.VMEM((MOE_ROWS, D_MODEL), BF16),
        ],
        compiler_params=_cparams(("parallel", "parallel")),
        name="moe",
    )(x, mod_l, g_ffn, wr, bias, wg, wu, wd, g_final)


def _rope_tables(S):
    pos = jnp.arange(S, dtype=jnp.int32)
    rows = (pos // GRID_W).astype(F32)[None, :]
    cols = (pos % GRID_W).astype(F32)[None, :]
    tabs = []
    for half in (HEAD_DIM // 2, D_ROPE // 2):
        inv = (ROPE_THETA ** (-jnp.arange(0, half, 2, dtype=F32) / half))[:, None]
        for p in (rows, cols):
            ang = inv * p
            tabs += [jnp.cos(ang), jnp.sin(ang)]
    return jnp.concatenate(tabs, axis=0)


def _group_experts(w, transpose_cols):
    e, a, b = w.shape
    w = w.reshape(N_EGROUPS, EXPERTS_PER_GROUP, a, b)
    if transpose_cols:
        return w.transpose(0, 2, 1, 3).reshape(N_EGROUPS, a, EXPERTS_PER_GROUP * b).astype(BF16)
    return w.reshape(N_EGROUPS, EXPERTS_PER_GROUP * a, b).astype(BF16)


def _trunk(x, mod, p):
    B, S, _ = x.shape
    tab = _rope_tables(S)
    for l in range(DEPTH):
        mod_l = mod[l]
        g_mix = p["g_mix"][l][None, :]
        if l % 2 == 0:
            i = l // 2
            qt, k, vt = _pre_attn(x, mod_l, g_mix, tab, p["wint"][i], p["g_qa"][i][:, None],
                                  p["g_ka"][i][:, None], p["g_cq"][i][:, None], p["wuqt"][i],
                                  p["g_ckv"][i][:, None], p["wukvt"][i])
            ot = _attention(qt, k, vt)
            x = _post_attn(ot, x, mod_l, p["wo"][i])
        else:
            x = _fourier_layer(x, mod_l, g_mix, p["wc"][l // 2])
        x = _moe(x, mod_l, p["g_ffn"][l][None, :], p["wr"], p["bias"], p["wg"][l], p["wu"][l],
                 p["wd"][l], p["g_final"], final=(l == DEPTH - 1))
    return x


def kernel(x_prompt, x_sample, c_prompt, c_sample, w_ada, b_ada, g_mix, g_ffn, w_in, g_qa, g_ka, g_cq,
           w_uq, g_ckv, w_ukv, w_o_attn, w_fourier, w_router, router_bias, w_gate, w_up, w_down, g_final):
    nbp = x_prompt.shape[0]
    mod = _modulation(jnp.concatenate([c_prompt, c_sample], axis=0), w_ada, b_ada)
    mod = mod.reshape(DEPTH, -1, 6, D_MODEL)
    wrt = w_router.T
    wr_hi = wrt.astype(BF16)
    wr_lo = (wrt - wr_hi.astype(F32)).astype(BF16)
    p = {
        "g_mix": g_mix, "g_ffn": g_ffn, "g_qa": g_qa, "g_ka": g_ka, "g_cq": g_cq, "g_ckv": g_ckv,
        "wint": jnp.swapaxes(w_in, 1, 2).astype(BF16),
        "wuqt": jnp.swapaxes(w_uq, 1, 2).astype(BF16),
        "wukvt": jnp.swapaxes(w_ukv, 1, 2).astype(BF16),
        "wo": w_o_attn.astype(BF16),
        "wc": w_fourier.astype(BF16),
        "wr": jnp.concatenate([wr_hi, wr_lo], axis=0),
        "bias": router_bias[:, None],
        "wg": [_group_experts(w_gate[l], True) for l in range(DEPTH)],
        "wu": [_group_experts(w_up[l], True) for l in range(DEPTH)],
        "wd": [_group_experts(w_down[l], False) for l in range(DEPTH)],
        "g_final": g_final[None, :],
    }
    y_prompt = _trunk(x_prompt, mod[:, :nbp], p)
    y_sample = _trunk(x_sample, mod[:, nbp:], p)
    return (y_prompt, y_sample)
```

```python
import functools
import math

import numpy as np
import jax
import jax.numpy as jnp
from jax import lax
from jax.experimental import pallas as pl
from jax.experimental.pallas import tpu as pltpu

F32 = jnp.float32
BF16 = jnp.bfloat16
QK_DTYPE = jnp.float8_e4m3fn
QK_MAX = float(jnp.finfo(QK_DTYPE).max)
LANES = 128
SUBLANES = 8

D_MODEL = 1024
DEPTH = 4
GRID_W = 64
HEAD_DIM = 64
N_HEADS_A = 8
N_KV_A = 2
N_HEADS_B = 8
D_NOPE = 64
D_ROPE = 32
D_V = 64
D_CQ = 384
D_CKV = 256
N_FGROUPS = 4
FGROUP = D_MODEL // N_FGROUPS
ROPE_THETA = 10000.0
N_EXPERTS = 16
N_EGROUPS = 4
EXPERTS_PER_GROUP = 4
D_EXPERT = 256
EPS = 1e-6
IN_SIZES = (N_HEADS_A * HEAD_DIM, N_KV_A * HEAD_DIM, N_KV_A * HEAD_DIM, D_CQ, D_CKV, D_ROPE)
IN_WIDTH = sum(IN_SIZES)
IN_OFFS = tuple(int(v) for v in np.cumsum((0,) + IN_SIZES))

N_HEADS = N_HEADS_A + N_HEADS_B
N_KSLOTS = 1 + N_HEADS_B
N_VSLOTS = N_KV_A + N_HEADS_B
QK_PAD = LANES
LOG2E = 1.4426950408889634
SCALE_A = HEAD_DIM ** -0.5 * LOG2E
SCALE_B = (D_NOPE + D_ROPE) ** -0.5 * LOG2E
NEG_BIG = -1e30

FFT_N2 = 128
FFT_T = SUBLANES

TOKEN_TILE = 512
Q_TILE = 512
ATTN_TRIP = 6
V_ROWS = D_V + 16
MOE_TILE = 512
MOE_CHUNK = 128
MOE_ROWS = MOE_TILE + (N_EGROUPS - 1) * MOE_CHUNK
VMEM_LIMIT = 56 * 1024 * 1024


def _cparams(sem):
    return pltpu.CompilerParams(dimension_semantics=sem, vmem_limit_bytes=VMEM_LIMIT)


def _to_qk(x):
    return jnp.clip(x, -QK_MAX, QK_MAX).astype(QK_DTYPE)


def _sigmoid(x):
    return 1.0 / (1.0 + jnp.exp(-x))


def _rms_rows(x, g):
    ms = jnp.mean(x * x, axis=-1, keepdims=True)
    return x * lax.rsqrt(ms + EPS) * g


def _rms_cols(xt, gcol):
    ms = jnp.mean(xt * xt, axis=0, keepdims=True)
    return xt * lax.rsqrt(ms + EPS) * gcol


def _dot(a, b):
    return jnp.dot(a, b, preferred_element_type=F32)


def _dot_nt(a, b):
    return lax.dot_general(a, b, (((1,), (1,)), ((), ())), preferred_element_type=F32)


def _dot_tn(a, b):
    return lax.dot_general(a, b, (((0,), (0,)), ((), ())), preferred_element_type=F32)


def _mod_kernel(c_ref, w_ref, b_ref, o_ref):
    c = c_ref[...]
    ca = c * _sigmoid(c)
    o_ref[...] = _dot(ca, w_ref[...]) + b_ref[...]


def _modulation(c_all, w_ada, b_ada):
    nb = c_all.shape[0]
    tn = 1536
    return pl.pallas_call(
        _mod_kernel,
        out_shape=jax.ShapeDtypeStruct((DEPTH, nb, 6 * D_MODEL), F32),
        grid=(DEPTH, 6 * D_MODEL // tn),
        in_specs=[
            pl.BlockSpec((nb, D_MODEL), lambda l, j: (0, 0)),
            pl.BlockSpec((None, D_MODEL, tn), lambda l, j: (l, 0, j)),
            pl.BlockSpec((None, 1, tn), lambda l, j: (l, 0, j)),
        ],
        out_specs=pl.BlockSpec((None, nb, tn), lambda l, j: (l, 0, j)),
        compiler_params=_cparams(("arbitrary", "arbitrary")),
        name="adaln_mod",
    )(c_all, w_ada, b_ada.reshape(DEPTH, 1, 6 * D_MODEL))


def _rope_t(xt, c, s):
    n = c.shape[0]
    x1, x2 = xt[:n], xt[n:]
    return jnp.concatenate([x1 * c - x2 * s, x1 * s + x2 * c], axis=0)


def _axial_t(xt, cr, sr, cc, sc):
    d2 = xt.shape[0] // 2
    return jnp.concatenate([_rope_t(xt[:d2], cr, sr), _rope_t(xt[d2:], cc, sc)], axis=0)


def _pre_attn_kernel(x_ref, mod_ref, gmix_ref, tab_ref, wint_ref, gqa_ref, gka_ref, gcq_ref,
                     wuqt_ref, gckv_ref, wukvt_ref, q_ref, k_ref, v_ref):
    x = x_ref[...]
    tm = x.shape[0]
    mod = mod_ref[...]
    h = _rms_rows(x, gmix_ref[...]) * (1.0 + mod[1:2]) + mod[0:1]
    zt = _dot_nt(wint_ref[...], h.astype(BF16))

    tab = tab_ref[...]
    ra = (tab[0:16], tab[16:32], tab[32:48], tab[48:64])
    rb = (tab[64:72], tab[72:80], tab[80:88], tab[88:96])
    zeros64 = jnp.zeros((HEAD_DIM, tm), F32)
    zeros32 = jnp.zeros((QK_PAD - D_NOPE - D_ROPE, tm), F32)

    o_q, o_k, o_v, o_cq, o_ckv, o_kr = IN_OFFS[:6]
    gqa = gqa_ref[...]
    for hh in range(N_HEADS_A):
        qh = _axial_t(_rms_cols(zt[o_q + HEAD_DIM * hh:o_q + HEAD_DIM * (hh + 1)], gqa), *ra) * SCALE_A
        parts = [qh, zeros64] if hh // (N_HEADS_A // N_KV_A) == 0 else [zeros64, qh]
        q_ref[hh] = _to_qk(jnp.concatenate(parts, axis=0))
    gka = gka_ref[...]
    kts = [jnp.concatenate(
        [_axial_t(_rms_cols(zt[o_k + HEAD_DIM * g:o_k + HEAD_DIM * (g + 1)], gka), *ra)
         for g in range(N_KV_A)], axis=0)]
    ones16 = jnp.ones((V_ROWS - D_V, tm), F32)
    for g in range(N_KV_A):
        v_ref[g] = jnp.concatenate(
            [zt[o_v + HEAD_DIM * g:o_v + HEAD_DIM * (g + 1)], ones16], axis=0).astype(BF16)

    cqn = _rms_cols(zt[o_cq:o_ckv], gcq_ref[...]).astype(BF16)
    qbt = _dot(wuqt_ref[...], cqn)
    dq = D_NOPE + D_ROPE
    for hh in range(N_HEADS_B):
        nope = qbt[dq * hh:dq * hh + D_NOPE]
        rp = _axial_t(qbt[dq * hh + D_NOPE:dq * (hh + 1)], *rb)
        q_ref[N_HEADS_A + hh] = _to_qk(jnp.concatenate([nope, rp, zeros32], axis=0) * SCALE_B)
    ckvn = _rms_cols(zt[o_ckv:o_kr], gckv_ref[...]).astype(BF16)
    kvt = _dot(wukvt_ref[...], ckvn)
    kr = _axial_t(zt[o_kr:o_kr + D_ROPE], *rb)
    dkv = D_NOPE + D_V
    for hh in range(N_HEADS_B):
        kts.append(jnp.concatenate([kvt[dkv * hh:dkv * hh + D_NOPE], kr, zeros32], axis=0))
        v_ref[N_KV_A + hh] = jnp.concatenate(
            [kvt[dkv * hh + D_NOPE:dkv * (hh + 1)], ones16], axis=0).astype(BF16)

    k_all = jnp.concatenate(kts, axis=0).T
    for i in range(N_KSLOTS):
        k_ref[i] = _to_qk(k_all[:, QK_PAD * i:QK_PAD * (i + 1)])


def _pre_attn(x, mod_l, g_mix, tab, wint, gqa, gka, gcq, wuqt, gckv, wukvt):
    B, S, _ = x.shape
    tm = TOKEN_TILE
    nt = S // tm
    const = lambda b, i: (0, 0)
    return pl.pallas_call(
        _pre_attn_kernel,
        out_shape=(
            jax.ShapeDtypeStruct((B, N_HEADS, QK_PAD, S), QK_DTYPE),
            jax.ShapeDtypeStruct((B, N_KSLOTS, S, QK_PAD), QK_DTYPE),
            jax.ShapeDtypeStruct((B, N_VSLOTS, nt, V_ROWS, tm), BF16),
        ),
        grid=(B, nt),
        in_specs=[
            pl.BlockSpec((None, tm, D_MODEL), lambda b, i: (b, i, 0)),
            pl.BlockSpec((None, 6, D_MODEL), lambda b, i: (b, 0, 0)),
            pl.BlockSpec((1, D_MODEL), const),
            pl.BlockSpec((96, tm), lambda b, i: (0, i)),
            pl.BlockSpec((IN_WIDTH, D_MODEL), const),
            pl.BlockSpec((HEAD_DIM, 1), const),
            pl.BlockSpec((HEAD_DIM, 1), const),
            pl.BlockSpec((D_CQ, 1), const),
            pl.BlockSpec((N_HEADS_B * (D_NOPE + D_ROPE), D_CQ), const),
            pl.BlockSpec((D_CKV, 1), const),
            pl.BlockSpec((N_HEADS_B * (D_NOPE + D_V), D_CKV), const),
        ],
        out_specs=(
            pl.BlockSpec((None, N_HEADS, QK_PAD, tm), lambda b, i: (b, 0, 0, i)),
            pl.BlockSpec((None, N_KSLOTS, tm, QK_PAD), lambda b, i: (b, 0, i, 0)),
            pl.BlockSpec((None, N_VSLOTS, None, V_ROWS, tm), lambda b, i: (b, 0, i, 0, 0)),
        ),
        compiler_params=_cparams(("parallel", "parallel")),
        name="pre_attn",
    )(x, mod_l, g_mix, tab, wint, gqa, gka, gcq, wuqt, gckv, wukvt)


def _attn_kernel(q_ref, k_ref, v_ref, o_ref, sa_scr, sb_scr, *, n_chunks, tk, trip):
    qt = q_ref[...]
    tq = qt.shape[1]
    bufs = (sa_scr, sb_scr)

    def scores(c, s_ref):
        off = c * tk if isinstance(c, int) else pl.multiple_of(c * tk, tk)
        s = _dot(k_ref[pl.ds(off, tk), :], qt)
        s_ref[...] = s
        return jnp.max(s, axis=0, keepdims=True)

    def update(c, s_ref, cm, m, acc):
        m_new = jnp.maximum(m, cm)
        p = jnp.exp2(s_ref[...] - m_new).astype(BF16)
        return m_new, acc * jnp.exp2(m - m_new) + _dot(v_ref[c], p)

    def steps(c0, count, cm, m, acc, last):
        for u in range(count):
            if not (last and u == count - 1):
                cm_next = scores(c0 + u + 1, bufs[(u + 1) % 2])
            m, acc = update(c0 + u, bufs[u % 2], cm, m, acc)
            cm = cm_next
        return cm, m, acc

    def body(i, carry):
        return steps(i * trip, trip, *carry, last=False)

    m0 = jnp.full((1, tq), NEG_BIG, F32)
    acc0 = jnp.zeros((V_ROWS, tq), F32)
    n_trips = (n_chunks - 2) // trip
    carry = lax.fori_loop(0, n_trips, body, (scores(0, sa_scr), m0, acc0))
    _, _, acc = steps(n_trips * trip, n_chunks - n_trips * trip, *carry, last=True)
    o_ref[...] = (acc[:D_V] * (1.0 / acc[D_V:D_V + 1])).astype(o_ref.dtype)


def _attention(qt, k, vt):
    B, _, _, S = qt.shape
    n_chunks, tk = vt.shape[2], vt.shape[4]
    tq = Q_TILE
    kslot = lambda h: jnp.where(h < N_HEADS_A, 0, h - (N_HEADS_A - 1))
    vslot = lambda h: jnp.where(h < N_HEADS_A, h // (N_HEADS_A // N_KV_A), h - (N_HEADS_A - N_KV_A))
    return pl.pallas_call(
        functools.partial(_attn_kernel, n_chunks=n_chunks, tk=tk,
                          trip=ATTN_TRIP),
        out_shape=jax.ShapeDtypeStruct((B, N_HEADS, D_V, S), BF16),
        grid=(B, N_HEADS, S // tq),
        in_specs=[
            pl.BlockSpec((None, None, QK_PAD, tq), lambda b, h, i: (b, h, 0, i)),
            pl.BlockSpec((None, None, S, QK_PAD), lambda b, h, i: (b, kslot(h), 0, 0)),
            pl.BlockSpec((None, None, n_chunks, V_ROWS, tk), lambda b, h, i: (b, vslot(h), 0, 0, 0)),
        ],
        out_specs=pl.BlockSpec((None, None, D_V, tq), lambda b, h, i: (b, h, 0, i)),
        scratch_shapes=[pltpu.VMEM((tk, tq), F32), pltpu.VMEM((tk, tq), F32)],
        compiler_params=_cparams(("parallel", "parallel", "arbitrary")),
        name="attn_sweep",
    )(qt, k, vt)


def _f1_kernel(x_ref, mod_ref, gmix_ref, fc_ref, m1_ref, twr_ref, twi_ref, y_ref, z_scr):
    n1 = x_ref.shape[0]
    x = x_ref[...].reshape(n1 * FFT_T, D_MODEL)
    mod = mod_ref[...]
    h = _rms_rows(x, gmix_ref[...]) * (1.0 + mod[1:2]) + mod[0:1]
    hb = h.astype(BF16)
    fc = fc_ref[...]
    for g in range(N_FGROUPS):
        zg = _dot(hb[:, FGROUP * g:FGROUP * (g + 1)], fc)
        z_scr[:, :, FGROUP * g:FGROUP * (g + 1)] = zg[:, :FGROUP].reshape(n1, FFT_T, FGROUP)
        z_scr[:, :, D_MODEL + FGROUP * g:D_MODEL + FGROUP * (g + 1)] = zg[:, FGROUP:].reshape(n1, FFT_T, FGROUP)
    m1 = m1_ref[...]
    for j in range(FFT_T):
        zj = z_scr[:, j, :]
        st = jnp.concatenate([zj[:, :D_MODEL], zj[:, D_MODEL:]], axis=0).astype(BF16)
        y = _dot(m1, st)
        yr, yi = y[:n1], y[n1:]
        tr = jnp.tile(twr_ref[j], (1, D_MODEL // LANES))
        ti = jnp.tile(twi_ref[j], (1, D_MODEL // LANES))
        y_ref[:, j, 0:D_MODEL] = yr * tr - yi * ti
        y_ref[:, j, D_MODEL:2 * D_MODEL] = yr * ti + yi * tr


def _f2_kernel(y_ref, x_ref, mod_ref, cs_ref, wc_ref, o_ref, f_scr):
    cs = cs_ref[...]
    for j in range(FFT_T):
        yk = y_ref[j]
        st = jnp.concatenate([yk[:, :D_MODEL], yk[:, D_MODEL:]], axis=0).astype(BF16)
        f_scr[FFT_N2 * j:FFT_N2 * (j + 1), :] = _dot(cs, st).astype(BF16)
    m = _dot(f_scr[...], wc_ref[...])
    gt1 = mod_ref[...][2:3]
    for j in range(FFT_T):
        o_ref[:, j, :] = x_ref[:, j, :] + gt1 * m[FFT_N2 * j:FFT_N2 * (j + 1)]


def _dft_constants(S):
    n1 = S // FFT_N2
    k = np.arange(FGROUP)
    ang = 2.0 * np.pi * np.outer(k, k) / FGROUP
    fc = np.concatenate([np.cos(ang), -np.sin(ang)], axis=1) / math.sqrt(FGROUP)
    a = np.arange(n1)
    ang1 = 2.0 * np.pi * np.outer(a, a) / n1
    c1, s1 = np.cos(ang1), np.sin(ang1)
    m1 = np.block([[c1, s1], [-s1, c1]]) / math.sqrt(n1)
    b = np.arange(FFT_N2)
    ang2 = 2.0 * np.pi * np.outer(b, b) / FFT_N2
    cs = np.concatenate([np.cos(ang2), np.sin(ang2)], axis=1) / math.sqrt(FFT_N2)
    angt = 2.0 * np.pi * np.outer(b, a) / S
    twr = np.broadcast_to(np.cos(angt)[:, :, None], (FFT_N2, n1, LANES))
    twi = np.broadcast_to(-np.sin(angt)[:, :, None], (FFT_N2, n1, LANES))
    return (jnp.asarray(fc, BF16), jnp.asarray(m1, BF16), jnp.asarray(cs, BF16),
            jnp.asarray(twr, F32), jnp.asarray(twi, F32))


def _fourier_layer(x, mod_l, g_mix, wc):
    B, S, _ = x.shape
    n1 = S // FFT_N2
    fc, m1, cs, twr, twi = _dft_constants(S)
    const2 = lambda b, i: (0, 0)
    y = pl.pallas_call(
        _f1_kernel,
        out_shape=jax.ShapeDtypeStruct((B, n1, FFT_N2, 2 * D_MODEL), F32),
        grid=(B, FFT_N2 // FFT_T),
        in_specs=[
            pl.BlockSpec((None, n1, FFT_T, D_MODEL), lambda b, i: (b, 0, i, 0)),
            pl.BlockSpec((None, 6, D_MODEL), lambda b, i: (b, 0, 0)),
            pl.BlockSpec((1, D_MODEL), const2),
            pl.BlockSpec((FGROUP, 2 * FGROUP), const2),
            pl.BlockSpec((2 * n1, 2 * n1), const2),
            pl.BlockSpec((FFT_T, n1, LANES), lambda b, i: (i, 0, 0)),
            pl.BlockSpec((FFT_T, n1, LANES), lambda b, i: (i, 0, 0)),
        ],
        out_specs=pl.BlockSpec((None, n1, FFT_T, 2 * D_MODEL), lambda b, i: (b, 0, i, 0)),
        scratch_shapes=[pltpu.VMEM((n1, FFT_T, 2 * D_MODEL), F32)],
        compiler_params=_cparams(("parallel", "parallel")),
        name="fourier_stage1",
    )(x.reshape(B, n1, FFT_N2, D_MODEL), mod_l, g_mix, fc, m1, twr, twi)
    out = pl.pallas_call(
        _f2_kernel,
        out_shape=jax.ShapeDtypeStruct((B, FFT_N2, n1, D_MODEL), F32),
        grid=(B, n1 // FFT_T),
        in_specs=[
            pl.BlockSpec((None, FFT_T, FFT_N2, 2 * D_MODEL), lambda b, i: (b, i, 0, 0)),
            pl.BlockSpec((None, FFT_N2, FFT_T, D_MODEL), lambda b, i: (b, 0, i, 0)),
            pl.BlockSpec((None, 6, D_MODEL), lambda b, i: (b, 0, 0)),
            pl.BlockSpec((FFT_N2, 2 * FFT_N2), const2),
            pl.BlockSpec((D_MODEL, D_MODEL), const2),
        ],
        out_specs=pl.BlockSpec((None, FFT_N2, FFT_T, D_MODEL), lambda b, i: (b, 0, i, 0)),
        scratch_shapes=[pltpu.VMEM((FFT_T * FFT_N2, D_MODEL), BF16)],
        compiler_params=_cparams(("parallel", "parallel")),
        name="fourier_stage2",
    )(y, x.reshape(B, FFT_N2, n1, D_MODEL), mod_l, cs, wc)
    return out.reshape(B, S, D_MODEL)


def _route_t(s, sb):
    rows = [sb[e:e + 1] for e in range(N_EXPERTS)]
    gscore = []
    for g in range(N_EGROUPS):
        a, b, c, d = rows[4 * g:4 * g + 4]
        gscore.append(jnp.maximum(jnp.maximum(jnp.maximum(a + b, a + c), jnp.maximum(a + d, b + c)),
                                  jnp.maximum(b + d, c + d)))
    best = gscore[0]
    gsel = jnp.zeros_like(best)
    for g in range(1, N_EGROUPS):
        upd = gscore[g] > best
        gsel = jnp.where(upd, float(g), gsel)
        best = jnp.where(upd, gscore[g], best)
    picked = []
    for e in range(N_EXPERTS):
        g = e // EXPERTS_PER_GROUP
        rank = jnp.zeros_like(best)
        for j in range(4 * g, 4 * g + 4):
            if j == e:
                continue
            ahead = (rows[j] >= rows[e]) if j < e else (rows[j] > rows[e])
            rank = rank + jnp.where(ahead, 1.0, 0.0)
        sel = jnp.where(rank < 1.5, 1.0, 0.0) * jnp.where(gsel == float(g), 1.0, 0.0)
        picked.append(sel * s[e:e + 1])
    total = picked[0]
    for e in range(1, N_EXPERTS):
        total = total + picked[e]
    inv = 1.0 / total
    return [p * inv for p in picked], gsel


def _moe_kernel(*refs, final, after_attention):
    if after_attention:
        o_ref, wo_ref, *refs = refs
    (x_ref, mod_ref, gffn_ref, wr_ref, bias_ref, wg_ref, wu_ref, wd_ref, gfin_ref,
     out_ref, srt_scr, y_scr) = refs
    tm = x_ref.shape[0]
    x = x_ref[...]
    mod = mod_ref[...]
    if after_attention:
        ot = o_ref[...].reshape(N_HEADS * D_V, tm)
        x = x + mod[2:3] * _dot_tn(ot, wo_ref[...])
    t = _rms_rows(x, gffn_ref[...]) * (1.0 + mod[4:5]) + mod[3:4]
    t_hi = t.astype(BF16)
    t_lo = (t - t_hi.astype(F32)).astype(BF16)
    wr = wr_ref[...]
    a = _dot_nt(wr, t_hi)
    b = _dot_nt(wr[0:N_EXPERTS], t_lo)
    logits = a[0:N_EXPERTS] + a[N_EXPERTS:2 * N_EXPERTS] + b
    s = _sigmoid(logits)
    gates, gsel = _route_t(s, s + bias_ref[...])

    masks = [jnp.where(gsel == float(g), 1.0, 0.0) for g in range(N_EGROUPS)]
    m8 = jnp.concatenate(masks + [jnp.zeros((SUBLANES - N_EGROUPS, tm), F32)], axis=0)
    earlier = jnp.where(lax.broadcasted_iota(jnp.int32, (tm, tm), 0)
                        < lax.broadcasted_iota(jnp.int32, (tm, tm), 1), 1.0, 0.0).astype(BF16)
    ranks = _dot(m8.astype(BF16), earlier)
    counts = jnp.sum(m8, axis=1, keepdims=True)
    n_chunks = jnp.floor((counts + (MOE_CHUNK - 1)) * (1.0 / MOE_CHUNK))
    starts = []
    start = jnp.zeros((1, 1), F32)
    for g in range(N_EGROUPS):
        starts.append(start)
        start = start + n_chunks[g:g + 1] * MOE_CHUNK
    pos = masks[0] * (starts[0] + ranks[0:1])
    for g in range(1, N_EGROUPS):
        pos = pos + masks[g] * (starts[g] + ranks[g:g + 1])
    perm = jnp.where(lax.broadcasted_iota(jnp.int32, (MOE_ROWS, tm), 0) == pos.astype(jnp.int32),
                     1.0, 0.0).astype(BF16)

    g4 = []
    for j in range(EXPERTS_PER_GROUP):
        r = masks[0] * gates[j]
        for g in range(1, N_EGROUPS):
            r = r + masks[g] * gates[EXPERTS_PER_GROUP * g + j]
        g4.append(r)
    g4t = jnp.concatenate(g4 + [jnp.zeros((LANES - EXPERTS_PER_GROUP, tm), F32)], axis=0).T
    src = jnp.concatenate([t_hi, g4t.astype(BF16)], axis=1)
    srt_scr[...] = _dot(perm, src).astype(BF16)
    y_scr[...] = jnp.zeros_like(y_scr)

    for g in range(N_EGROUPS):
        base = starts[g][0, 0].astype(jnp.int32)

        def chunk(j, carry, g=g, base=base):
            off = pl.multiple_of(base + j * MOE_CHUNK, MOE_CHUNK)
            xs = srt_scr[pl.ds(off, MOE_CHUNK), 0:D_MODEL]
            gs = srt_scr[pl.ds(off, MOE_CHUNK), D_MODEL:D_MODEL + LANES].astype(F32)
            u1 = _dot(xs, wg_ref[g])
            u2 = _dot(xs, wu_ref[g])
            he = u1 * _sigmoid(u1) * u2
            he = jnp.concatenate(
                [he[:, D_EXPERT * e:D_EXPERT * (e + 1)] * gs[:, e:e + 1] for e in range(EXPERTS_PER_GROUP)],
                axis=1)
            y_scr[pl.ds(off, MOE_CHUNK), :] = _dot(he.astype(BF16), wd_ref[g]).astype(BF16)
            return carry

        lax.fori_loop(0, n_chunks[g, 0].astype(jnp.int32), chunk, 0)

    y = x + mod[5:6] * _dot_tn(perm, y_scr[...])
    if final:
        y = _rms_rows(y, gfin_ref[...])
    out_ref[...] = y


def _moe(x, mod_l, g_ffn, wr, bias, wg, wu, wd, g_final, final, attn=None):
    B, S, _ = x.shape
    tm = MOE_TILE
    const = lambda b, i: (0, 0)
    wspec = pl.BlockSpec((N_EGROUPS, D_MODEL, D_MODEL), lambda b, i: (0, 0, 0),
                         pipeline_mode=pl.Buffered(1))
    attn_specs = [] if attn is None else [
        pl.BlockSpec((None, N_HEADS, D_V, tm), lambda b, i: (b, 0, 0, i)),
        pl.BlockSpec((D_MODEL, D_MODEL), const, pipeline_mode=pl.Buffered(1)),
    ]
    return pl.pallas_call(
        functools.partial(_moe_kernel, final=final, after_attention=attn is not None),
        out_shape=jax.ShapeDtypeStruct(x.shape, F32),
        grid=(B, S // tm),
        in_specs=attn_specs + [
            pl.BlockSpec((None, tm, D_MODEL), lambda b, i: (b, i, 0)),
            pl.BlockSpec((None, 6, D_MODEL), lambda b, i: (b, 0, 0)),
            pl.BlockSpec((1, D_MODEL), const),
            pl.BlockSpec((2 * N_EXPERTS, D_MODEL), const),
            pl.BlockSpec((N_EXPERTS, 1), const),
            wspec, wspec, wspec,
            pl.BlockSpec((1, D_MODEL), const),
        ],
        out_specs=pl.BlockSpec((None, tm, D_MODEL), lambda b, i: (b, i, 0)),
        scratch_shapes=[
            pltpu.VMEM((MOE_ROWS, D_MODEL + LANES), BF16),
            pltpu.VMEM((MOE_ROWS, D_MODEL), BF16),
        ],
        compiler_params=_cparams(("parallel", "parallel")),
        name="moe",
    )(*(attn or ()), x, mod_l, g_ffn, wr, bias, wg, wu, wd, g_final)


def _rope_tables(S):
    pos = jnp.arange(S, dtype=jnp.int32)
    rows = (pos // GRID_W).astype(F32)[None, :]
    cols = (pos % GRID_W).astype(F32)[None, :]
    tabs = []
    for half in (HEAD_DIM // 2, D_ROPE // 2):
        inv = (ROPE_THETA ** (-jnp.arange(0, half, 2, dtype=F32) / half))[:, None]
        for p in (rows, cols):
            ang = inv * p
            tabs += [jnp.cos(ang), jnp.sin(ang)]
    return jnp.concatenate(tabs, axis=0)


def _group_experts(w, transpose_cols):
    e, a, b = w.shape
    w = w.reshape(N_EGROUPS, EXPERTS_PER_GROUP, a, b)
    if transpose_cols:
        return w.transpose(0, 2, 1, 3).reshape(N_EGROUPS, a, EXPERTS_PER_GROUP * b).astype(BF16)
    return w.reshape(N_EGROUPS, EXPERTS_PER_GROUP * a, b).astype(BF16)


def _trunk(x, mod, p):
    B, S, _ = x.shape
    tab = _rope_tables(S)
    for l in range(DEPTH):
        mod_l = mod[l]
        g_mix = p["g_mix"][l][None, :]
        if l % 2 == 0:
            i = l // 2
            qt, k, vt = _pre_attn(x, mod_l, g_mix, tab, p["wint"][i], p["g_qa"][i][:, None],
                                  p["g_ka"][i][:, None], p["g_cq"][i][:, None], p["wuqt"][i],
                                  p["g_ckv"][i][:, None], p["wukvt"][i])
            attn = (_attention(qt, k, vt), p["wo"][i])
        else:
            attn = None
            x = _fourier_layer(x, mod_l, g_mix, p["wc"][l // 2])
        x = _moe(x, mod_l, p["g_ffn"][l][None, :], p["wr"], p["bias"], p["wg"][l], p["wu"][l],
                 p["wd"][l], p["g_final"], final=(l == DEPTH - 1), attn=attn)
    return x


def kernel(x_prompt, x_sample, c_prompt, c_sample, w_ada, b_ada, g_mix, g_ffn, w_in, g_qa, g_ka, g_cq,
           w_uq, g_ckv, w_ukv, w_o_attn, w_fourier, w_router, router_bias, w_gate, w_up, w_down, g_final):
    nbp = x_prompt.shape[0]
    mod = _modulation(jnp.concatenate([c_prompt, c_sample], axis=0), w_ada, b_ada)
    mod = mod.reshape(DEPTH, -1, 6, D_MODEL)
    wrt = w_router.T
    wr_hi = wrt.astype(BF16)
    wr_lo = (wrt - wr_hi.astype(F32)).astype(BF16)
    p = {
        "g_mix": g_mix, "g_ffn": g_ffn, "g_qa": g_qa, "g_ka": g_ka, "g_cq": g_cq, "g_ckv": g_ckv,
        "wint": jnp.swapaxes(w_in, 1, 2).astype(BF16),
        "wuqt": jnp.swapaxes(w_uq, 1, 2).astype(BF16),
        "wukvt": jnp.swapaxes(w_ukv, 1, 2).astype(BF16),
        "wo": w_o_attn.astype(BF16),
        "wc": w_fourier.astype(BF16),
        "wr": jnp.concatenate([wr_hi, wr_lo], axis=0),
        "bias": router_bias[:, None],
        "wg": [_group_experts(w_gate[l], True) for l in range(DEPTH)],
        "wu": [_group_experts(w_up[l], True) for l in range(DEPTH)],
        "wd": [_group_experts(w_down[l], False) for l in range(DEPTH)],
        "g_final": g_final[None, :],
    }
    y_prompt = _trunk(x_prompt, mod[:, :nbp], p)
    y_sample = _trunk(x_sample, mod[:, nbp:], p)
    return (y_prompt, y_sample)
```

```python
import functools
import math

import numpy as np
import jax
import jax.numpy as jnp
from jax import lax
from jax.experimental import pallas as pl
from jax.experimental.pallas import tpu as pltpu

F32 = jnp.float32
BF16 = jnp.bfloat16
QK_DTYPE = jnp.float8_e4m3fn
QK_MAX = float(jnp.finfo(QK_DTYPE).max)
LANES = 128
SUBLANES = 8

D_MODEL = 1024
DEPTH = 4
GRID_W = 64
HEAD_DIM = 64
N_HEADS_A = 8
N_KV_A = 2
N_HEADS_B = 8
D_NOPE = 64
D_ROPE = 32
D_V = 64
D_CQ = 384
D_CKV = 256
N_FGROUPS = 4
FGROUP = D_MODEL // N_FGROUPS
ROPE_THETA = 10000.0
N_EXPERTS = 16
N_EGROUPS = 4
EXPERTS_PER_GROUP = 4
D_EXPERT = 256
EPS = 1e-6
IN_SIZES = (N_HEADS_A * HEAD_DIM, N_KV_A * HEAD_DIM, N_KV_A * HEAD_DIM, D_CQ, D_CKV, D_ROPE)
IN_WIDTH = sum(IN_SIZES)
IN_OFFS = tuple(int(v) for v in np.cumsum((0,) + IN_SIZES))

N_HEADS = N_HEADS_A + N_HEADS_B
N_KSLOTS = 1 + N_HEADS_B
N_VSLOTS = N_KV_A + N_HEADS_B
QK_PAD = LANES
LOG2E = 1.4426950408889634
SCALE_A = HEAD_DIM ** -0.5 * LOG2E
SCALE_B = (D_NOPE + D_ROPE) ** -0.5 * LOG2E
NEG_BIG = -1e30

FFT_N2 = 128
FFT_T = SUBLANES

TOKEN_TILE = 512
Q_TILE = 512
ATTN_TRIP = 6
V_ROWS = D_V + 16
MOE_TILE = 512
MOE_CHUNK = 144
_MOE_MAX_CHUNKS = (MOE_TILE + N_EGROUPS * (MOE_CHUNK - 1)) // MOE_CHUNK
MOE_ROWS = -(-_MOE_MAX_CHUNKS * MOE_CHUNK // LANES) * LANES
VMEM_LIMIT = 56 * 1024 * 1024


def _cparams(sem):
    return pltpu.CompilerParams(dimension_semantics=sem, vmem_limit_bytes=VMEM_LIMIT)


def _to_qk(x):
    return jnp.clip(x, -QK_MAX, QK_MAX).astype(QK_DTYPE)


def _sigmoid(x):
    return 1.0 / (1.0 + jnp.exp(-x))


def _rms_rows(x, g):
    ms = jnp.mean(x * x, axis=-1, keepdims=True)
    return x * lax.rsqrt(ms + EPS) * g


def _rms_cols(xt, gcol):
    ms = jnp.mean(xt * xt, axis=0, keepdims=True)
    return xt * lax.rsqrt(ms + EPS) * gcol


def _dot(a, b):
    return jnp.dot(a, b, preferred_element_type=F32)


def _dot_nt(a, b):
    return lax.dot_general(a, b, (((1,), (1,)), ((), ())), preferred_element_type=F32)


def _dot_tn(a, b):
    return lax.dot_general(a, b, (((0,), (0,)), ((), ())), preferred_element_type=F32)


def _mod_kernel(c_ref, w_ref, b_ref, o_ref):
    c = c_ref[...]
    ca = c * _sigmoid(c)
    o_ref[...] = _dot(ca, w_ref[...]) + b_ref[...]


def _modulation(c_all, w_ada, b_ada):
    nb = c_all.shape[0]
    tn = 1536
    return pl.pallas_call(
        _mod_kernel,
        out_shape=jax.ShapeDtypeStruct((DEPTH, nb, 6 * D_MODEL), F32),
        grid=(DEPTH, 6 * D_MODEL // tn),
        in_specs=[
            pl.BlockSpec((nb, D_MODEL), lambda l, j: (0, 0)),
            pl.BlockSpec((None, D_MODEL, tn), lambda l, j: (l, 0, j)),
            pl.BlockSpec((None, 1, tn), lambda l, j: (l, 0, j)),
        ],
        out_specs=pl.BlockSpec((None, nb, tn), lambda l, j: (l, 0, j)),
        compiler_params=_cparams(("arbitrary", "arbitrary")),
        name="adaln_mod",
    )(c_all, w_ada, b_ada.reshape(DEPTH, 1, 6 * D_MODEL))


def _rope_t(xt, c, s):
    n = c.shape[0]
    x1, x2 = xt[:n], xt[n:]
    return jnp.concatenate([x1 * c - x2 * s, x1 * s + x2 * c], axis=0)


def _axial_t(xt, cr, sr, cc, sc):
    d2 = xt.shape[0] // 2
    return jnp.concatenate([_rope_t(xt[:d2], cr, sr), _rope_t(xt[d2:], cc, sc)], axis=0)


def _pre_attn_kernel(x_ref, mod_ref, gmix_ref, tab_ref, wint_ref, gqa_ref, gka_ref, gcq_ref,
                     wuqt_ref, gckv_ref, wukvt_ref, q_ref, k_ref, v_ref):
    x = x_ref[...]
    tm = x.shape[0]
    mod = mod_ref[...]
    h = _rms_rows(x, gmix_ref[...]) * (1.0 + mod[1:2]) + mod[0:1]
    zt = _dot_nt(wint_ref[...], h.astype(BF16))

    tab = tab_ref[...]
    ra = (tab[0:16], tab[16:32], tab[32:48], tab[48:64])
    rb = (tab[64:72], tab[72:80], tab[80:88], tab[88:96])
    zeros64 = jnp.zeros((HEAD_DIM, tm), F32)
    zeros32 = jnp.zeros((QK_PAD - D_NOPE - D_ROPE, tm), F32)

    o_q, o_k, o_v, o_cq, o_ckv, o_kr = IN_OFFS[:6]
    gqa = gqa_ref[...]
    for hh in range(N_HEADS_A):
        qh = _axial_t(_rms_cols(zt[o_q + HEAD_DIM * hh:o_q + HEAD_DIM * (hh + 1)], gqa), *ra) * SCALE_A
        parts = [qh, zeros64] if hh // (N_HEADS_A // N_KV_A) == 0 else [zeros64, qh]
        q_ref[hh] = _to_qk(jnp.concatenate(parts, axis=0))
    gka = gka_ref[...]
    kts = [jnp.concatenate(
        [_axial_t(_rms_cols(zt[o_k + HEAD_DIM * g:o_k + HEAD_DIM * (g + 1)], gka), *ra)
         for g in range(N_KV_A)], axis=0)]
    ones16 = jnp.ones((V_ROWS - D_V, tm), F32)
    for g in range(N_KV_A):
        v_ref[g] = jnp.concatenate(
            [zt[o_v + HEAD_DIM * g:o_v + HEAD_DIM * (g + 1)], ones16], axis=0).astype(BF16)

    cqn = _rms_cols(zt[o_cq:o_ckv], gcq_ref[...]).astype(BF16)
    qbt = _dot(wuqt_ref[...], cqn)
    dq = D_NOPE + D_ROPE
    for hh in range(N_HEADS_B):
        nope = qbt[dq * hh:dq * hh + D_NOPE]
        rp = _axial_t(qbt[dq * hh + D_NOPE:dq * (hh + 1)], *rb)
        q_ref[N_HEADS_A + hh] = _to_qk(jnp.concatenate([nope, rp, zeros32], axis=0) * SCALE_B)
    ckvn = _rms_cols(zt[o_ckv:o_kr], gckv_ref[...]).astype(BF16)
    kvt = _dot(wukvt_ref[...], ckvn)
    kr = _axial_t(zt[o_kr:o_kr + D_ROPE], *rb)
    dkv = D_NOPE + D_V
    for hh in range(N_HEADS_B):
        kts.append(jnp.concatenate([kvt[dkv * hh:dkv * hh + D_NOPE], kr, zeros32], axis=0))
        v_ref[N_KV_A + hh] = jnp.concatenate(
            [kvt[dkv * hh + D_NOPE:dkv * (hh + 1)], ones16], axis=0).astype(BF16)

    k_all = jnp.concatenate(kts, axis=0).T
    for i in range(N_KSLOTS):
        k_ref[i] = _to_qk(k_all[:, QK_PAD * i:QK_PAD * (i + 1)])


def _pre_attn(x, mod_l, g_mix, tab, wint, gqa, gka, gcq, wuqt, gckv, wukvt):
    B, S, _ = x.shape
    tm = TOKEN_TILE
    nt = S // tm
    const = lambda b, i: (0, 0)
    return pl.pallas_call(
        _pre_attn_kernel,
        out_shape=(
            jax.ShapeDtypeStruct((B, N_HEADS, QK_PAD, S), QK_DTYPE),
            jax.ShapeDtypeStruct((B, N_KSLOTS, S, QK_PAD), QK_DTYPE),
            jax.ShapeDtypeStruct((B, N_VSLOTS, nt, V_ROWS, tm), BF16),
        ),
        grid=(B, nt),
        in_specs=[
            pl.BlockSpec((None, tm, D_MODEL), lambda b, i: (b, i, 0)),
            pl.BlockSpec((None, 6, D_MODEL), lambda b, i: (b, 0, 0)),
            pl.BlockSpec((1, D_MODEL), const),
            pl.BlockSpec((96, tm), lambda b, i: (0, i)),
            pl.BlockSpec((IN_WIDTH, D_MODEL), const),
            pl.BlockSpec((HEAD_DIM, 1), const),
            pl.BlockSpec((HEAD_DIM, 1), const),
            pl.BlockSpec((D_CQ, 1), const),
            pl.BlockSpec((N_HEADS_B * (D_NOPE + D_ROPE), D_CQ), const),
            pl.BlockSpec((D_CKV, 1), const),
            pl.BlockSpec((N_HEADS_B * (D_NOPE + D_V), D_CKV), const),
        ],
        out_specs=(
            pl.BlockSpec((None, N_HEADS, QK_PAD, tm), lambda b, i: (b, 0, 0, i)),
            pl.BlockSpec((None, N_KSLOTS, tm, QK_PAD), lambda b, i: (b, 0, i, 0)),
            pl.BlockSpec((None, N_VSLOTS, None, V_ROWS, tm), lambda b, i: (b, 0, i, 0, 0)),
        ),
        compiler_params=_cparams(("parallel", "parallel")),
        name="pre_attn",
    )(x, mod_l, g_mix, tab, wint, gqa, gka, gcq, wuqt, gckv, wukvt)


def _attn_kernel(q_ref, k_ref, v_ref, o_ref, sa_scr, sb_scr, *, n_chunks, tk, trip):
    qt = q_ref[...]
    tq = qt.shape[1]
    bufs = (sa_scr, sb_scr)

    def scores(c, s_ref):
        off = c * tk if isinstance(c, int) else pl.multiple_of(c * tk, tk)
        s = _dot(k_ref[pl.ds(off, tk), :], qt)
        s_ref[...] = s
        return jnp.max(s, axis=0, keepdims=True)

    def update(c, s_ref, cm, m, acc):
        m_new = jnp.maximum(m, cm)
        p = jnp.exp2(s_ref[...] - m_new).astype(BF16)
        return m_new, acc * jnp.exp2(m - m_new) + _dot(v_ref[c], p)

    def steps(c0, count, cm, m, acc, last):
        for u in range(count):
            if not (last and u == count - 1):
                cm_next = scores(c0 + u + 1, bufs[(u + 1) % 2])
            m, acc = update(c0 + u, bufs[u % 2], cm, m, acc)
            cm = cm_next
        return cm, m, acc

    def body(i, carry):
        return steps(i * trip, trip, *carry, last=False)

    m0 = jnp.full((1, tq), NEG_BIG, F32)
    acc0 = jnp.zeros((V_ROWS, tq), F32)
    n_trips = (n_chunks - 2) // trip
    carry = lax.fori_loop(0, n_trips, body, (scores(0, sa_scr), m0, acc0))
    _, _, acc = steps(n_trips * trip, n_chunks - n_trips * trip, *carry, last=True)
    o_ref[...] = (acc[:D_V] * (1.0 / acc[D_V:D_V + 1])).astype(o_ref.dtype)


def _attention(qt, k, vt):
    B, _, _, S = qt.shape
    n_chunks, tk = vt.shape[2], vt.shape[4]
    tq = Q_TILE
    kslot = lambda h: jnp.where(h < N_HEADS_A, 0, h - (N_HEADS_A - 1))
    vslot = lambda h: jnp.where(h < N_HEADS_A, h // (N_HEADS_A // N_KV_A), h - (N_HEADS_A - N_KV_A))
    return pl.pallas_call(
        functools.partial(_attn_kernel, n_chunks=n_chunks, tk=tk,
                          trip=ATTN_TRIP),
        out_shape=jax.ShapeDtypeStruct((B, N_HEADS, D_V, S), BF16),
        grid=(B, N_HEADS, S // tq),
        in_specs=[
            pl.BlockSpec((None, None, QK_PAD, tq), lambda b, h, i: (b, h, 0, i)),
            pl.BlockSpec((None, None, S, QK_PAD), lambda b, h, i: (b, kslot(h), 0, 0)),
            pl.BlockSpec((None, None, n_chunks, V_ROWS, tk), lambda b, h, i: (b, vslot(h), 0, 0, 0)),
        ],
        out_specs=pl.BlockSpec((None, None, D_V, tq), lambda b, h, i: (b, h, 0, i)),
        scratch_shapes=[pltpu.VMEM((tk, tq), F32), pltpu.VMEM((tk, tq), F32)],
        compiler_params=_cparams(("parallel", "parallel", "arbitrary")),
        name="attn_sweep",
    )(qt, k, vt)


def _f1_kernel(x_ref, mod_ref, gmix_ref, fc_ref, m1_ref, twr_ref, twi_ref, y_ref, z_scr):
    n1 = x_ref.shape[0]
    x = x_ref[...].reshape(n1 * FFT_T, D_MODEL)
    mod = mod_ref[...]
    h = _rms_rows(x, gmix_ref[...]) * (1.0 + mod[1:2]) + mod[0:1]
    hb = h.astype(BF16)
    fc = fc_ref[...]
    for g in range(N_FGROUPS):
        zg = _dot(hb[:, FGROUP * g:FGROUP * (g + 1)], fc)
        z_scr[:, :, FGROUP * g:FGROUP * (g + 1)] = zg[:, :FGROUP].reshape(n1, FFT_T, FGROUP)
        z_scr[:, :, D_MODEL + FGROUP * g:D_MODEL + FGROUP * (g + 1)] = zg[:, FGROUP:].reshape(n1, FFT_T, FGROUP)
    m1 = m1_ref[...]
    for j in range(FFT_T):
        zj = z_scr[:, j, :]
        st = jnp.concatenate([zj[:, :D_MODEL], zj[:, D_MODEL:]], axis=0).astype(BF16)
        y = _dot(m1, st)
        yr, yi = y[:n1], y[n1:]
        tr = jnp.tile(twr_ref[j], (1, D_MODEL // LANES))
        ti = jnp.tile(twi_ref[j], (1, D_MODEL // LANES))
        y_ref[:, j, 0:D_MODEL] = yr * tr - yi * ti
        y_ref[:, j, D_MODEL:2 * D_MODEL] = yr * ti + yi * tr


def _f2_kernel(y_ref, x_ref, mod_ref, cs_ref, wc_ref, o_ref, f_scr):
    cs = cs_ref[...]
    for j in range(FFT_T):
        yk = y_ref[j]
        st = jnp.concatenate([yk[:, :D_MODEL], yk[:, D_MODEL:]], axis=0).astype(BF16)
        f_scr[FFT_N2 * j:FFT_N2 * (j + 1), :] = _dot(cs, st).astype(BF16)
    m = _dot(f_scr[...], wc_ref[...])
    gt1 = mod_ref[...][2:3]
    for j in range(FFT_T):
        o_ref[:, j, :] = x_ref[:, j, :] + gt1 * m[FFT_N2 * j:FFT_N2 * (j + 1)]


def _dft_constants(S):
    n1 = S // FFT_N2
    k = np.arange(FGROUP)
    ang = 2.0 * np.pi * np.outer(k, k) / FGROUP
    fc = np.concatenate([np.cos(ang), -np.sin(ang)], axis=1) / math.sqrt(FGROUP)
    a = np.arange(n1)
    ang1 = 2.0 * np.pi * np.outer(a, a) / n1
    c1, s1 = np.cos(ang1), np.sin(ang1)
    m1 = np.block([[c1, s1], [-s1, c1]]) / math.sqrt(n1)
    b = np.arange(FFT_N2)
    ang2 = 2.0 * np.pi * np.outer(b, b) / FFT_N2
    cs = np.concatenate([np.cos(ang2), np.sin(ang2)], axis=1) / math.sqrt(FFT_N2)
    angt = 2.0 * np.pi * np.outer(b, a) / S
    twr = np.broadcast_to(np.cos(angt)[:, :, None], (FFT_N2, n1, LANES))
    twi = np.broadcast_to(-np.sin(angt)[:, :, None], (FFT_N2, n1, LANES))
    return (jnp.asarray(fc, BF16), jnp.asarray(m1, BF16), jnp.asarray(cs, BF16),
            jnp.asarray(twr, F32), jnp.asarray(twi, F32))


def _fourier_layer(x, mod_l, g_mix, wc):
    B, S, _ = x.shape
    n1 = S // FFT_N2
    fc, m1, cs, twr, twi = _dft_constants(S)
    const2 = lambda b, i: (0, 0)
    y = pl.pallas_call(
        _f1_kernel,
        out_shape=jax.ShapeDtypeStruct((B, n1, FFT_N2, 2 * D_MODEL), F32),
        grid=(B, FFT_N2 // FFT_T),
        in_specs=[
            pl.BlockSpec((None, n1, FFT_T, D_MODEL), lambda b, i: (b, 0, i, 0)),
            pl.BlockSpec((None, 6, D_MODEL), lambda b, i: (b, 0, 0)),
            pl.BlockSpec((1, D_MODEL), const2),
            pl.BlockSpec((FGROUP, 2 * FGROUP), const2),
            pl.BlockSpec((2 * n1, 2 * n1), const2),
            pl.BlockSpec((FFT_T, n1, LANES), lambda b, i: (i, 0, 0)),
            pl.BlockSpec((FFT_T, n1, LANES), lambda b, i: (i, 0, 0)),
        ],
        out_specs=pl.BlockSpec((None, n1, FFT_T, 2 * D_MODEL), lambda b, i: (b, 0, i, 0)),
        scratch_shapes=[pltpu.VMEM((n1, FFT_T, 2 * D_MODEL), F32)],
        compiler_params=_cparams(("parallel", "parallel")),
        name="fourier_stage1",
    )(x.reshape(B, n1, FFT_N2, D_MODEL), mod_l, g_mix, fc, m1, twr, twi)
    out = pl.pallas_call(
        _f2_kernel,
        out_shape=jax.ShapeDtypeStruct((B, FFT_N2, n1, D_MODEL), F32),
        grid=(B, n1 // FFT_T),
        in_specs=[
            pl.BlockSpec((None, FFT_T, FFT_N2, 2 * D_MODEL), lambda b, i: (b, i, 0, 0)),
            pl.BlockSpec((None, FFT_N2, FFT_T, D_MODEL), lambda b, i: (b, 0, i, 0)),
            pl.BlockSpec((None, 6, D_MODEL), lambda b, i: (b, 0, 0)),
            pl.BlockSpec((FFT_N2, 2 * FFT_N2), const2),
            pl.BlockSpec((D_MODEL, D_MODEL), const2),
        ],
        out_specs=pl.BlockSpec((None, FFT_N2, FFT_T, D_MODEL), lambda b, i: (b, 0, i, 0)),
        scratch_shapes=[pltpu.VMEM((FFT_T * FFT_N2, D_MODEL), BF16)],
        compiler_params=_cparams(("parallel", "parallel")),
        name="fourier_stage2",
    )(y, x.reshape(B, FFT_N2, n1, D_MODEL), mod_l, cs, wc)
    return out.reshape(B, S, D_MODEL)


def _route_t(s, sb):
    rows = [sb[e:e + 1] for e in range(N_EXPERTS)]
    gscore = []
    for g in range(N_EGROUPS):
        a, b, c, d = rows[4 * g:4 * g + 4]
        gscore.append(jnp.maximum(jnp.maximum(jnp.maximum(a + b, a + c), jnp.maximum(a + d, b + c)),
                                  jnp.maximum(b + d, c + d)))
    best = gscore[0]
    gsel = jnp.zeros_like(best)
    for g in range(1, N_EGROUPS):
        upd = gscore[g] > best
        gsel = jnp.where(upd, float(g), gsel)
        best = jnp.where(upd, gscore[g], best)
    picked = []
    for e in range(N_EXPERTS):
        g = e // EXPERTS_PER_GROUP
        rank = jnp.zeros_like(best)
        for j in range(4 * g, 4 * g + 4):
            if j == e:
                continue
            ahead = (rows[j] >= rows[e]) if j < e else (rows[j] > rows[e])
            rank = rank + jnp.where(ahead, 1.0, 0.0)
        sel = jnp.where(rank < 1.5, 1.0, 0.0) * jnp.where(gsel == float(g), 1.0, 0.0)
        picked.append(sel * s[e:e + 1])
    total = picked[0]
    for e in range(1, N_EXPERTS):
        total = total + picked[e]
    inv = 1.0 / total
    return [p * inv for p in picked], gsel


def _moe_kernel(*refs, final, after_attention):
    if after_attention:
        o_ref, wo_ref, *refs = refs
    (x_ref, mod_ref, gffn_ref, wr_ref, bias_ref, wg_ref, wu_ref, wd_ref, gfin_ref,
     out_ref, srt_scr, y_scr) = refs
    tm = x_ref.shape[0]
    x = x_ref[...]
    mod = mod_ref[...]
    if after_attention:
        ot = o_ref[...].reshape(N_HEADS * D_V, tm)
        x = x + mod[2:3] * _dot_tn(ot, wo_ref[...])
    t = _rms_rows(x, gffn_ref[...]) * (1.0 + mod[4:5]) + mod[3:4]
    t_hi = t.astype(BF16)
    t_lo = (t - t_hi.astype(F32)).astype(BF16)
    wr = wr_ref[...]
    a = _dot_nt(wr, t_hi)
    b = _dot_nt(wr[0:N_EXPERTS], t_lo)
    logits = a[0:N_EXPERTS] + a[N_EXPERTS:2 * N_EXPERTS] + b
    s = _sigmoid(logits)
    gates, gsel = _route_t(s, s + bias_ref[...])

    masks = [jnp.where(gsel == float(g), 1.0, 0.0) for g in range(N_EGROUPS)]
    m8 = jnp.concatenate(masks + [jnp.zeros((SUBLANES - N_EGROUPS, tm), F32)], axis=0)
    earlier = jnp.where(lax.broadcasted_iota(jnp.int32, (tm, tm), 0)
                        < lax.broadcasted_iota(jnp.int32, (tm, tm), 1), 1.0, 0.0).astype(BF16)
    ranks = _dot(m8.astype(BF16), earlier)
    counts = jnp.sum(m8, axis=1, keepdims=True)
    n_chunks = jnp.floor((counts + (MOE_CHUNK - 0.5)) * (1.0 / MOE_CHUNK))
    starts = []
    start = jnp.zeros((1, 1), F32)
    for g in range(N_EGROUPS):
        starts.append(start)
        start = start + n_chunks[g:g + 1] * MOE_CHUNK
    pos = masks[0] * (starts[0] + ranks[0:1])
    for g in range(1, N_EGROUPS):
        pos = pos + masks[g] * (starts[g] + ranks[g:g + 1])
    perm = jnp.where(lax.broadcasted_iota(jnp.int32, (MOE_ROWS, tm), 0) == pos.astype(jnp.int32),
                     1.0, 0.0).astype(BF16)

    g4 = []
    for j in range(EXPERTS_PER_GROUP):
        r = masks[0] * gates[j]
        for g in range(1, N_EGROUPS):
            r = r + masks[g] * gates[EXPERTS_PER_GROUP * g + j]
        g4.append(r)
    g4t = jnp.concatenate(g4 + [jnp.zeros((LANES - EXPERTS_PER_GROUP, tm), F32)], axis=0).T
    src = jnp.concatenate([t_hi, g4t.astype(BF16)], axis=1)
    srt_scr[...] = _dot(perm, src).astype(BF16)
    y_scr[...] = jnp.zeros_like(y_scr)

    for g in range(N_EGROUPS):
        base = starts[g][0, 0].astype(jnp.int32)

        def chunk(j, carry, g=g, base=base):
            off = pl.multiple_of(base + j * MOE_CHUNK, MOE_CHUNK)
            xs = srt_scr[pl.ds(off, MOE_CHUNK), 0:D_MODEL]
            gs = srt_scr[pl.ds(off, MOE_CHUNK), D_MODEL:D_MODEL + LANES].astype(F32)
            u1 = _dot(xs, wg_ref[g])
            u2 = _dot(xs, wu_ref[g])
            he = u1 * _sigmoid(u1) * u2
            he = jnp.concatenate(
                [he[:, D_EXPERT * e:D_EXPERT * (e + 1)] * gs[:, e:e + 1] for e in range(EXPERTS_PER_GROUP)],
                axis=1)
            y_scr[pl.ds(off, MOE_CHUNK), :] = _dot(he.astype(BF16), wd_ref[g]).astype(BF16)
            return carry

        lax.fori_loop(0, n_chunks[g, 0].astype(jnp.int32), chunk, 0)

    y = x + mod[5:6] * _dot_tn(perm, y_scr[...])
    if final:
        y = _rms_rows(y, gfin_ref[...])
    out_ref[...] = y


def _moe(x, mod_l, g_ffn, wr, bias, wg, wu, wd, g_final, final, attn=None):
    B, S, _ = x.shape
    tm = MOE_TILE
    const = lambda b, i: (0, 0)
    wspec = pl.BlockSpec((N_EGROUPS, D_MODEL, D_MODEL), lambda b, i: (0, 0, 0),
                         pipeline_mode=pl.Buffered(1))
    attn_specs = [] if attn is None else [
        pl.BlockSpec((None, N_HEADS, D_V, tm), lambda b, i: (b, 0, 0, i)),
        pl.BlockSpec((D_MODEL, D_MODEL), const, pipeline_mode=pl.Buffered(1)),
    ]
    return pl.pallas_call(
        functools.partial(_moe_kernel, final=final, after_attention=attn is not None),
        out_shape=jax.ShapeDtypeStruct(x.shape, F32),
        grid=(B, S // tm),
        in_specs=attn_specs + [
            pl.BlockSpec((None, tm, D_MODEL), lambda b, i: (b, i, 0)),
            pl.BlockSpec((None, 6, D_MODEL), lambda b, i: (b, 0, 0)),
            pl.BlockSpec((1, D_MODEL), const),
            pl.BlockSpec((2 * N_EXPERTS, D_MODEL), const),
            pl.BlockSpec((N_EXPERTS, 1), const),
            wspec, wspec, wspec,
            pl.BlockSpec((1, D_MODEL), const),
        ],
        out_specs=pl.BlockSpec((None, tm, D_MODEL), lambda b, i: (b, i, 0)),
        scratch_shapes=[
            pltpu.VMEM((MOE_ROWS, D_MODEL + LANES), BF16),
            pltpu.VMEM((MOE_ROWS, D_MODEL), BF16),
        ],
        compiler_params=_cparams(("parallel", "parallel")),
        name="moe",
    )(*(attn or ()), x, mod_l, g_ffn, wr, bias, wg, wu, wd, g_final)


def _rope_tables(S):
    pos = jnp.arange(S, dtype=jnp.int32)
    rows = (pos // GRID_W).astype(F32)[None, :]
    cols = (pos % GRID_W).astype(F32)[None, :]
    tabs = []
    for half in (HEAD_DIM // 2, D_ROPE // 2):
        inv = (ROPE_THETA ** (-jnp.arange(0, half, 2, dtype=F32) / half))[:, None]
        for p in (rows, cols):
            ang = inv * p
            tabs += [jnp.cos(ang), jnp.sin(ang)]
    return jnp.concatenate(tabs, axis=0)


def _group_experts(w, transpose_cols):
    e, a, b = w.shape
    w = w.reshape(N_EGROUPS, EXPERTS_PER_GROUP, a, b)
    if transpose_cols:
        return w.transpose(0, 2, 1, 3).reshape(N_EGROUPS, a, EXPERTS_PER_GROUP * b).astype(BF16)
    return w.reshape(N_EGROUPS, EXPERTS_PER_GROUP * a, b).astype(BF16)


def _trunk(x, mod, p):
    B, S, _ = x.shape
    tab = _rope_tables(S)
    for l in range(DEPTH):
        mod_l = mod[l]
        g_mix = p["g_mix"][l][None, :]
        if l % 2 == 0:
            i = l // 2
            qt, k, vt = _pre_attn(x, mod_l, g_mix, tab, p["wint"][i], p["g_qa"][i][:, None],
                                  p["g_ka"][i][:, None], p["g_cq"][i][:, None], p["wuqt"][i],
                                  p["g_ckv"][i][:, None], p["wukvt"][i])
            attn = (_attention(qt, k, vt), p["wo"][i])
        else:
            attn = None
            x = _fourier_layer(x, mod_l, g_mix, p["wc"][l // 2])
        x = _moe(x, mod_l, p["g_ffn"][l][None, :], p["wr"], p["bias"], p["wg"][l], p["wu"][l],
                 p["wd"][l], p["g_final"], final=(l == DEPTH - 1), attn=attn)
    return x


def kernel(x_prompt, x_sample, c_prompt, c_sample, w_ada, b_ada, g_mix, g_ffn, w_in, g_qa, g_ka, g_cq,
           w_uq, g_ckv, w_ukv, w_o_attn, w_fourier, w_router, router_bias, w_gate, w_up, w_down, g_final):
    nbp = x_prompt.shape[0]
    mod = _modulation(jnp.concatenate([c_prompt, c_sample], axis=0), w_ada, b_ada)
    mod = mod.reshape(DEPTH, -1, 6, D_MODEL)
    wrt = w_router.T
    wr_hi = wrt.astype(BF16)
    wr_lo = (wrt - wr_hi.astype(F32)).astype(BF16)
    p = {
        "g_mix": g_mix, "g_ffn": g_ffn, "g_qa": g_qa, "g_ka": g_ka, "g_cq": g_cq, "g_ckv": g_ckv,
        "wint": jnp.swapaxes(w_in, 1, 2).astype(BF16),
        "wuqt": jnp.swapaxes(w_uq, 1, 2).astype(BF16),
        "wukvt": jnp.swapaxes(w_ukv, 1, 2).astype(BF16),
        "wo": w_o_attn.astype(BF16),
        "wc": w_fourier.astype(BF16),
        "wr": jnp.concatenate([wr_hi, wr_lo], axis=0),
        "bias": router_bias[:, None],
        "wg": [_group_experts(w_gate[l], True) for l in range(DEPTH)],
        "wu": [_group_experts(w_up[l], True) for l in range(DEPTH)],
        "wd": [_group_experts(w_down[l], False) for l in range(DEPTH)],
        "g_final": g_final[None, :],
    }
    y_prompt = _trunk(x_prompt, mod[:, :nbp], p)
    y_sample = _trunk(x_sample, mod[:, nbp:], p)
    return (y_prompt, y_sample)
```

```python
import functools
import math

import numpy as np
import jax
import jax.numpy as jnp
from jax import lax
from jax.experimental import pallas as pl
from jax.experimental.pallas import tpu as pltpu

F32 = jnp.float32
BF16 = jnp.bfloat16
QK_DTYPE = jnp.float8_e4m3fn
QK_MAX = float(jnp.finfo(QK_DTYPE).max)
LANES = 128
SUBLANES = 8

D_MODEL = 1024
DEPTH = 4
GRID_W = 64
HEAD_DIM = 64
N_HEADS_A = 8
N_KV_A = 2
N_HEADS_B = 8
D_NOPE = 64
D_ROPE = 32
D_V = 64
D_CQ = 384
D_CKV = 256
N_FGROUPS = 4
FGROUP = D_MODEL // N_FGROUPS
ROPE_THETA = 10000.0
N_EXPERTS = 16
N_EGROUPS = 4
EXPERTS_PER_GROUP = 4
D_EXPERT = 256
EPS = 1e-6
IN_SIZES = (N_HEADS_A * HEAD_DIM, N_KV_A * HEAD_DIM, N_KV_A * HEAD_DIM, D_CQ, D_CKV, D_ROPE)
IN_WIDTH = sum(IN_SIZES)
IN_OFFS = tuple(int(v) for v in np.cumsum((0,) + IN_SIZES))

N_HEADS = N_HEADS_A + N_HEADS_B
N_KSLOTS = 1 + N_HEADS_B
N_VSLOTS = N_KV_A + N_HEADS_B
QK_PAD = LANES
LOG2E = 1.4426950408889634
SCALE_A = HEAD_DIM ** -0.5 * LOG2E
SCALE_B = (D_NOPE + D_ROPE) ** -0.5 * LOG2E
NEG_BIG = -1e30

FFT_N2 = 128
FFT_T = SUBLANES

TOKEN_TILE = 512
Q_TILE = 512
ATTN_TRIP = 6
V_ROWS = D_V + 16
MOE_TILE = 512
MOE_CHUNK = 128
MOE_ROWS = MOE_TILE + (N_EGROUPS - 1) * MOE_CHUNK
VMEM_LIMIT = 56 * 1024 * 1024


def _cparams(sem):
    return pltpu.CompilerParams(dimension_semantics=sem, vmem_limit_bytes=VMEM_LIMIT)


def _to_qk(x):
    return jnp.clip(x, -QK_MAX, QK_MAX).astype(QK_DTYPE)


def _sigmoid(x):
    return 1.0 / (1.0 + jnp.exp(-x))


def _rms_rows(x, g):
    ms = jnp.mean(x * x, axis=-1, keepdims=True)
    return x * lax.rsqrt(ms + EPS) * g


def _rms_cols(xt, gcol):
    ms = jnp.mean(xt * xt, axis=0, keepdims=True)
    return xt * lax.rsqrt(ms + EPS) * gcol


def _dot(a, b):
    return jnp.dot(a, b, preferred_element_type=F32)


def _dot_nt(a, b):
    return lax.dot_general(a, b, (((1,), (1,)), ((), ())), preferred_element_type=F32)


def _dot_tn(a, b):
    return lax.dot_general(a, b, (((0,), (0,)), ((), ())), preferred_element_type=F32)


def _mod_kernel(c_ref, w_ref, b_ref, o_ref):
    c = c_ref[...]
    ca = c * _sigmoid(c)
    o_ref[...] = _dot(ca, w_ref[...]) + b_ref[...]


def _modulation(c_all, w_ada, b_ada):
    nb = c_all.shape[0]
    tn = 1536
    return pl.pallas_call(
        _mod_kernel,
        out_shape=jax.ShapeDtypeStruct((DEPTH, nb, 6 * D_MODEL), F32),
        grid=(DEPTH, 6 * D_MODEL // tn),
        in_specs=[
            pl.BlockSpec((nb, D_MODEL), lambda l, j: (0, 0)),
            pl.BlockSpec((None, D_MODEL, tn), lambda l, j: (l, 0, j)),
            pl.BlockSpec((None, 1, tn), lambda l, j: (l, 0, j)),
        ],
        out_specs=pl.BlockSpec((None, nb, tn), lambda l, j: (l, 0, j)),
        compiler_params=_cparams(("arbitrary", "arbitrary")),
        name="adaln_mod",
    )(c_all, w_ada, b_ada.reshape(DEPTH, 1, 6 * D_MODEL))


def _rope_t(xt, c, s):
    n = c.shape[0]
    x1, x2 = xt[:n], xt[n:]
    return jnp.concatenate([x1 * c - x2 * s, x1 * s + x2 * c], axis=0)


def _axial_t(xt, cr, sr, cc, sc):
    d2 = xt.shape[0] // 2
    return jnp.concatenate([_rope_t(xt[:d2], cr, sr), _rope_t(xt[d2:], cc, sc)], axis=0)


def _pre_attn_kernel(x_ref, mod_ref, gmix_ref, tab_ref, wint_ref, gqa_ref, gka_ref, gcq_ref,
                     wuqt_ref, gckv_ref, wukvt_ref, q_ref, k_ref, v_ref):
    x = x_ref[...]
    tm = x.shape[0]
    mod = mod_ref[...]
    h = _rms_rows(x, gmix_ref[...]) * (1.0 + mod[1:2]) + mod[0:1]
    zt = _dot_nt(wint_ref[...], h.astype(BF16))

    tab = tab_ref[...]
    ra = (tab[0:16], tab[16:32], tab[32:48], tab[48:64])
    rb = (tab[64:72], tab[72:80], tab[80:88], tab[88:96])
    zeros64 = jnp.zeros((HEAD_DIM, tm), F32)
    zeros32 = jnp.zeros((QK_PAD - D_NOPE - D_ROPE, tm), F32)

    o_q, o_k, o_v, o_cq, o_ckv, o_kr = IN_OFFS[:6]
    gqa = gqa_ref[...]
    for hh in range(N_HEADS_A):
        qh = _axial_t(_rms_cols(zt[o_q + HEAD_DIM * hh:o_q + HEAD_DIM * (hh + 1)], gqa), *ra) * SCALE_A
        parts = [qh, zeros64] if hh // (N_HEADS_A // N_KV_A) == 0 else [zeros64, qh]
        q_ref[hh] = _to_qk(jnp.concatenate(parts, axis=0))
    gka = gka_ref[...]
    kts = [jnp.concatenate(
        [_axial_t(_rms_cols(zt[o_k + HEAD_DIM * g:o_k + HEAD_DIM * (g + 1)], gka), *ra)
         for g in range(N_KV_A)], axis=0)]
    ones16 = jnp.ones((V_ROWS - D_V, tm), F32)
    for g in range(N_KV_A):
        v_ref[g] = jnp.concatenate(
            [zt[o_v + HEAD_DIM * g:o_v + HEAD_DIM * (g + 1)], ones16], axis=0).astype(BF16)

    cqn = _rms_cols(zt[o_cq:o_ckv], gcq_ref[...]).astype(BF16)
    qbt = _dot(wuqt_ref[...], cqn)
    dq = D_NOPE + D_ROPE
    for hh in range(N_HEADS_B):
        nope = qbt[dq * hh:dq * hh + D_NOPE]
        rp = _axial_t(qbt[dq * hh + D_NOPE:dq * (hh + 1)], *rb)
        q_ref[N_HEADS_A + hh] = _to_qk(jnp.concatenate([nope, rp, zeros32], axis=0) * SCALE_B)
    ckvn = _rms_cols(zt[o_ckv:o_kr], gckv_ref[...]).astype(BF16)
    kvt = _dot(wukvt_ref[...], ckvn)
    kr = _axial_t(zt[o_kr:o_kr + D_ROPE], *rb)
    dkv = D_NOPE + D_V
    for hh in range(N_HEADS_B):
        kts.append(jnp.concatenate([kvt[dkv * hh:dkv * hh + D_NOPE], kr, zeros32], axis=0))
        v_ref[N_KV_A + hh] = jnp.concatenate(
            [kvt[dkv * hh + D_NOPE:dkv * (hh + 1)], ones16], axis=0).astype(BF16)

    k_all = jnp.concatenate(kts, axis=0).T
    for i in range(N_KSLOTS):
        k_ref[i] = _to_qk(k_all[:, QK_PAD * i:QK_PAD * (i + 1)])


def _pre_attn(x, mod_l, g_mix, tab, wint, gqa, gka, gcq, wuqt, gckv, wukvt):
    B, S, _ = x.shape
    tm = TOKEN_TILE
    nt = S // tm
    const = lambda b, i: (0, 0)
    return pl.pallas_call(
        _pre_attn_kernel,
        out_shape=(
            jax.ShapeDtypeStruct((B, N_HEADS, QK_PAD, S), QK_DTYPE),
            jax.ShapeDtypeStruct((B, N_KSLOTS, S, QK_PAD), QK_DTYPE),
            jax.ShapeDtypeStruct((B, N_VSLOTS, nt, V_ROWS, tm), BF16),
        ),
        grid=(B, nt),
        in_specs=[
            pl.BlockSpec((None, tm, D_MODEL), lambda b, i: (b, i, 0)),
            pl.BlockSpec((None, 6, D_MODEL), lambda b, i: (b, 0, 0)),
            pl.BlockSpec((1, D_MODEL), const),
            pl.BlockSpec((96, tm), lambda b, i: (0, i)),
            pl.BlockSpec((IN_WIDTH, D_MODEL), const),
            pl.BlockSpec((HEAD_DIM, 1), const),
            pl.BlockSpec((HEAD_DIM, 1), const),
            pl.BlockSpec((D_CQ, 1), const),
            pl.BlockSpec((N_HEADS_B * (D_NOPE + D_ROPE), D_CQ), const),
            pl.BlockSpec((D_CKV, 1), const),
            pl.BlockSpec((N_HEADS_B * (D_NOPE + D_V), D_CKV), const),
        ],
        out_specs=(
            pl.BlockSpec((None, N_HEADS, QK_PAD, tm), lambda b, i: (b, 0, 0, i)),
            pl.BlockSpec((None, N_KSLOTS, tm, QK_PAD), lambda b, i: (b, 0, i, 0)),
            pl.BlockSpec((None, N_VSLOTS, None, V_ROWS, tm), lambda b, i: (b, 0, i, 0, 0)),
        ),
        compiler_params=_cparams(("parallel", "parallel")),
        name="pre_attn",
    )(x, mod_l, g_mix, tab, wint, gqa, gka, gcq, wuqt, gckv, wukvt)


def _attn_kernel(q_ref, k_ref, v_ref, o_ref, sa_scr, sb_scr, *, n_chunks, tk, trip):
    qt = q_ref[...]
    tq = qt.shape[1]
    bufs = (sa_scr, sb_scr)

    def scores(c, s_ref):
        off = c * tk if isinstance(c, int) else pl.multiple_of(c * tk, tk)
        s = _dot(k_ref[pl.ds(off, tk), :], qt)
        s_ref[...] = s
        return jnp.max(s, axis=0, keepdims=True)

    def update(c, s_ref, cm, m, acc):
        m_new = jnp.maximum(m, cm)
        p = jnp.exp2(s_ref[...] - m_new).astype(BF16)
        return m_new, acc * jnp.exp2(m - m_new) + _dot(v_ref[c], p)

    def steps(c0, count, cm, m, acc, last):
        for u in range(count):
            if not (last and u == count - 1):
                cm_next = scores(c0 + u + 1, bufs[(u + 1) % 2])
            m, acc = update(c0 + u, bufs[u % 2], cm, m, acc)
            cm = cm_next
        return cm, m, acc

    def body(i, carry):
        return steps(i * trip, trip, *carry, last=False)

    m0 = jnp.full((1, tq), NEG_BIG, F32)
    acc0 = jnp.zeros((V_ROWS, tq), F32)
    n_trips = (n_chunks - 2) // trip
    carry = lax.fori_loop(0, n_trips, body, (scores(0, sa_scr), m0, acc0))
    _, _, acc = steps(n_trips * trip, n_chunks - n_trips * trip, *carry, last=True)
    o_ref[...] = (acc[:D_V] * (1.0 / acc[D_V:D_V + 1])).astype(o_ref.dtype)


def _attention(qt, k, vt):
    B, _, _, S = qt.shape
    n_chunks, tk = vt.shape[2], vt.shape[4]
    tq = Q_TILE
    kslot = lambda h: jnp.where(h < N_HEADS_A, 0, h - (N_HEADS_A - 1))
    vslot = lambda h: jnp.where(h < N_HEADS_A, h // (N_HEADS_A // N_KV_A), h - (N_HEADS_A - N_KV_A))
    return pl.pallas_call(
        functools.partial(_attn_kernel, n_chunks=n_chunks, tk=tk,
                          trip=ATTN_TRIP),
        out_shape=jax.ShapeDtypeStruct((B, N_HEADS, D_V, S), BF16),
        grid=(B, N_HEADS, S // tq),
        in_specs=[
            pl.BlockSpec((None, None, QK_PAD, tq), lambda b, h, i: (b, h, 0, i)),
            pl.BlockSpec((None, None, S, QK_PAD), lambda b, h, i: (b, kslot(h), 0, 0)),
            pl.BlockSpec((None, None, n_chunks, V_ROWS, tk), lambda b, h, i: (b, vslot(h), 0, 0, 0)),
        ],
        out_specs=pl.BlockSpec((None, None, D_V, tq), lambda b, h, i: (b, h, 0, i)),
        scratch_shapes=[pltpu.VMEM((tk, tq), F32), pltpu.VMEM((tk, tq), F32)],
        compiler_params=_cparams(("parallel", "parallel", "arbitrary")),
        name="attn_sweep",
    )(qt, k, vt)


def _f1_kernel(x_ref, mod_ref, gmix_ref, fc_ref, m1_ref, twr_ref, twi_ref, y_ref, z_scr):
    n1 = x_ref.shape[0]
    x = x_ref[...].reshape(n1 * FFT_T, D_MODEL)
    mod = mod_ref[...]
    h = _rms_rows(x, gmix_ref[...]) * (1.0 + mod[1:2]) + mod[0:1]
    hb = h.astype(BF16)
    fc = fc_ref[...]
    for g in range(N_FGROUPS):
        zg = _dot(hb[:, FGROUP * g:FGROUP * (g + 1)], fc)
        z_scr[:, :, FGROUP * g:FGROUP * (g + 1)] = zg[:, :FGROUP].reshape(n1, FFT_T, FGROUP)
        z_scr[:, :, D_MODEL + FGROUP * g:D_MODEL + FGROUP * (g + 1)] = zg[:, FGROUP:].reshape(n1, FFT_T, FGROUP)
    m1 = m1_ref[...]
    for j in range(FFT_T):
        zj = z_scr[:, j, :]
        st = jnp.concatenate([zj[:, :D_MODEL], zj[:, D_MODEL:]], axis=0).astype(BF16)
        y = _dot(m1, st)
        yr, yi = y[:n1], y[n1:]
        tr = jnp.tile(twr_ref[j], (1, D_MODEL // LANES))
        ti = jnp.tile(twi_ref[j], (1, D_MODEL // LANES))
        y_ref[j, :, 0:D_MODEL] = (yr * tr - yi * ti).astype(BF16)
        y_ref[j, :, D_MODEL:2 * D_MODEL] = (yr * ti + yi * tr).astype(BF16)


def _f2_kernel(y_ref, x_ref, mod_ref, cs_ref, wc_ref, o_ref, f_scr):
    cs = cs_ref[...]
    for j in range(FFT_T):
        st = jnp.concatenate([y_ref[:, 2 * D_MODEL * j:2 * D_MODEL * j + D_MODEL],
                              y_ref[:, 2 * D_MODEL * j + D_MODEL:2 * D_MODEL * (j + 1)]], axis=0)
        f_scr[FFT_N2 * j:FFT_N2 * (j + 1), :] = _dot(cs, st).astype(BF16)
    m = _dot(f_scr[...], wc_ref[...])
    gt1 = mod_ref[...][2:3]
    for j in range(FFT_T):
        o_ref[:, j, :] = x_ref[:, j, :] + gt1 * m[FFT_N2 * j:FFT_N2 * (j + 1)]


def _dft_constants(S):
    n1 = S // FFT_N2
    k = np.arange(FGROUP)
    ang = 2.0 * np.pi * np.outer(k, k) / FGROUP
    fc = np.concatenate([np.cos(ang), -np.sin(ang)], axis=1) / math.sqrt(FGROUP)
    a = np.arange(n1)
    ang1 = 2.0 * np.pi * np.outer(a, a) / n1
    c1, s1 = np.cos(ang1), np.sin(ang1)
    m1 = np.block([[c1, s1], [-s1, c1]]) / math.sqrt(n1)
    b = np.arange(FFT_N2)
    ang2 = 2.0 * np.pi * np.outer(b, b) / FFT_N2
    cs = np.concatenate([np.cos(ang2), np.sin(ang2)], axis=1) / math.sqrt(FFT_N2)
    angt = 2.0 * np.pi * np.outer(b, a) / S
    twr = np.broadcast_to(np.cos(angt)[:, :, None], (FFT_N2, n1, LANES))
    twi = np.broadcast_to(-np.sin(angt)[:, :, None], (FFT_N2, n1, LANES))
    return (jnp.asarray(fc, BF16), jnp.asarray(m1, BF16), jnp.asarray(cs, BF16),
            jnp.asarray(twr, F32), jnp.asarray(twi, F32))


def _fourier_layer(x, mod_l, g_mix, wc):
    B, S, _ = x.shape
    n1 = S // FFT_N2
    fc, m1, cs, twr, twi = _dft_constants(S)
    const2 = lambda b, i: (0, 0)
    y = pl.pallas_call(
        _f1_kernel,
        out_shape=jax.ShapeDtypeStruct((B, FFT_N2, n1, 2 * D_MODEL), BF16),
        grid=(B, FFT_N2 // FFT_T),
        in_specs=[
            pl.BlockSpec((None, n1, FFT_T, D_MODEL), lambda b, i: (b, 0, i, 0)),
            pl.BlockSpec((None, 6, D_MODEL), lambda b, i: (b, 0, 0)),
            pl.BlockSpec((1, D_MODEL), const2),
            pl.BlockSpec((FGROUP, 2 * FGROUP), const2),
            pl.BlockSpec((2 * n1, 2 * n1), const2),
            pl.BlockSpec((FFT_T, n1, LANES), lambda b, i: (i, 0, 0)),
            pl.BlockSpec((FFT_T, n1, LANES), lambda b, i: (i, 0, 0)),
        ],
        out_specs=pl.BlockSpec((None, FFT_T, n1, 2 * D_MODEL), lambda b, i: (b, i, 0, 0)),
        scratch_shapes=[pltpu.VMEM((n1, FFT_T, 2 * D_MODEL), F32)],
        compiler_params=_cparams(("parallel", "parallel")),
        name="fourier_stage1",
    )(x.reshape(B, n1, FFT_N2, D_MODEL), mod_l, g_mix, fc, m1, twr, twi)
    out = pl.pallas_call(
        _f2_kernel,
        out_shape=jax.ShapeDtypeStruct((B, FFT_N2, n1, D_MODEL), F32),
        grid=(B, n1 // FFT_T),
        in_specs=[
            pl.BlockSpec((None, FFT_N2, FFT_T * 2 * D_MODEL), lambda b, i: (b, 0, i)),
            pl.BlockSpec((None, FFT_N2, FFT_T, D_MODEL), lambda b, i: (b, 0, i, 0)),
            pl.BlockSpec((None, 6, D_MODEL), lambda b, i: (b, 0, 0)),
            pl.BlockSpec((FFT_N2, 2 * FFT_N2), const2),
            pl.BlockSpec((D_MODEL, D_MODEL), const2),
        ],
        out_specs=pl.BlockSpec((None, FFT_N2, FFT_T, D_MODEL), lambda b, i: (b, 0, i, 0)),
        scratch_shapes=[pltpu.VMEM((FFT_T * FFT_N2, D_MODEL), BF16)],
        compiler_params=_cparams(("parallel", "parallel")),
        name="fourier_stage2",
    )(y.reshape(B, FFT_N2, n1 * 2 * D_MODEL), x.reshape(B, FFT_N2, n1, D_MODEL), mod_l, cs, wc)
    return out.reshape(B, S, D_MODEL)


def _route_t(s, sb):
    rows = [sb[e:e + 1] for e in range(N_EXPERTS)]
    gscore = []
    for g in range(N_EGROUPS):
        a, b, c, d = rows[4 * g:4 * g + 4]
        gscore.append(jnp.maximum(jnp.maximum(jnp.maximum(a + b, a + c), jnp.maximum(a + d, b + c)),
                                  jnp.maximum(b + d, c + d)))
    best = gscore[0]
    gsel = jnp.zeros_like(best)
    for g in range(1, N_EGROUPS):
        upd = gscore[g] > best
        gsel = jnp.where(upd, float(g), gsel)
        best = jnp.where(upd, gscore[g], best)
    picked = []
    for e in range(N_EXPERTS):
        g = e // EXPERTS_PER_GROUP
        rank = jnp.zeros_like(best)
        for j in range(4 * g, 4 * g + 4):
            if j == e:
                continue
            ahead = (rows[j] >= rows[e]) if j < e else (rows[j] > rows[e])
            rank = rank + jnp.where(ahead, 1.0, 0.0)
        sel = jnp.where(rank < 1.5, 1.0, 0.0) * jnp.where(gsel == float(g), 1.0, 0.0)
        picked.append(sel * s[e:e + 1])
    total = picked[0]
    for e in range(1, N_EXPERTS):
        total = total + picked[e]
    inv = 1.0 / total
    return [p * inv for p in picked], gsel


def _moe_kernel(*refs, final, after_attention):
    if after_attention:
        o_ref, wo_ref, *refs = refs
    (x_ref, mod_ref, gffn_ref, wr_ref, bias_ref, wg_ref, wu_ref, wd_ref, gfin_ref,
     out_ref, srt_scr, y_scr) = refs
    tm = x_ref.shape[0]
    x = x_ref[...]
    mod = mod_ref[...]
    if after_attention:
        ot = o_ref[...].reshape(N_HEADS * D_V, tm)
        x = x + mod[2:3] * _dot_tn(ot, wo_ref[...])
    t = _rms_rows(x, gffn_ref[...]) * (1.0 + mod[4:5]) + mod[3:4]
    t_hi = t.astype(BF16)
    t_lo = (t - t_hi.astype(F32)).astype(BF16)
    wr = wr_ref[...]
    a = _dot_nt(wr, t_hi)
    b = _dot_nt(wr[0:N_EXPERTS], t_lo)
    logits = a[0:N_EXPERTS] + a[N_EXPERTS:2 * N_EXPERTS] + b
    s = _sigmoid(logits)
    gates, gsel = _route_t(s, s + bias_ref[...])

    masks = [jnp.where(gsel == float(g), 1.0, 0.0) for g in range(N_EGROUPS)]
    m8 = jnp.concatenate(masks + [jnp.zeros((SUBLANES - N_EGROUPS, tm), F32)], axis=0)
    earlier = jnp.where(lax.broadcasted_iota(jnp.int32, (tm, tm), 0)
                        < lax.broadcasted_iota(jnp.int32, (tm, tm), 1), 1.0, 0.0).astype(BF16)
    ranks = _dot(m8.astype(BF16), earlier)
    counts = jnp.sum(m8, axis=1, keepdims=True)
    n_chunks = jnp.floor((counts + (MOE_CHUNK - 1)) * (1.0 / MOE_CHUNK))
    starts = []
    start = jnp.zeros((1, 1), F32)
    for g in range(N_EGROUPS):
        starts.append(start)
        start = start + n_chunks[g:g + 1] * MOE_CHUNK
    pos = masks[0] * (starts[0] + ranks[0:1])
    for g in range(1, N_EGROUPS):
        pos = pos + masks[g] * (starts[g] + ranks[g:g + 1])
    perm = jnp.where(lax.broadcasted_iota(jnp.int32, (MOE_ROWS, tm), 0) == pos.astype(jnp.int32),
                     1.0, 0.0).astype(BF16)

    g4 = []
    for j in range(EXPERTS_PER_GROUP):
        r = masks[0] * gates[j]
        for g in range(1, N_EGROUPS):
            r = r + masks[g] * gates[EXPERTS_PER_GROUP * g + j]
        g4.append(r)
    g4t = jnp.concatenate(g4 + [jnp.zeros((LANES - EXPERTS_PER_GROUP, tm), F32)], axis=0).T
    src = jnp.concatenate([t_hi, g4t.astype(BF16)], axis=1)
    srt_scr[...] = _dot(perm, src).astype(BF16)
    y_scr[...] = jnp.zeros_like(y_scr)

    for g in range(N_EGROUPS):
        base = starts[g][0, 0].astype(jnp.int32)

        def chunk(j, carry, g=g, base=base):
            off = pl.multiple_of(base + j * MOE_CHUNK, MOE_CHUNK)
            xs = srt_scr[pl.ds(off, MOE_CHUNK), 0:D_MODEL]
            gs = srt_scr[pl.ds(off, MOE_CHUNK), D_MODEL:D_MODEL + LANES].astype(F32)
            u1 = _dot(xs, wg_ref[g])
            u2 = _dot(xs, wu_ref[g])
            he = u1 * _sigmoid(u1) * u2
            he = jnp.concatenate(
                [he[:, D_EXPERT * e:D_EXPERT * (e + 1)] * gs[:, e:e + 1] for e in range(EXPERTS_PER_GROUP)],
                axis=1)
            y_scr[pl.ds(off, MOE_CHUNK), :] = _dot(he.astype(BF16), wd_ref[g]).astype(BF16)
            return carry

        lax.fori_loop(0, n_chunks[g, 0].astype(jnp.int32), chunk, 0)

    y = x + mod[5:6] * _dot_tn(perm, y_scr[...])
    if final:
        y = _rms_rows(y, gfin_ref[...])
    out_ref[...] = y


def _moe(x, mod_l, g_ffn, wr, bias, wg, wu, wd, g_final, final, attn=None):
    B, S, _ = x.shape
    tm = MOE_TILE
    const = lambda b, i: (0, 0)
    wspec = pl.BlockSpec((N_EGROUPS, D_MODEL, D_MODEL), lambda b, i: (0, 0, 0),
                         pipeline_mode=pl.Buffered(1))
    attn_specs = [] if attn is None else [
        pl.BlockSpec((None, N_HEADS, D_V, tm), lambda b, i: (b, 0, 0, i)),
        pl.BlockSpec((D_MODEL, D_MODEL), const, pipeline_mode=pl.Buffered(1)),
    ]
    return pl.pallas_call(
        functools.partial(_moe_kernel, final=final, after_attention=attn is not None),
        out_shape=jax.ShapeDtypeStruct(x.shape, F32),
        grid=(B, S // tm),
        in_specs=attn_specs + [
            pl.BlockSpec((None, tm, D_MODEL), lambda b, i: (b, i, 0)),
            pl.BlockSpec((None, 6, D_MODEL), lambda b, i: (b, 0, 0)),
            pl.BlockSpec((1, D_MODEL), const),
            pl.BlockSpec((2 * N_EXPERTS, D_MODEL), const),
            pl.BlockSpec((N_EXPERTS, 1), const),
            wspec, wspec, wspec,
            pl.BlockSpec((1, D_MODEL), const),
        ],
        out_specs=pl.BlockSpec((None, tm, D_MODEL), lambda b, i: (b, i, 0)),
        scratch_shapes=[
            pltpu.VMEM((MOE_ROWS, D_MODEL + LANES), BF16),
            pltpu.VMEM((MOE_ROWS, D_MODEL), BF16),
        ],
        compiler_params=_cparams(("parallel", "parallel")),
        name="moe",
    )(*(attn or ()), x, mod_l, g_ffn, wr, bias, wg, wu, wd, g_final)


def _rope_tables(S):
    pos = jnp.arange(S, dtype=jnp.int32)
    rows = (pos // GRID_W).astype(F32)[None, :]
    cols = (pos % GRID_W).astype(F32)[None, :]
    tabs = []
    for half in (HEAD_DIM // 2, D_ROPE // 2):
        inv = (ROPE_THETA ** (-jnp.arange(0, half, 2, dtype=F32) / half))[:, None]
        for p in (rows, cols):
            ang = inv * p
            tabs += [jnp.cos(ang), jnp.sin(ang)]
    return jnp.concatenate(tabs, axis=0)


def _group_experts(w, transpose_cols):
    e, a, b = w.shape
    w = w.reshape(N_EGROUPS, EXPERTS_PER_GROUP, a, b)
    if transpose_cols:
        return w.transpose(0, 2, 1, 3).reshape(N_EGROUPS, a, EXPERTS_PER_GROUP * b).astype(BF16)
    return w.reshape(N_EGROUPS, EXPERTS_PER_GROUP * a, b).astype(BF16)


def _trunk(x, mod, p):
    B, S, _ = x.shape
    tab = _rope_tables(S)
    for l in range(DEPTH):
        mod_l = mod[l]
        g_mix = p["g_mix"][l][None, :]
        if l % 2 == 0:
            i = l // 2
            qt, k, vt = _pre_attn(x, mod_l, g_mix, tab, p["wint"][i], p["g_qa"][i][:, None],
                                  p["g_ka"][i][:, None], p["g_cq"][i][:, None], p["wuqt"][i],
                                  p["g_ckv"][i][:, None], p["wukvt"][i])
            attn = (_attention(qt, k, vt), p["wo"][i])
        else:
            attn = None
            x = _fourier_layer(x, mod_l, g_mix, p["wc"][l // 2])
        x = _moe(x, mod_l, p["g_ffn"][l][None, :], p["wr"], p["bias"], p["wg"][l], p["wu"][l],
                 p["wd"][l], p["g_final"], final=(l == DEPTH - 1), attn=attn)
    return x


def kernel(x_prompt, x_sample, c_prompt, c_sample, w_ada, b_ada, g_mix, g_ffn, w_in, g_qa, g_ka, g_cq,
           w_uq, g_ckv, w_ukv, w_o_attn, w_fourier, w_router, router_bias, w_gate, w_up, w_down, g_final):
    nbp = x_prompt.shape[0]
    mod = _modulation(jnp.concatenate([c_prompt, c_sample], axis=0), w_ada, b_ada)
    mod = mod.reshape(DEPTH, -1, 6, D_MODEL)
    wrt = w_router.T
    wr_hi = wrt.astype(BF16)
    wr_lo = (wrt - wr_hi.astype(F32)).astype(BF16)
    p = {
        "g_mix": g_mix, "g_ffn": g_ffn, "g_qa": g_qa, "g_ka": g_ka, "g_cq": g_cq, "g_ckv": g_ckv,
        "wint": jnp.swapaxes(w_in, 1, 2).astype(BF16),
        "wuqt": jnp.swapaxes(w_uq, 1, 2).astype(BF16),
        "wukvt": jnp.swapaxes(w_ukv, 1, 2).astype(BF16),
        "wo": w_o_attn.astype(BF16),
        "wc": w_fourier.astype(BF16),
        "wr": jnp.concatenate([wr_hi, wr_lo], axis=0),
        "bias": router_bias[:, None],
        "wg": [_group_experts(w_gate[l], True) for l in range(DEPTH)],
        "wu": [_group_experts(w_up[l], True) for l in range(DEPTH)],
        "wd": [_group_experts(w_down[l], False) for l in range(DEPTH)],
        "g_final": g_final[None, :],
    }
    y_prompt = _trunk(x_prompt, mod[:, :nbp], p)
    y_sample = _trunk(x_sample, mod[:, nbp:], p)
    return (y_prompt, y_sample)
```

```python
import functools
import math

import numpy as np
import jax
import jax.numpy as jnp
from jax import lax
from jax.experimental import pallas as pl
from jax.experimental.pallas import tpu as pltpu

F32 = jnp.float32
BF16 = jnp.bfloat16
QK_DTYPE = jnp.float8_e4m3fn
QK_MAX = float(jnp.finfo(QK_DTYPE).max)
LANES = 128
SUBLANES = 8

D_MODEL = 1024
DEPTH = 4
GRID_W = 64
HEAD_DIM = 64
N_HEADS_A = 8
N_KV_A = 2
N_HEADS_B = 8
D_NOPE = 64
D_ROPE = 32
D_V = 64
D_CQ = 384
D_CKV = 256
N_FGROUPS = 4
FGROUP = D_MODEL // N_FGROUPS
ROPE_THETA = 10000.0
N_EXPERTS = 16
N_EGROUPS = 4
EXPERTS_PER_GROUP = 4
D_EXPERT = 256
EPS = 1e-6
IN_SIZES = (N_HEADS_A * HEAD_DIM, N_KV_A * HEAD_DIM, N_KV_A * HEAD_DIM, D_CQ, D_CKV, D_ROPE)
IN_WIDTH = sum(IN_SIZES)
IN_OFFS = tuple(int(v) for v in np.cumsum((0,) + IN_SIZES))

N_HEADS = N_HEADS_A + N_HEADS_B
N_KSLOTS = 1 + N_HEADS_B
N_VSLOTS = N_KV_A + N_HEADS_B
QK_PAD = LANES
LOG2E = 1.4426950408889634
SCALE_A = HEAD_DIM ** -0.5 * LOG2E
SCALE_B = (D_NOPE + D_ROPE) ** -0.5 * LOG2E
NEG_BIG = -1e30

FFT_N2 = 128
FFT_T = SUBLANES

TOKEN_TILE = 512
Q_TILE = 512
Q_TILES_PER_STEP = 4
ATTN_TRIP = 6
V_ROWS = D_V + 16
MOE_TILE = 512
MOE_CHUNK = 128
MOE_ROWS = MOE_TILE + (N_EGROUPS - 1) * MOE_CHUNK
VMEM_LIMIT = 56 * 1024 * 1024


def _cparams(sem):
    return pltpu.CompilerParams(dimension_semantics=sem, vmem_limit_bytes=VMEM_LIMIT)


def _to_qk(x):
    return jnp.clip(x, -QK_MAX, QK_MAX).astype(QK_DTYPE)


def _sigmoid(x):
    return 1.0 / (1.0 + jnp.exp(-x))


def _rms_rows(x, g):
    ms = jnp.mean(x * x, axis=-1, keepdims=True)
    return x * lax.rsqrt(ms + EPS) * g


def _rms_cols(xt, gcol):
    ms = jnp.mean(xt * xt, axis=0, keepdims=True)
    return xt * lax.rsqrt(ms + EPS) * gcol


def _dot(a, b):
    return jnp.dot(a, b, preferred_element_type=F32)


def _dot_nt(a, b):
    return lax.dot_general(a, b, (((1,), (1,)), ((), ())), preferred_element_type=F32)


def _dot_tn(a, b):
    return lax.dot_general(a, b, (((0,), (0,)), ((), ())), preferred_element_type=F32)


def _mod_kernel(c_ref, w_ref, b_ref, o_ref):
    c = c_ref[...]
    ca = c * _sigmoid(c)
    o_ref[...] = _dot(ca, w_ref[...]) + b_ref[...]


def _modulation(c_all, w_ada, b_ada):
    nb = c_all.shape[0]
    tn = 1536
    return pl.pallas_call(
        _mod_kernel,
        out_shape=jax.ShapeDtypeStruct((DEPTH, nb, 6 * D_MODEL), F32),
        grid=(DEPTH, 6 * D_MODEL // tn),
        in_specs=[
            pl.BlockSpec((nb, D_MODEL), lambda l, j: (0, 0)),
            pl.BlockSpec((None, D_MODEL, tn), lambda l, j: (l, 0, j)),
            pl.BlockSpec((None, 1, tn), lambda l, j: (l, 0, j)),
        ],
        out_specs=pl.BlockSpec((None, nb, tn), lambda l, j: (l, 0, j)),
        compiler_params=_cparams(("arbitrary", "arbitrary")),
        name="adaln_mod",
    )(c_all, w_ada, b_ada.reshape(DEPTH, 1, 6 * D_MODEL))


def _rope_t(xt, c, s):
    n = c.shape[0]
    x1, x2 = xt[:n], xt[n:]
    return jnp.concatenate([x1 * c - x2 * s, x1 * s + x2 * c], axis=0)


def _axial_t(xt, cr, sr, cc, sc):
    d2 = xt.shape[0] // 2
    return jnp.concatenate([_rope_t(xt[:d2], cr, sr), _rope_t(xt[d2:], cc, sc)], axis=0)


def _pre_attn_kernel(x_ref, mod_ref, gmix_ref, tab_ref, wint_ref, gqa_ref, gka_ref, gcq_ref,
                     wuqt_ref, gckv_ref, wukvt_ref, q_ref, k_ref, v_ref):
    x = x_ref[...]
    tm = x.shape[0]
    mod = mod_ref[...]
    h = _rms_rows(x, gmix_ref[...]) * (1.0 + mod[1:2]) + mod[0:1]
    zt = _dot_nt(wint_ref[...], h.astype(BF16))

    tab = tab_ref[...]
    ra = (tab[0:16], tab[16:32], tab[32:48], tab[48:64])
    rb = (tab[64:72], tab[72:80], tab[80:88], tab[88:96])
    zeros64 = jnp.zeros((HEAD_DIM, tm), F32)
    zeros32 = jnp.zeros((QK_PAD - D_NOPE - D_ROPE, tm), F32)

    o_q, o_k, o_v, o_cq, o_ckv, o_kr = IN_OFFS[:6]
    gqa = gqa_ref[...]
    for hh in range(N_HEADS_A):
        qh = _axial_t(_rms_cols(zt[o_q + HEAD_DIM * hh:o_q + HEAD_DIM * (hh + 1)], gqa), *ra) * SCALE_A
        parts = [qh, zeros64] if hh // (N_HEADS_A // N_KV_A) == 0 else [zeros64, qh]
        q_ref[hh] = _to_qk(jnp.concatenate(parts, axis=0))
    gka = gka_ref[...]
    kts = [jnp.concatenate(
        [_axial_t(_rms_cols(zt[o_k + HEAD_DIM * g:o_k + HEAD_DIM * (g + 1)], gka), *ra)
         for g in range(N_KV_A)], axis=0)]
    ones16 = jnp.ones((V_ROWS - D_V, tm), F32)
    for g in range(N_KV_A):
        v_ref[g] = jnp.concatenate(
            [zt[o_v + HEAD_DIM * g:o_v + HEAD_DIM * (g + 1)], ones16], axis=0).astype(BF16)

    cqn = _rms_cols(zt[o_cq:o_ckv], gcq_ref[...]).astype(BF16)
    qbt = _dot(wuqt_ref[...], cqn)
    dq = D_NOPE + D_ROPE
    for hh in range(N_HEADS_B):
        nope = qbt[dq * hh:dq * hh + D_NOPE]
        rp = _axial_t(qbt[dq * hh + D_NOPE:dq * (hh + 1)], *rb)
        q_ref[N_HEADS_A + hh] = _to_qk(jnp.concatenate([nope, rp, zeros32], axis=0) * SCALE_B)
    ckvn = _rms_cols(zt[o_ckv:o_kr], gckv_ref[...]).astype(BF16)
    kvt = _dot(wukvt_ref[...], ckvn)
    kr = _axial_t(zt[o_kr:o_kr + D_ROPE], *rb)
    dkv = D_NOPE + D_V
    for hh in range(N_HEADS_B):
        kts.append(jnp.concatenate([kvt[dkv * hh:dkv * hh + D_NOPE], kr, zeros32], axis=0))
        v_ref[N_KV_A + hh] = jnp.concatenate(
            [kvt[dkv * hh + D_NOPE:dkv * (hh + 1)], ones16], axis=0).astype(BF16)

    k_all = jnp.concatenate(kts, axis=0).T
    for i in range(N_KSLOTS):
        k_ref[i] = _to_qk(k_all[:, QK_PAD * i:QK_PAD * (i + 1)])


def _pre_attn(x, mod_l, g_mix, tab, wint, gqa, gka, gcq, wuqt, gckv, wukvt):
    B, S, _ = x.shape
    tm = TOKEN_TILE
    nt = S // tm
    const = lambda b, i: (0, 0)
    return pl.pallas_call(
        _pre_attn_kernel,
        out_shape=(
            jax.ShapeDtypeStruct((B, N_HEADS, QK_PAD, S), QK_DTYPE),
            jax.ShapeDtypeStruct((B, N_KSLOTS, S, QK_PAD), QK_DTYPE),
            jax.ShapeDtypeStruct((B, N_VSLOTS, nt, V_ROWS, tm), BF16),
        ),
        grid=(B, nt),
        in_specs=[
            pl.BlockSpec((None, tm, D_MODEL), lambda b, i: (b, i, 0)),
            pl.BlockSpec((None, 6, D_MODEL), lambda b, i: (b, 0, 0)),
            pl.BlockSpec((1, D_MODEL), const),
            pl.BlockSpec((96, tm), lambda b, i: (0, i)),
            pl.BlockSpec((IN_WIDTH, D_MODEL), const),
            pl.BlockSpec((HEAD_DIM, 1), const),
            pl.BlockSpec((HEAD_DIM, 1), const),
            pl.BlockSpec((D_CQ, 1), const),
            pl.BlockSpec((N_HEADS_B * (D_NOPE + D_ROPE), D_CQ), const),
            pl.BlockSpec((D_CKV, 1), const),
            pl.BlockSpec((N_HEADS_B * (D_NOPE + D_V), D_CKV), const),
        ],
        out_specs=(
            pl.BlockSpec((None, N_HEADS, QK_PAD, tm), lambda b, i: (b, 0, 0, i)),
            pl.BlockSpec((None, N_KSLOTS, tm, QK_PAD), lambda b, i: (b, 0, i, 0)),
            pl.BlockSpec((None, N_VSLOTS, None, V_ROWS, tm), lambda b, i: (b, 0, i, 0, 0)),
        ),
        compiler_params=_cparams(("parallel", "parallel")),
        name="pre_attn",
    )(x, mod_l, g_mix, tab, wint, gqa, gka, gcq, wuqt, gckv, wukvt)


def _attn_kernel(q_ref, k_ref, v_ref, o_ref, sa_scr, sb_scr, *, n_chunks, tk, tq, trip):
    bufs = (sa_scr, sb_scr)
    tiles = [q_ref[:, tq * t:tq * (t + 1)] for t in range(q_ref.shape[1] // tq)]

    def scores(qt, c, s_ref):
        off = c * tk if isinstance(c, int) else pl.multiple_of(c * tk, tk)
        s = _dot(k_ref[pl.ds(off, tk), :], qt)
        s_ref[...] = s
        return jnp.max(s, axis=0, keepdims=True)

    def update(c, s_ref, cm, m, acc):
        m_new = jnp.maximum(m, cm)
        p = jnp.exp2(s_ref[...] - m_new).astype(BF16)
        return m_new, acc * jnp.exp2(m - m_new) + _dot(v_ref[c], p)

    def steps(qt, c0, count, cm, m, acc, then=None):
        for u in range(count):
            if not isinstance(c0, int) or c0 + u + 1 < n_chunks:
                cm_next = scores(qt, c0 + u + 1, bufs[(u + 1) % 2])
            else:
                cm_next = None if then is None else scores(then, 0, bufs[0])
            m, acc = update(c0 + u, bufs[u % 2], cm, m, acc)
            cm = cm_next
        return cm, m, acc

    n_trips = (n_chunks - 2) // trip
    cm = scores(tiles[0], 0, sa_scr)
    for t, qt in enumerate(tiles):
        init = (cm, jnp.full((1, tq), NEG_BIG, F32), jnp.zeros((V_ROWS, tq), F32))
        carry = lax.fori_loop(0, n_trips, lambda i, c, qt=qt: steps(qt, i * trip, trip, *c), init)
        cm, _, acc = steps(qt, n_trips * trip, n_chunks - n_trips * trip, *carry,
                           then=tiles[t + 1] if t + 1 < len(tiles) else None)
        o_ref[:, tq * t:tq * (t + 1)] = (acc[:D_V] * (1.0 / acc[D_V:D_V + 1])).astype(o_ref.dtype)


def _attention(qt, k, vt):
    B, _, _, S = qt.shape
    n_chunks, tk = vt.shape[2], vt.shape[4]
    tq = Q_TILE
    tqb = Q_TILE * Q_TILES_PER_STEP
    kslot = lambda h: jnp.where(h < N_HEADS_A, 0, h - (N_HEADS_A - 1))
    vslot = lambda h: jnp.where(h < N_HEADS_A, h // (N_HEADS_A // N_KV_A), h - (N_HEADS_A - N_KV_A))
    return pl.pallas_call(
        functools.partial(_attn_kernel, n_chunks=n_chunks, tk=tk, tq=tq, trip=ATTN_TRIP),
        out_shape=jax.ShapeDtypeStruct((B, N_HEADS, D_V, S), BF16),
        grid=(B, N_HEADS, S // tqb),
        in_specs=[
            pl.BlockSpec((None, None, QK_PAD, tqb), lambda b, h, i: (b, h, 0, i)),
            pl.BlockSpec((None, None, S, QK_PAD), lambda b, h, i: (b, kslot(h), 0, 0)),
            pl.BlockSpec((None, None, n_chunks, V_ROWS, tk), lambda b, h, i: (b, vslot(h), 0, 0, 0)),
        ],
        out_specs=pl.BlockSpec((None, None, D_V, tqb), lambda b, h, i: (b, h, 0, i)),
        scratch_shapes=[pltpu.VMEM((tk, tq), F32), pltpu.VMEM((tk, tq), F32)],
        compiler_params=_cparams(("parallel", "parallel", "arbitrary")),
        name="attn_sweep",
    )(qt, k, vt)


def _f1_kernel(x_ref, mod_ref, gmix_ref, fc_ref, m1_ref, twr_ref, twi_ref, y_ref, z_scr):
    n1 = x_ref.shape[0]
    x = x_ref[...].reshape(n1 * FFT_T, D_MODEL)
    mod = mod_ref[...]
    h = _rms_rows(x, gmix_ref[...]) * (1.0 + mod[1:2]) + mod[0:1]
    hb = h.astype(BF16)
    fc = fc_ref[...]
    for g in range(N_FGROUPS):
        zg = _dot(hb[:, FGROUP * g:FGROUP * (g + 1)], fc)
        z_scr[:, :, FGROUP * g:FGROUP * (g + 1)] = zg[:, :FGROUP].reshape(n1, FFT_T, FGROUP)
        z_scr[:, :, D_MODEL + FGROUP * g:D_MODEL + FGROUP * (g + 1)] = zg[:, FGROUP:].reshape(n1, FFT_T, FGROUP)
    m1 = m1_ref[...]
    for j in range(FFT_T):
        zj = z_scr[:, j, :]
        st = jnp.concatenate([zj[:, :D_MODEL], zj[:, D_MODEL:]], axis=0).astype(BF16)
        y = _dot(m1, st)
        yr, yi = y[:n1], y[n1:]
        tr = jnp.tile(twr_ref[j], (1, D_MODEL // LANES))
        ti = jnp.tile(twi_ref[j], (1, D_MODEL // LANES))
        y_ref[:, j, 0:D_MODEL] = yr * tr - yi * ti
        y_ref[:, j, D_MODEL:2 * D_MODEL] = yr * ti + yi * tr


def _f2_kernel(y_ref, x_ref, mod_ref, cs_ref, wc_ref, o_ref, f_scr):
    cs = cs_ref[...]
    for j in range(FFT_T):
        yk = y_ref[j]
        st = jnp.concatenate([yk[:, :D_MODEL], yk[:, D_MODEL:]], axis=0).astype(BF16)
        f_scr[FFT_N2 * j:FFT_N2 * (j + 1), :] = _dot(cs, st).astype(BF16)
    m = _dot(f_scr[...], wc_ref[...])
    gt1 = mod_ref[...][2:3]
    for j in range(FFT_T):
        o_ref[:, j, :] = x_ref[:, j, :] + gt1 * m[FFT_N2 * j:FFT_N2 * (j + 1)]


def _dft_constants(S):
    n1 = S // FFT_N2
    k = np.arange(FGROUP)
    ang = 2.0 * np.pi * np.outer(k, k) / FGROUP
    fc = np.concatenate([np.cos(ang), -np.sin(ang)], axis=1) / math.sqrt(FGROUP)
    a = np.arange(n1)
    ang1 = 2.0 * np.pi * np.outer(a, a) / n1
    c1, s1 = np.cos(ang1), np.sin(ang1)
    m1 = np.block([[c1, s1], [-s1, c1]]) / math.sqrt(n1)
    b = np.arange(FFT_N2)
    ang2 = 2.0 * np.pi * np.outer(b, b) / FFT_N2
    cs = np.concatenate([np.cos(ang2), np.sin(ang2)], axis=1) / math.sqrt(FFT_N2)
    angt = 2.0 * np.pi * np.outer(b, a) / S
    twr = np.broadcast_to(np.cos(angt)[:, :, None], (FFT_N2, n1, LANES))
    twi = np.broadcast_to(-np.sin(angt)[:, :, None], (FFT_N2, n1, LANES))
    return (jnp.asarray(fc, BF16), jnp.asarray(m1, BF16), jnp.asarray(cs, BF16),
            jnp.asarray(twr, F32), jnp.asarray(twi, F32))


def _fourier_layer(x, mod_l, g_mix, wc):
    B, S, _ = x.shape
    n1 = S // FFT_N2
    fc, m1, cs, twr, twi = _dft_constants(S)
    const2 = lambda b, i: (0, 0)
    y = pl.pallas_call(
        _f1_kernel,
        out_shape=jax.ShapeDtypeStruct((B, n1, FFT_N2, 2 * D_MODEL), F32),
        grid=(B, FFT_N2 // FFT_T),
        in_specs=[
            pl.BlockSpec((None, n1, FFT_T, D_MODEL), lambda b, i: (b, 0, i, 0)),
            pl.BlockSpec((None, 6, D_MODEL), lambda b, i: (b, 0, 0)),
            pl.BlockSpec((1, D_MODEL), const2),
            pl.BlockSpec((FGROUP, 2 * FGROUP), const2),
            pl.BlockSpec((2 * n1, 2 * n1), const2),
            pl.BlockSpec((FFT_T, n1, LANES), lambda b, i: (i, 0, 0)),
            pl.BlockSpec((FFT_T, n1, LANES), lambda b, i: (i, 0, 0)),
        ],
        out_specs=pl.BlockSpec((None, n1, FFT_T, 2 * D_MODEL), lambda b, i: (b, 0, i, 0)),
        scratch_shapes=[pltpu.VMEM((n1, FFT_T, 2 * D_MODEL), F32)],
        compiler_params=_cparams(("parallel", "parallel")),
        name="fourier_stage1",
    )(x.reshape(B, n1, FFT_N2, D_MODEL), mod_l, g_mix, fc, m1, twr, twi)
    out = pl.pallas_call(
        _f2_kernel,
        out_shape=jax.ShapeDtypeStruct((B, FFT_N2, n1, D_MODEL), F32),
        grid=(B, n1 // FFT_T),
        in_specs=[
            pl.BlockSpec((None, FFT_T, FFT_N2, 2 * D_MODEL), lambda b, i: (b, i, 0, 0)),
            pl.BlockSpec((None, FFT_N2, FFT_T, D_MODEL), lambda b, i: (b, 0, i, 0)),
            pl.BlockSpec((None, 6, D_MODEL), lambda b, i: (b, 0, 0)),
            pl.BlockSpec((FFT_N2, 2 * FFT_N2), const2),
            pl.BlockSpec((D_MODEL, D_MODEL), const2),
        ],
        out_specs=pl.BlockSpec((None, FFT_N2, FFT_T, D_MODEL), lambda b, i: (b, 0, i, 0)),
        scratch_shapes=[pltpu.VMEM((FFT_T * FFT_N2, D_MODEL), BF16)],
        compiler_params=_cparams(("parallel", "parallel")),
        name="fourier_stage2",
    )(y, x.reshape(B, FFT_N2, n1, D_MODEL), mod_l, cs, wc)
    return out.reshape(B, S, D_MODEL)


def _route_t(s, sb):
    rows = [sb[e:e + 1] for e in range(N_EXPERTS)]
    gscore = []
    for g in range(N_EGROUPS):
        a, b, c, d = rows[4 * g:4 * g + 4]
        gscore.append(jnp.maximum(jnp.maximum(jnp.maximum(a + b, a + c), jnp.maximum(a + d, b + c)),
                                  jnp.maximum(b + d, c + d)))
    best = gscore[0]
    gsel = jnp.zeros_like(best)
    for g in range(1, N_EGROUPS):
        upd = gscore[g] > best
        gsel = jnp.where(upd, float(g), gsel)
        best = jnp.where(upd, gscore[g], best)
    picked = []
    for e in range(N_EXPERTS):
        g = e // EXPERTS_PER_GROUP
        rank = jnp.zeros_like(best)
        for j in range(4 * g, 4 * g + 4):
            if j == e:
                continue
            ahead = (rows[j] >= rows[e]) if j < e else (rows[j] > rows[e])
            rank = rank + jnp.where(ahead, 1.0, 0.0)
        sel = jnp.where(rank < 1.5, 1.0, 0.0) * jnp.where(gsel == float(g), 1.0, 0.0)
        picked.append(sel * s[e:e + 1])
    total = picked[0]
    for e in range(1, N_EXPERTS):
        total = total + picked[e]
    inv = 1.0 / total
    return [p * inv for p in picked], gsel


def _moe_kernel(*refs, final, after_attention):
    if after_attention:
        o_ref, wo_ref, *refs = refs
    (x_ref, mod_ref, gffn_ref, wr_ref, bias_ref, wg_ref, wu_ref, wd_ref, gfin_ref,
     out_ref, srt_scr, y_scr) = refs
    tm = x_ref.shape[0]
    x = x_ref[...]
    mod = mod_ref[...]
    if after_attention:
        ot = o_ref[...].reshape(N_HEADS * D_V, tm)
        x = x + mod[2:3] * _dot_tn(ot, wo_ref[...])
    t = _rms_rows(x, gffn_ref[...]) * (1.0 + mod[4:5]) + mod[3:4]
    t_hi = t.astype(BF16)
    t_lo = (t - t_hi.astype(F32)).astype(BF16)
    wr = wr_ref[...]
    a = _dot_nt(wr, t_hi)
    b = _dot_nt(wr[0:N_EXPERTS], t_lo)
    logits = a[0:N_EXPERTS] + a[N_EXPERTS:2 * N_EXPERTS] + b
    s = _sigmoid(logits)
    gates, gsel = _route_t(s, s + bias_ref[...])

    masks = [jnp.where(gsel == float(g), 1.0, 0.0) for g in range(N_EGROUPS)]
    m8 = jnp.concatenate(masks + [jnp.zeros((SUBLANES - N_EGROUPS, tm), F32)], axis=0)
    earlier = jnp.where(lax.broadcasted_iota(jnp.int32, (tm, tm), 0)
                        < lax.broadcasted_iota(jnp.int32, (tm, tm), 1), 1.0, 0.0).astype(BF16)
    ranks = _dot(m8.astype(BF16), earlier)
    counts = jnp.sum(m8, axis=1, keepdims=True)
    n_chunks = jnp.floor((counts + (MOE_CHUNK - 1)) * (1.0 / MOE_CHUNK))
    starts = []
    start = jnp.zeros((1, 1), F32)
    for g in range(N_EGROUPS):
        starts.append(start)
        start = start + n_chunks[g:g + 1] * MOE_CHUNK
    pos = masks[0] * (starts[0] + ranks[0:1])
    for g in range(1, N_EGROUPS):
        pos = pos + masks[g] * (starts[g] + ranks[g:g + 1])
    perm = jnp.where(lax.broadcasted_iota(jnp.int32, (MOE_ROWS, tm), 0) == pos.astype(jnp.int32),
                     1.0, 0.0).astype(BF16)

    g4 = []
    for j in range(EXPERTS_PER_GROUP):
        r = masks[0] * gates[j]
        for g in range(1, N_EGROUPS):
            r = r + masks[g] * gates[EXPERTS_PER_GROUP * g + j]
        g4.append(r)
    g4t = jnp.concatenate(g4 + [jnp.zeros((LANES - EXPERTS_PER_GROUP, tm), F32)], axis=0).T
    src = jnp.concatenate([t_hi, g4t.astype(BF16)], axis=1)
    srt_scr[...] = _dot(perm, src).astype(BF16)
    y_scr[...] = jnp.zeros_like(y_scr)

    for g in range(N_EGROUPS):
        base = starts[g][0, 0].astype(jnp.int32)

        def chunk(j, carry, g=g, base=base):
            off = pl.multiple_of(base + j * MOE_CHUNK, MOE_CHUNK)
            xs = srt_scr[pl.ds(off, MOE_CHUNK), 0:D_MODEL]
            gs = srt_scr[pl.ds(off, MOE_CHUNK), D_MODEL:D_MODEL + LANES].astype(F32)
            u1 = _dot(xs, wg_ref[g])
            u2 = _dot(xs, wu_ref[g])
            he = u1 * _sigmoid(u1) * u2
            he = jnp.concatenate(
                [he[:, D_EXPERT * e:D_EXPERT * (e + 1)] * gs[:, e:e + 1] for e in range(EXPERTS_PER_GROUP)],
                axis=1)
            y_scr[pl.ds(off, MOE_CHUNK), :] = _dot(he.astype(BF16), wd_ref[g]).astype(BF16)
            return carry

        lax.fori_loop(0, n_chunks[g, 0].astype(jnp.int32), chunk, 0)

    y = x + mod[5:6] * _dot_tn(perm, y_scr[...])
    if final:
        y = _rms_rows(y, gfin_ref[...])
    out_ref[...] = y


def _moe(x, mod_l, g_ffn, wr, bias, wg, wu, wd, g_final, final, attn=None):
    B, S, _ = x.shape
    tm = MOE_TILE
    const = lambda b, i: (0, 0)
    wspec = pl.BlockSpec((N_EGROUPS, D_MODEL, D_MODEL), lambda b, i: (0, 0, 0),
                         pipeline_mode=pl.Buffered(1))
    attn_specs = [] if attn is None else [
        pl.BlockSpec((None, N_HEADS, D_V, tm), lambda b, i: (b, 0, 0, i)),
        pl.BlockSpec((D_MODEL, D_MODEL), const, pipeline_mode=pl.Buffered(1)),
    ]
    return pl.pallas_call(
        functools.partial(_moe_kernel, final=final, after_attention=attn is not None),
        out_shape=jax.ShapeDtypeStruct(x.shape, F32),
        grid=(B, S // tm),
        in_specs=attn_specs + [
            pl.BlockSpec((None, tm, D_MODEL), lambda b, i: (b, i, 0)),
            pl.BlockSpec((None, 6, D_MODEL), lambda b, i: (b, 0, 0)),
            pl.BlockSpec((1, D_MODEL), const),
            pl.BlockSpec((2 * N_EXPERTS, D_MODEL), const),
            pl.BlockSpec((N_EXPERTS, 1), const),
            wspec, wspec, wspec,
            pl.BlockSpec((1, D_MODEL), const),
        ],
        out_specs=pl.BlockSpec((None, tm, D_MODEL), lambda b, i: (b, i, 0)),
        scratch_shapes=[
            pltpu.VMEM((MOE_ROWS, D_MODEL + LANES), BF16),
            pltpu.VMEM((MOE_ROWS, D_MODEL), BF16),
        ],
        compiler_params=_cparams(("parallel", "parallel")),
        name="moe",
    )(*(attn or ()), x, mod_l, g_ffn, wr, bias, wg, wu, wd, g_final)


def _rope_tables(S):
    pos = jnp.arange(S, dtype=jnp.int32)
    rows = (pos // GRID_W).astype(F32)[None, :]
    cols = (pos % GRID_W).astype(F32)[None, :]
    tabs = []
    for half in (HEAD_DIM // 2, D_ROPE // 2):
        inv = (ROPE_THETA ** (-jnp.arange(0, half, 2, dtype=F32) / half))[:, None]
        for p in (rows, cols):
            ang = inv * p
            tabs += [jnp.cos(ang), jnp.sin(ang)]
    return jnp.concatenate(tabs, axis=0)


def _group_experts(w, transpose_cols):
    e, a, b = w.shape
    w = w.reshape(N_EGROUPS, EXPERTS_PER_GROUP, a, b)
    if transpose_cols:
        return w.transpose(0, 2, 1, 3).reshape(N_EGROUPS, a, EXPERTS_PER_GROUP * b).astype(BF16)
    return w.reshape(N_EGROUPS, EXPERTS_PER_GROUP * a, b).astype(BF16)


def _trunk(x, mod, p):
    B, S, _ = x.shape
    tab = _rope_tables(S)
    for l in range(DEPTH):
        mod_l = mod[l]
        g_mix = p["g_mix"][l][None, :]
        if l % 2 == 0:
            i = l // 2
            qt, k, vt = _pre_attn(x, mod_l, g_mix, tab, p["wint"][i], p["g_qa"][i][:, None],
                                  p["g_ka"][i][:, None], p["g_cq"][i][:, None], p["wuqt"][i],
                                  p["g_ckv"][i][:, None], p["wukvt"][i])
            attn = (_attention(qt, k, vt), p["wo"][i])
        else:
            attn = None
            x = _fourier_layer(x, mod_l, g_mix, p["wc"][l // 2])
        x = _moe(x, mod_l, p["g_ffn"][l][None, :], p["wr"], p["bias"], p["wg"][l], p["wu"][l],
                 p["wd"][l], p["g_final"], final=(l == DEPTH - 1), attn=attn)
    return x


def kernel(x_prompt, x_sample, c_prompt, c_sample, w_ada, b_ada, g_mix, g_ffn, w_in, g_qa, g_ka, g_cq,
           w_uq, g_ckv, w_ukv, w_o_attn, w_fourier, w_router, router_bias, w_gate, w_up, w_down, g_final):
    nbp = x_prompt.shape[0]
    mod = _modulation(jnp.concatenate([c_prompt, c_sample], axis=0), w_ada, b_ada)
    mod = mod.reshape(DEPTH, -1, 6, D_MODEL)
    wrt = w_router.T
    wr_hi = wrt.astype(BF16)
    wr_lo = (wrt - wr_hi.astype(F32)).astype(BF16)
    p = {
        "g_mix": g_mix, "g_ffn": g_ffn, "g_qa": g_qa, "g_ka": g_ka, "g_cq": g_cq, "g_ckv": g_ckv,
        "wint": jnp.swapaxes(w_in, 1, 2).astype(BF16),
        "wuqt": jnp.swapaxes(w_uq, 1, 2).astype(BF16),
        "wukvt": jnp.swapaxes(w_ukv, 1, 2).astype(BF16),
        "wo": w_o_attn.astype(BF16),
        "wc": w_fourier.astype(BF16),
        "wr": jnp.concatenate([wr_hi, wr_lo], axis=0),
        "bias": router_bias[:, None],
        "wg": [_group_experts(w_gate[l], True) for l in range(DEPTH)],
        "wu": [_group_experts(w_up[l], True) for l in range(DEPTH)],
        "wd": [_group_experts(w_down[l], False) for l in range(DEPTH)],
        "g_final": g_final[None, :],
    }
    y_prompt = _trunk(x_prompt, mod[:, :nbp], p)
    y_sample = _trunk(x_sample, mod[:, nbp:], p)
    return (y_prompt, y_sample)
```

```python
import functools
import math

import numpy as np
import jax
import jax.numpy as jnp
from jax import lax
from jax.experimental import pallas as pl
from jax.experimental.pallas import tpu as pltpu

F32 = jnp.float32
BF16 = jnp.bfloat16
QK_DTYPE = jnp.float8_e4m3fn
QK_MAX = float(jnp.finfo(QK_DTYPE).max)
LANES = 128
SUBLANES = 8

D_MODEL = 1024
DEPTH = 4
GRID_W = 64
HEAD_DIM = 64
N_HEADS_A = 8
N_KV_A = 2
N_HEADS_B = 8
D_NOPE = 64
D_ROPE = 32
D_V = 64
D_CQ = 384
D_CKV = 256
N_FGROUPS = 4
FGROUP = D_MODEL // N_FGROUPS
ROPE_THETA = 10000.0
N_EXPERTS = 16
N_EGROUPS = 4
EXPERTS_PER_GROUP = 4
D_EXPERT = 256
EPS = 1e-6
IN_SIZES = (N_HEADS_A * HEAD_DIM, N_KV_A * HEAD_DIM, N_KV_A * HEAD_DIM, D_CQ, D_CKV, D_ROPE)
IN_WIDTH = sum(IN_SIZES)
IN_OFFS = tuple(int(v) for v in np.cumsum((0,) + IN_SIZES))

N_HEADS = N_HEADS_A + N_HEADS_B
N_KSLOTS = 1 + N_HEADS_B
N_VSLOTS = N_KV_A + N_HEADS_B
QK_PAD = LANES
LOG2E = 1.4426950408889634
SCALE_A = HEAD_DIM ** -0.5 * LOG2E
SCALE_B = (D_NOPE + D_ROPE) ** -0.5 * LOG2E
NEG_BIG = -1e30

FFT_N2 = 128
FFT_T = SUBLANES

TOKEN_TILE = 512
Q_TILE = 512
Q_TILES_PER_STEP = 4
ATTN_TRIP = 6
V_ROWS = D_V + 16
MOE_TILE = 512
MOE_CHUNK = 128
MOE_ROWS = MOE_TILE + (N_EGROUPS - 1) * MOE_CHUNK
VMEM_LIMIT = 56 * 1024 * 1024


def _cparams(sem):
    return pltpu.CompilerParams(dimension_semantics=sem, vmem_limit_bytes=VMEM_LIMIT)


def _to_qk(x):
    return jnp.clip(x, -QK_MAX, QK_MAX).astype(QK_DTYPE)


def _sigmoid(x):
    return 1.0 / (1.0 + jnp.exp(-x))


def _rms_rows(x, g):
    ms = jnp.mean(x * x, axis=-1, keepdims=True)
    return x * lax.rsqrt(ms + EPS) * g


def _rms_cols(xt, gcol):
    ms = jnp.mean(xt * xt, axis=0, keepdims=True)
    return xt * lax.rsqrt(ms + EPS) * gcol


def _dot(a, b):
    return jnp.dot(a, b, preferred_element_type=F32)


def _dot_nt(a, b):
    return lax.dot_general(a, b, (((1,), (1,)), ((), ())), preferred_element_type=F32)


def _dot_tn(a, b):
    return lax.dot_general(a, b, (((0,), (0,)), ((), ())), preferred_element_type=F32)


def _mod_kernel(c_ref, w_ref, b_ref, o_ref):
    c = c_ref[...]
    ca = c * _sigmoid(c)
    o_ref[...] = _dot(ca, w_ref[...]) + b_ref[...]


def _modulation(c_all, w_ada, b_ada):
    nb = c_all.shape[0]
    tn = 1536
    return pl.pallas_call(
        _mod_kernel,
        out_shape=jax.ShapeDtypeStruct((DEPTH, nb, 6 * D_MODEL), F32),
        grid=(DEPTH, 6 * D_MODEL // tn),
        in_specs=[
            pl.BlockSpec((nb, D_MODEL), lambda l, j: (0, 0)),
            pl.BlockSpec((None, D_MODEL, tn), lambda l, j: (l, 0, j)),
            pl.BlockSpec((None, 1, tn), lambda l, j: (l, 0, j)),
        ],
        out_specs=pl.BlockSpec((None, nb, tn), lambda l, j: (l, 0, j)),
        compiler_params=_cparams(("arbitrary", "arbitrary")),
        name="adaln_mod",
    )(c_all, w_ada, b_ada.reshape(DEPTH, 1, 6 * D_MODEL))


def _rope_t(xt, c, s):
    n = c.shape[0]
    x1, x2 = xt[:n], xt[n:]
    return jnp.concatenate([x1 * c - x2 * s, x1 * s + x2 * c], axis=0)


def _axial_t(xt, cr, sr, cc, sc):
    d2 = xt.shape[0] // 2
    return jnp.concatenate([_rope_t(xt[:d2], cr, sr), _rope_t(xt[d2:], cc, sc)], axis=0)


def _pre_attn_kernel(x_ref, mod_ref, gmix_ref, tab_ref, wint_ref, gqa_ref, gka_ref, gcq_ref,
                     wuqt_ref, gckv_ref, wukvt_ref, q_ref, k_ref, v_ref):
    x = x_ref[...]
    tm = x.shape[0]
    mod = mod_ref[...]
    h = _rms_rows(x, gmix_ref[...]) * (1.0 + mod[1:2]) + mod[0:1]
    zt = _dot_nt(wint_ref[...], h.astype(BF16))

    tab = tab_ref[...]
    ra = (tab[0:16], tab[16:32], tab[32:48], tab[48:64])
    rb = (tab[64:72], tab[72:80], tab[80:88], tab[88:96])
    zeros64 = jnp.zeros((HEAD_DIM, tm), F32)
    zeros32 = jnp.zeros((QK_PAD - D_NOPE - D_ROPE, tm), F32)

    o_q, o_k, o_v, o_cq, o_ckv, o_kr = IN_OFFS[:6]
    gqa = gqa_ref[...]
    for hh in range(N_HEADS_A):
        qh = _axial_t(_rms_cols(zt[o_q + HEAD_DIM * hh:o_q + HEAD_DIM * (hh + 1)], gqa), *ra) * SCALE_A
        parts = [qh, zeros64] if hh // (N_HEADS_A // N_KV_A) == 0 else [zeros64, qh]
        q_ref[hh] = _to_qk(jnp.concatenate(parts, axis=0))
    gka = gka_ref[...]
    kts = [jnp.concatenate(
        [_axial_t(_rms_cols(zt[o_k + HEAD_DIM * g:o_k + HEAD_DIM * (g + 1)], gka), *ra)
         for g in range(N_KV_A)], axis=0)]
    ones16 = jnp.ones((V_ROWS - D_V, tm), F32)
    for g in range(N_KV_A):
        v_ref[g] = jnp.concatenate(
            [zt[o_v + HEAD_DIM * g:o_v + HEAD_DIM * (g + 1)], ones16], axis=0).astype(BF16)

    cqn = _rms_cols(zt[o_cq:o_ckv], gcq_ref[...]).astype(BF16)
    qbt = _dot(wuqt_ref[...], cqn)
    dq = D_NOPE + D_ROPE
    for hh in range(N_HEADS_B):
        nope = qbt[dq * hh:dq * hh + D_NOPE]
        rp = _axial_t(qbt[dq * hh + D_NOPE:dq * (hh + 1)], *rb)
        q_ref[N_HEADS_A + hh] = _to_qk(jnp.concatenate([nope, rp, zeros32], axis=0) * SCALE_B)
    ckvn = _rms_cols(zt[o_ckv:o_kr], gckv_ref[...]).astype(BF16)
    kvt = _dot(wukvt_ref[...], ckvn)
    kr = _axial_t(zt[o_kr:o_kr + D_ROPE], *rb)
    dkv = D_NOPE + D_V
    for hh in range(N_HEADS_B):
        kts.append(jnp.concatenate([kvt[dkv * hh:dkv * hh + D_NOPE], kr, zeros32], axis=0))
        v_ref[N_KV_A + hh] = jnp.concatenate(
            [kvt[dkv * hh + D_NOPE:dkv * (hh + 1)], ones16], axis=0).astype(BF16)

    k_all = jnp.concatenate(kts, axis=0).T
    for i in range(N_KSLOTS):
        k_ref[i] = _to_qk(k_all[:, QK_PAD * i:QK_PAD * (i + 1)])


def _pre_attn(x, mod_l, g_mix, tab, wint, gqa, gka, gcq, wuqt, gckv, wukvt):
    B, S, _ = x.shape
    tm = TOKEN_TILE
    nt = S // tm
    const = lambda b, i: (0, 0)
    return pl.pallas_call(
        _pre_attn_kernel,
        out_shape=(
            jax.ShapeDtypeStruct((B, N_HEADS, QK_PAD, S), QK_DTYPE),
            jax.ShapeDtypeStruct((B, N_KSLOTS, S, QK_PAD), QK_DTYPE),
            jax.ShapeDtypeStruct((B, N_VSLOTS, nt, V_ROWS, tm), BF16),
        ),
        grid=(B, nt),
        in_specs=[
            pl.BlockSpec((None, tm, D_MODEL), lambda b, i: (b, i, 0)),
            pl.BlockSpec((None, 6, D_MODEL), lambda b, i: (b, 0, 0)),
            pl.BlockSpec((1, D_MODEL), const),
            pl.BlockSpec((96, tm), lambda b, i: (0, i)),
            pl.BlockSpec((IN_WIDTH, D_MODEL), const),
            pl.BlockSpec((HEAD_DIM, 1), const),
            pl.BlockSpec((HEAD_DIM, 1), const),
            pl.BlockSpec((D_CQ, 1), const),
            pl.BlockSpec((N_HEADS_B * (D_NOPE + D_ROPE), D_CQ), const),
            pl.BlockSpec((D_CKV, 1), const),
            pl.BlockSpec((N_HEADS_B * (D_NOPE + D_V), D_CKV), const),
        ],
        out_specs=(
            pl.BlockSpec((None, N_HEADS, QK_PAD, tm), lambda b, i: (b, 0, 0, i)),
            pl.BlockSpec((None, N_KSLOTS, tm, QK_PAD), lambda b, i: (b, 0, i, 0)),
            pl.BlockSpec((None, N_VSLOTS, None, V_ROWS, tm), lambda b, i: (b, 0, i, 0, 0)),
        ),
        compiler_params=_cparams(("parallel", "parallel")),
        name="pre_attn",
    )(x, mod_l, g_mix, tab, wint, gqa, gka, gcq, wuqt, gckv, wukvt)


def _attn_kernel(q_ref, k_ref, v_ref, o_ref, sa_scr, sb_scr, *, n_chunks, tk, tq, trip):
    bufs = (sa_scr, sb_scr)
    tiles = [q_ref[:, tq * t:tq * (t + 1)] for t in range(q_ref.shape[1] // tq)]

    def scores(qt, c, s_ref):
        off = c * tk if isinstance(c, int) else pl.multiple_of(c * tk, tk)
        s = _dot(k_ref[pl.ds(off, tk), :], qt)
        s_ref[...] = s
        return jnp.max(s, axis=0, keepdims=True)

    def update(c, s_ref, cm, m, acc):
        m_new = jnp.maximum(m, cm)
        p = jnp.exp2(s_ref[...] - m_new).astype(BF16)
        return m_new, acc * jnp.exp2(m - m_new) + _dot(v_ref[c], p)

    def steps(qt, c0, count, cm, m, acc, then=None):
        for u in range(count):
            if not isinstance(c0, int) or c0 + u + 1 < n_chunks:
                cm_next = scores(qt, c0 + u + 1, bufs[(u + 1) % 2])
            else:
                cm_next = None if then is None else scores(then, 0, bufs[0])
            m, acc = update(c0 + u, bufs[u % 2], cm, m, acc)
            cm = cm_next
        return cm, m, acc

    n_trips = (n_chunks - 2) // trip
    cm = scores(tiles[0], 0, sa_scr)
    for t, qt in enumerate(tiles):
        init = (cm, jnp.full((1, tq), NEG_BIG, F32), jnp.zeros((V_ROWS, tq), F32))
        carry = lax.fori_loop(0, n_trips, lambda i, c, qt=qt: steps(qt, i * trip, trip, *c), init)
        cm, _, acc = steps(qt, n_trips * trip, n_chunks - n_trips * trip, *carry,
                           then=tiles[t + 1] if t + 1 < len(tiles) else None)
        o_ref[:, tq * t:tq * (t + 1)] = (acc[:D_V] * (1.0 / acc[D_V:D_V + 1])).astype(o_ref.dtype)


def _attention(qt, k, vt):
    B, _, _, S = qt.shape
    n_chunks, tk = vt.shape[2], vt.shape[4]
    tq = Q_TILE
    tqb = Q_TILE * Q_TILES_PER_STEP
    kslot = lambda h: jnp.where(h < N_HEADS_A, 0, h - (N_HEADS_A - 1))
    vslot = lambda h: jnp.where(h < N_HEADS_A, h // (N_HEADS_A // N_KV_A), h - (N_HEADS_A - N_KV_A))
    return pl.pallas_call(
        functools.partial(_attn_kernel, n_chunks=n_chunks, tk=tk, tq=tq, trip=ATTN_TRIP),
        out_shape=jax.ShapeDtypeStruct((B, N_HEADS, D_V, S), BF16),
        grid=(B, N_HEADS, S // tqb),
        in_specs=[
            pl.BlockSpec((None, None, QK_PAD, tqb), lambda b, h, i: (b, h, 0, i)),
            pl.BlockSpec((None, None, S, QK_PAD), lambda b, h, i: (b, kslot(h), 0, 0)),
            pl.BlockSpec((None, None, n_chunks, V_ROWS, tk), lambda b, h, i: (b, vslot(h), 0, 0, 0)),
        ],
        out_specs=pl.BlockSpec((None, None, D_V, tqb), lambda b, h, i: (b, h, 0, i)),
        scratch_shapes=[pltpu.VMEM((tk, tq), F32), pltpu.VMEM((tk, tq), F32)],
        compiler_params=_cparams(("parallel", "parallel", "arbitrary")),
        name="attn_sweep",
    )(qt, k, vt)


def _f1_kernel(x_ref, mod_ref, gmix_ref, fc_ref, m1_ref, twr_ref, twi_ref, y_ref, z_scr):
    n1 = x_ref.shape[0]
    x = x_ref[...].reshape(n1 * FFT_T, D_MODEL)
    mod = mod_ref[...]
    h = _rms_rows(x, gmix_ref[...]) * (1.0 + mod[1:2]) + mod[0:1]
    hb = h.astype(BF16)
    fc = fc_ref[...]
    for g in range(N_FGROUPS):
        zg = _dot(hb[:, FGROUP * g:FGROUP * (g + 1)], fc)
        z_scr[:, :, FGROUP * g:FGROUP * (g + 1)] = zg[:, :FGROUP].reshape(n1, FFT_T, FGROUP)
        z_scr[:, :, D_MODEL + FGROUP * g:D_MODEL + FGROUP * (g + 1)] = zg[:, FGROUP:].reshape(n1, FFT_T, FGROUP)
    m1 = m1_ref[...]
    for j in range(FFT_T):
        zj = z_scr[:, j, :]
        st = jnp.concatenate([zj[:, :D_MODEL], zj[:, D_MODEL:]], axis=0).astype(BF16)
        y = _dot(m1, st)
        yr, yi = y[:n1], y[n1:]
        tr = jnp.tile(twr_ref[j], (1, D_MODEL // LANES))
        ti = jnp.tile(twi_ref[j], (1, D_MODEL // LANES))
        y_ref[:, j, 0:D_MODEL] = yr * tr - yi * ti
        y_ref[:, j, D_MODEL:2 * D_MODEL] = yr * ti + yi * tr


def _f2_kernel(y_ref, x_ref, mod_ref, cs_ref, wc_ref, o_ref, f_scr):
    cs = cs_ref[...]
    for j in range(FFT_T):
        yk = y_ref[j]
        st = jnp.concatenate([yk[:, :D_MODEL], yk[:, D_MODEL:]], axis=0).astype(BF16)
        f_scr[FFT_N2 * j:FFT_N2 * (j + 1), :] = _dot(cs, st).astype(BF16)
    m = _dot(f_scr[...], wc_ref[...])
    gt1 = mod_ref[...][2:3]
    for j in range(FFT_T):
        o_ref[:, j, :] = x_ref[:, j, :] + gt1 * m[FFT_N2 * j:FFT_N2 * (j + 1)]


def _dft_constants(S):
    n1 = S // FFT_N2
    k = np.arange(FGROUP)
    ang = 2.0 * np.pi * np.outer(k, k) / FGROUP
    fc = np.concatenate([np.cos(ang), -np.sin(ang)], axis=1) / math.sqrt(FGROUP)
    a = np.arange(n1)
    ang1 = 2.0 * np.pi * np.outer(a, a) / n1
    c1, s1 = np.cos(ang1), np.sin(ang1)
    m1 = np.block([[c1, s1], [-s1, c1]]) / math.sqrt(n1)
    b = np.arange(FFT_N2)
    ang2 = 2.0 * np.pi * np.outer(b, b) / FFT_N2
    cs = np.concatenate([np.cos(ang2), np.sin(ang2)], axis=1) / math.sqrt(FFT_N2)
    angt = 2.0 * np.pi * np.outer(b, a) / S
    twr = np.broadcast_to(np.cos(angt)[:, :, None], (FFT_N2, n1, LANES))
    twi = np.broadcast_to(-np.sin(angt)[:, :, None], (FFT_N2, n1, LANES))
    return (jnp.asarray(fc, BF16), jnp.asarray(m1, BF16), jnp.asarray(cs, BF16),
            jnp.asarray(twr, F32), jnp.asarray(twi, F32))


def _fourier_layer(x, mod_l, g_mix, wc):
    B, S, _ = x.shape
    n1 = S // FFT_N2
    fc, m1, cs, twr, twi = _dft_constants(S)
    const2 = lambda b, i: (0, 0)
    y = pl.pallas_call(
        _f1_kernel,
        out_shape=jax.ShapeDtypeStruct((B, n1, FFT_N2, 2 * D_MODEL), F32),
        grid=(B, FFT_N2 // FFT_T),
        in_specs=[
            pl.BlockSpec((None, n1, FFT_T, D_MODEL), lambda b, i: (b, 0, i, 0)),
            pl.BlockSpec((None, 6, D_MODEL), lambda b, i: (b, 0, 0)),
            pl.BlockSpec((1, D_MODEL), const2),
            pl.BlockSpec((FGROUP, 2 * FGROUP), const2),
            pl.BlockSpec((2 * n1, 2 * n1), const2),
            pl.BlockSpec((FFT_T, n1, LANES), lambda b, i: (i, 0, 0)),
            pl.BlockSpec((FFT_T, n1, LANES), lambda b, i: (i, 0, 0)),
        ],
        out_specs=pl.BlockSpec((None, n1, FFT_T, 2 * D_MODEL), lambda b, i: (b, 0, i, 0)),
        scratch_shapes=[pltpu.VMEM((n1, FFT_T, 2 * D_MODEL), F32)],
        compiler_params=_cparams(("parallel", "parallel")),
        name="fourier_stage1",
    )(x.reshape(B, n1, FFT_N2, D_MODEL), mod_l, g_mix, fc, m1, twr, twi)
    out = pl.pallas_call(
        _f2_kernel,
        out_shape=jax.ShapeDtypeStruct((B, FFT_N2, n1, D_MODEL), F32),
        grid=(B, n1 // FFT_T),
        in_specs=[
            pl.BlockSpec((None, FFT_T, FFT_N2, 2 * D_MODEL), lambda b, i: (b, i, 0, 0)),
            pl.BlockSpec((None, FFT_N2, FFT_T, D_MODEL), lambda b, i: (b, 0, i, 0)),
            pl.BlockSpec((None, 6, D_MODEL), lambda b, i: (b, 0, 0)),
            pl.BlockSpec((FFT_N2, 2 * FFT_N2), const2),
            pl.BlockSpec((D_MODEL, D_MODEL), const2),
        ],
        out_specs=pl.BlockSpec((None, FFT_N2, FFT_T, D_MODEL), lambda b, i: (b, 0, i, 0)),
        scratch_shapes=[pltpu.VMEM((FFT_T * FFT_N2, D_MODEL), BF16)],
        compiler_params=_cparams(("parallel", "parallel")),
        name="fourier_stage2",
    )(y, x.reshape(B, FFT_N2, n1, D_MODEL), mod_l, cs, wc)
    return out.reshape(B, S, D_MODEL)


def _route_t(s, sb):
    rows = [sb[e:e + 1] for e in range(N_EXPERTS)]
    gscore = []
    for g in range(N_EGROUPS):
        a, b, c, d = rows[4 * g:4 * g + 4]
        gscore.append(jnp.maximum(jnp.maximum(jnp.maximum(a + b, a + c), jnp.maximum(a + d, b + c)),
                                  jnp.maximum(b + d, c + d)))
    best = gscore[0]
    gsel = jnp.zeros_like(best)
    for g in range(1, N_EGROUPS):
        upd = gscore[g] > best
        gsel = jnp.where(upd, float(g), gsel)
        best = jnp.where(upd, gscore[g], best)
    picked = []
    for e in range(N_EXPERTS):
        g = e // EXPERTS_PER_GROUP
        rank = jnp.zeros_like(best)
        for j in range(4 * g, 4 * g + 4):
            if j == e:
                continue
            ahead = (rows[j] >= rows[e]) if j < e else (rows[j] > rows[e])
            rank = rank + jnp.where(ahead, 1.0, 0.0)
        sel = jnp.where(rank < 1.5, 1.0, 0.0) * jnp.where(gsel == float(g), 1.0, 0.0)
        picked.append(sel * s[e:e + 1])
    total = picked[0]
    for e in range(1, N_EXPERTS):
        total = total + picked[e]
    inv = 1.0 / total
    return [p * inv for p in picked], gsel


def _moe_kernel(*refs, final, after_attention):
    if after_attention:
        o_ref, wo_ref, *refs = refs
    (x_ref, mod_ref, gffn_ref, wr_ref, bias_ref, earlier_ref, wg_ref, wu_ref, wd_ref, gfin_ref,
     out_ref, srt_scr, y_scr) = refs
    tm = x_ref.shape[0]
    x = x_ref[...]
    mod = mod_ref[...]
    if after_attention:
        ot = o_ref[...].reshape(N_HEADS * D_V, tm)
        x = x + mod[2:3] * _dot_tn(ot, wo_ref[...])
    t = _rms_rows(x, gffn_ref[...]) * (1.0 + mod[4:5]) + mod[3:4]
    t_hi = t.astype(BF16)
    t_lo = (t - t_hi.astype(F32)).astype(BF16)
    wr = wr_ref[...]
    a = _dot_nt(wr, t_hi)
    b = _dot_nt(wr[0:N_EXPERTS], t_lo)
    logits = a[0:N_EXPERTS] + a[N_EXPERTS:2 * N_EXPERTS] + b
    s = _sigmoid(logits)
    gates, gsel = _route_t(s, s + bias_ref[...])

    masks = [jnp.where(gsel == float(g), 1.0, 0.0) for g in range(N_EGROUPS)]
    m8 = jnp.concatenate(masks + [jnp.zeros((SUBLANES - N_EGROUPS, tm), F32)], axis=0)
    ranks = _dot(m8.astype(BF16), earlier_ref[...])
    counts = jnp.sum(m8, axis=1, keepdims=True)
    n_chunks = jnp.floor((counts + (MOE_CHUNK - 1)) * (1.0 / MOE_CHUNK))
    starts = []
    start = jnp.zeros((1, 1), F32)
    for g in range(N_EGROUPS):
        starts.append(start)
        start = start + n_chunks[g:g + 1] * MOE_CHUNK
    pos = masks[0] * (starts[0] + ranks[0:1])
    for g in range(1, N_EGROUPS):
        pos = pos + masks[g] * (starts[g] + ranks[g:g + 1])
    perm = jnp.where(lax.broadcasted_iota(jnp.int32, (MOE_ROWS, tm), 0) == pos.astype(jnp.int32),
                     1.0, 0.0).astype(BF16)

    g4 = []
    for j in range(EXPERTS_PER_GROUP):
        r = masks[0] * gates[j]
        for g in range(1, N_EGROUPS):
            r = r + masks[g] * gates[EXPERTS_PER_GROUP * g + j]
        g4.append(r)
    g4t = jnp.concatenate(g4 + [jnp.zeros((LANES - EXPERTS_PER_GROUP, tm), F32)], axis=0).T
    src = jnp.concatenate([t_hi, g4t.astype(BF16)], axis=1)
    srt_scr[...] = _dot(perm, src).astype(BF16)
    y_scr[...] = jnp.zeros_like(y_scr)

    for g in range(N_EGROUPS):
        base = starts[g][0, 0].astype(jnp.int32)

        def chunk(j, carry, g=g, base=base):
            off = pl.multiple_of(base + j * MOE_CHUNK, MOE_CHUNK)
            xs = srt_scr[pl.ds(off, MOE_CHUNK), 0:D_MODEL]
            gs = srt_scr[pl.ds(off, MOE_CHUNK), D_MODEL:D_MODEL + LANES].astype(F32)
            he = []
            for e in range(EXPERTS_PER_GROUP):
                u1 = _dot(xs, wg_ref[EXPERTS_PER_GROUP * g + e])
                u2 = _dot(xs, wu_ref[EXPERTS_PER_GROUP * g + e])
                he.append(u1 * _sigmoid(u1) * u2 * gs[:, e:e + 1])
            he = jnp.concatenate(he, axis=1).astype(BF16)
            y_scr[pl.ds(off, MOE_CHUNK), :] = _dot(he, wd_ref[g]).astype(BF16)
            return carry

        lax.fori_loop(0, n_chunks[g, 0].astype(jnp.int32), chunk, 0)

    y = x + mod[5:6] * _dot_tn(perm, y_scr[...])
    if final:
        y = _rms_rows(y, gfin_ref[...])
    out_ref[...] = y


def _moe(x, mod_l, g_ffn, wr, bias, wg, wu, wd, g_final, final, attn=None):
    B, S, _ = x.shape
    tm = MOE_TILE
    const = lambda b, i: (0, 0)
    earlier = jnp.asarray(np.triu(np.ones((tm, tm), np.float32), k=1), BF16)
    wspec = pl.BlockSpec((N_EXPERTS, D_MODEL, D_EXPERT), lambda b, i: (0, 0, 0),
                         pipeline_mode=pl.Buffered(1))
    wdspec = pl.BlockSpec((N_EGROUPS, EXPERTS_PER_GROUP * D_EXPERT, D_MODEL), lambda b, i: (0, 0, 0),
                          pipeline_mode=pl.Buffered(1))
    attn_specs = [] if attn is None else [
        pl.BlockSpec((None, N_HEADS, D_V, tm), lambda b, i: (b, 0, 0, i)),
        pl.BlockSpec((D_MODEL, D_MODEL), const, pipeline_mode=pl.Buffered(1)),
    ]
    return pl.pallas_call(
        functools.partial(_moe_kernel, final=final, after_attention=attn is not None),
        out_shape=jax.ShapeDtypeStruct(x.shape, F32),
        grid=(B, S // tm),
        in_specs=attn_specs + [
            pl.BlockSpec((None, tm, D_MODEL), lambda b, i: (b, i, 0)),
            pl.BlockSpec((None, 6, D_MODEL), lambda b, i: (b, 0, 0)),
            pl.BlockSpec((1, D_MODEL), const),
            pl.BlockSpec((2 * N_EXPERTS, D_MODEL), const),
            pl.BlockSpec((N_EXPERTS, 1), const),
            pl.BlockSpec((tm, tm), const, pipeline_mode=pl.Buffered(1)),
            wspec, wspec, wdspec,
            pl.BlockSpec((1, D_MODEL), const),
        ],
        out_specs=pl.BlockSpec((None, tm, D_MODEL), lambda b, i: (b, i, 0)),
        scratch_shapes=[
            pltpu.VMEM((MOE_ROWS, D_MODEL + LANES), BF16),
            pltpu.VMEM((MOE_ROWS, D_MODEL), BF16),
        ],
        compiler_params=_cparams(("parallel", "parallel")),
        name="moe",
    )(*(attn or ()), x, mod_l, g_ffn, wr, bias, earlier, wg, wu, wd, g_final)


def _rope_tables(S):
    pos = jnp.arange(S, dtype=jnp.int32)
    rows = (pos // GRID_W).astype(F32)[None, :]
    cols = (pos % GRID_W).astype(F32)[None, :]
    tabs = []
    for half in (HEAD_DIM // 2, D_ROPE // 2):
        inv = (ROPE_THETA ** (-jnp.arange(0, half, 2, dtype=F32) / half))[:, None]
        for p in (rows, cols):
            ang = inv * p
            tabs += [jnp.cos(ang), jnp.sin(ang)]
    return jnp.concatenate(tabs, axis=0)


def _trunk(x, mod, p):
    B, S, _ = x.shape
    tab = _rope_tables(S)
    for l in range(DEPTH):
        mod_l = mod[l]
        g_mix = p["g_mix"][l][None, :]
        if l % 2 == 0:
            i = l // 2
            qt, k, vt = _pre_attn(x, mod_l, g_mix, tab, p["wint"][i], p["g_qa"][i][:, None],
                                  p["g_ka"][i][:, None], p["g_cq"][i][:, None], p["wuqt"][i],
                                  p["g_ckv"][i][:, None], p["wukvt"][i])
            attn = (_attention(qt, k, vt), p["wo"][i])
        else:
            attn = None
            x = _fourier_layer(x, mod_l, g_mix, p["wc"][l // 2])
        x = _moe(x, mod_l, p["g_ffn"][l][None, :], p["wr"], p["bias"], p["wg"][l], p["wu"][l],
                 p["wd"][l], p["g_final"], final=(l == DEPTH - 1), attn=attn)
    return x


def kernel(x_prompt, x_sample, c_prompt, c_sample, w_ada, b_ada, g_mix, g_ffn, w_in, g_qa, g_ka, g_cq,
           w_uq, g_ckv, w_ukv, w_o_attn, w_fourier, w_router, router_bias, w_gate, w_up, w_down, g_final):
    nbp = x_prompt.shape[0]
    mod = _modulation(jnp.concatenate([c_prompt, c_sample], axis=0), w_ada, b_ada)
    mod = mod.reshape(DEPTH, -1, 6, D_MODEL)
    wrt = w_router.T
    wr_hi = wrt.astype(BF16)
    wr_lo = (wrt - wr_hi.astype(F32)).astype(BF16)
    p = {
        "g_mix": g_mix, "g_ffn": g_ffn, "g_qa": g_qa, "g_ka": g_ka, "g_cq": g_cq, "g_ckv": g_ckv,
        "wint": jnp.swapaxes(w_in, 1, 2).astype(BF16),
        "wuqt": jnp.swapaxes(w_uq, 1, 2).astype(BF16),
        "wukvt": jnp.swapaxes(w_ukv, 1, 2).astype(BF16),
        "wo": w_o_attn.astype(BF16),
        "wc": w_fourier.astype(BF16),
        "wr": jnp.concatenate([wr_hi, wr_lo], axis=0),
        "bias": router_bias[:, None],
        "wg": w_gate.astype(BF16),
        "wu": w_up.astype(BF16),
        "wd": w_down.astype(BF16).reshape(DEPTH, N_EGROUPS, EXPERTS_PER_GROUP * D_EXPERT, D_MODEL),
        "g_final": g_final[None, :],
    }
    y_prompt = _trunk(x_prompt, mod[:, :nbp], p)
    y_sample = _trunk(x_sample, mod[:, nbp:], p)
    return (y_prompt, y_sample)
```

```python
import functools
import math

import numpy as np
import jax
import jax.numpy as jnp
from jax import lax
from jax.experimental import pallas as pl
from jax.experimental.pallas import tpu as pltpu

F32 = jnp.float32
BF16 = jnp.bfloat16
QK_DTYPE = jnp.float8_e4m3fn
QK_MAX = float(jnp.finfo(QK_DTYPE).max)
LANES = 128
SUBLANES = 8

D_MODEL = 1024
DEPTH = 4
GRID_W = 64
HEAD_DIM = 64
N_HEADS_A = 8
N_KV_A = 2
N_HEADS_B = 8
D_NOPE = 64
D_ROPE = 32
D_V = 64
D_CQ = 384
D_CKV = 256
N_FGROUPS = 4
FGROUP = D_MODEL // N_FGROUPS
ROPE_THETA = 10000.0
N_EXPERTS = 16
N_EGROUPS = 4
EXPERTS_PER_GROUP = 4
D_EXPERT = 256
EPS = 1e-6
IN_SIZES = (N_HEADS_A * HEAD_DIM, N_KV_A * HEAD_DIM, N_KV_A * HEAD_DIM, D_CQ, D_CKV, D_ROPE)
IN_WIDTH = sum(IN_SIZES)
IN_OFFS = tuple(int(v) for v in np.cumsum((0,) + IN_SIZES))

N_HEADS = N_HEADS_A + N_HEADS_B
N_KSLOTS = 1 + N_HEADS_B
N_VSLOTS = N_KV_A + N_HEADS_B
QK_PAD = LANES
LOG2E = 1.4426950408889634
SCALE_A = HEAD_DIM ** -0.5 * LOG2E
SCALE_B = (D_NOPE + D_ROPE) ** -0.5 * LOG2E
NEG_BIG = -1e30

FFT_N2 = 128
FFT_T = SUBLANES

TOKEN_TILE = 512
Q_TILE = 512
Q_TILES_PER_STEP = 4
ATTN_TRIP = 6
V_ROWS = D_V + 16
MOE_TILE = 512
MOE_CHUNK = 128
MOE_ROWS = MOE_TILE + (N_EGROUPS - 1) * MOE_CHUNK
VMEM_LIMIT = 56 * 1024 * 1024


def _cparams(sem):
    return pltpu.CompilerParams(dimension_semantics=sem, vmem_limit_bytes=VMEM_LIMIT)


def _to_qk(x):
    return jnp.clip(x, -QK_MAX, QK_MAX).astype(QK_DTYPE)


def _sigmoid(x):
    return 1.0 / (1.0 + jnp.exp(-x))


def _rms_rows(x, g):
    ms = jnp.mean(x * x, axis=-1, keepdims=True)
    return x * lax.rsqrt(ms + EPS) * g


def _adaln_rows(x, g, shift, scale):
    return _rms_rows(x, g * (1.0 + scale)) + shift


def _rms_cols(xt, gcol):
    ms = jnp.mean(xt * xt, axis=0, keepdims=True)
    return xt * lax.rsqrt(ms + EPS) * gcol


def _dot(a, b):
    return jnp.dot(a, b, preferred_element_type=F32)


def _dot_nt(a, b):
    return lax.dot_general(a, b, (((1,), (1,)), ((), ())), preferred_element_type=F32)


def _dot_tn(a, b):
    return lax.dot_general(a, b, (((0,), (0,)), ((), ())), preferred_element_type=F32)


def _mod_kernel(c_ref, w_ref, b_ref, o_ref):
    c = c_ref[...]
    ca = c * _sigmoid(c)
    o_ref[...] = _dot(ca, w_ref[...]) + b_ref[...]


def _modulation(c_all, w_ada, b_ada):
    nb = c_all.shape[0]
    tn = 1536
    return pl.pallas_call(
        _mod_kernel,
        out_shape=jax.ShapeDtypeStruct((DEPTH, nb, 6 * D_MODEL), F32),
        grid=(DEPTH, 6 * D_MODEL // tn),
        in_specs=[
            pl.BlockSpec((nb, D_MODEL), lambda l, j: (0, 0)),
            pl.BlockSpec((None, D_MODEL, tn), lambda l, j: (l, 0, j)),
            pl.BlockSpec((None, 1, tn), lambda l, j: (l, 0, j)),
        ],
        out_specs=pl.BlockSpec((None, nb, tn), lambda l, j: (l, 0, j)),
        compiler_params=_cparams(("arbitrary", "arbitrary")),
        name="adaln_mod",
    )(c_all, w_ada, b_ada.reshape(DEPTH, 1, 6 * D_MODEL))


def _rope_t(xt, c, s):
    n = c.shape[0]
    x1, x2 = xt[:n], xt[n:]
    return jnp.concatenate([x1 * c - x2 * s, x1 * s + x2 * c], axis=0)


def _axial_t(xt, cr, sr, cc, sc):
    d2 = xt.shape[0] // 2
    return jnp.concatenate([_rope_t(xt[:d2], cr, sr), _rope_t(xt[d2:], cc, sc)], axis=0)


def _pre_attn_kernel(x_ref, mod_ref, gmix_ref, tab_ref, wint_ref, gqa_ref, gka_ref, gcq_ref,
                     wuqt_ref, gckv_ref, wukvt_ref, q_ref, k_ref, v_ref):
    x = x_ref[...]
    tm = x.shape[0]
    mod = mod_ref[...]
    h = _adaln_rows(x, gmix_ref[...], mod[0:1], mod[1:2])
    zt = _dot_nt(wint_ref[...], h.astype(BF16))

    tab = tab_ref[...]
    ra = (tab[0:16], tab[16:32], tab[32:48], tab[48:64])
    rb = (tab[64:72], tab[72:80], tab[80:88], tab[88:96])
    zeros64 = jnp.zeros((HEAD_DIM, tm), F32)
    zeros32 = jnp.zeros((QK_PAD - D_NOPE - D_ROPE, tm), F32)

    o_q, o_k, o_v, o_cq, o_ckv, o_kr = IN_OFFS[:6]
    gqa = gqa_ref[...]
    for hh in range(N_HEADS_A):
        qh = _axial_t(_rms_cols(zt[o_q + HEAD_DIM * hh:o_q + HEAD_DIM * (hh + 1)], gqa), *ra) * SCALE_A
        parts = [qh, zeros64] if hh // (N_HEADS_A // N_KV_A) == 0 else [zeros64, qh]
        q_ref[hh] = _to_qk(jnp.concatenate(parts, axis=0))
    gka = gka_ref[...]
    kts = [jnp.concatenate(
        [_axial_t(_rms_cols(zt[o_k + HEAD_DIM * g:o_k + HEAD_DIM * (g + 1)], gka), *ra)
         for g in range(N_KV_A)], axis=0)]
    ones16 = jnp.ones((V_ROWS - D_V, tm), F32)
    for g in range(N_KV_A):
        v_ref[g] = jnp.concatenate(
            [zt[o_v + HEAD_DIM * g:o_v + HEAD_DIM * (g + 1)], ones16], axis=0).astype(BF16)

    cqn = _rms_cols(zt[o_cq:o_ckv], gcq_ref[...]).astype(BF16)
    qbt = _dot(wuqt_ref[...], cqn)
    dq = D_NOPE + D_ROPE
    for hh in range(N_HEADS_B):
        nope = qbt[dq * hh:dq * hh + D_NOPE]
        rp = _axial_t(qbt[dq * hh + D_NOPE:dq * (hh + 1)], *rb)
        q_ref[N_HEADS_A + hh] = _to_qk(jnp.concatenate([nope, rp, zeros32], axis=0) * SCALE_B)
    ckvn = _rms_cols(zt[o_ckv:o_kr], gckv_ref[...]).astype(BF16)
    kvt = _dot(wukvt_ref[...], ckvn)
    kr = _axial_t(zt[o_kr:o_kr + D_ROPE], *rb)
    dkv = D_NOPE + D_V
    for hh in range(N_HEADS_B):
        kts.append(jnp.concatenate([kvt[dkv * hh:dkv * hh + D_NOPE], kr, zeros32], axis=0))
        v_ref[N_KV_A + hh] = jnp.concatenate(
            [kvt[dkv * hh + D_NOPE:dkv * (hh + 1)], ones16], axis=0).astype(BF16)

    k_all = jnp.concatenate(kts, axis=0).T
    for i in range(N_KSLOTS):
        k_ref[i] = _to_qk(k_all[:, QK_PAD * i:QK_PAD * (i + 1)])


def _pre_attn(x, mod_l, g_mix, tab, wint, gqa, gka, gcq, wuqt, gckv, wukvt):
    B, S, _ = x.shape
    tm = TOKEN_TILE
    nt = S // tm
    const = lambda b, i: (0, 0)
    return pl.pallas_call(
        _pre_attn_kernel,
        out_shape=(
            jax.ShapeDtypeStruct((B, N_HEADS, QK_PAD, S), QK_DTYPE),
            jax.ShapeDtypeStruct((B, N_KSLOTS, S, QK_PAD), QK_DTYPE),
            jax.ShapeDtypeStruct((B, N_VSLOTS, nt, V_ROWS, tm), BF16),
        ),
        grid=(B, nt),
        in_specs=[
            pl.BlockSpec((None, tm, D_MODEL), lambda b, i: (b, i, 0)),
            pl.BlockSpec((None, 6, D_MODEL), lambda b, i: (b, 0, 0)),
            pl.BlockSpec((1, D_MODEL), const),
            pl.BlockSpec((96, tm), lambda b, i: (0, i)),
            pl.BlockSpec((IN_WIDTH, D_MODEL), const),
            pl.BlockSpec((HEAD_DIM, 1), const),
            pl.BlockSpec((HEAD_DIM, 1), const),
            pl.BlockSpec((D_CQ, 1), const),
            pl.BlockSpec((N_HEADS_B * (D_NOPE + D_ROPE), D_CQ), const),
            pl.BlockSpec((D_CKV, 1), const),
            pl.BlockSpec((N_HEADS_B * (D_NOPE + D_V), D_CKV), const),
        ],
        out_specs=(
            pl.BlockSpec((None, N_HEADS, QK_PAD, tm), lambda b, i: (b, 0, 0, i)),
            pl.BlockSpec((None, N_KSLOTS, tm, QK_PAD), lambda b, i: (b, 0, i, 0)),
            pl.BlockSpec((None, N_VSLOTS, None, V_ROWS, tm), lambda b, i: (b, 0, i, 0, 0)),
        ),
        compiler_params=_cparams(("parallel", "parallel")),
        name="pre_attn",
    )(x, mod_l, g_mix, tab, wint, gqa, gka, gcq, wuqt, gckv, wukvt)


def _attn_kernel(q_ref, k_ref, v_ref, o_ref, sa_scr, sb_scr, *, n_chunks, tk, tq, trip):
    bufs = (sa_scr, sb_scr)
    tiles = [q_ref[:, tq * t:tq * (t + 1)] for t in range(q_ref.shape[1] // tq)]

    def scores(qt, c, s_ref):
        off = c * tk if isinstance(c, int) else pl.multiple_of(c * tk, tk)
        s = _dot(k_ref[pl.ds(off, tk), :], qt)
        s_ref[...] = s
        return jnp.max(s, axis=0, keepdims=True)

    def update(c, s_ref, cm, m, acc):
        m_new = jnp.maximum(m, cm)
        p = jnp.exp2(s_ref[...] - m_new).astype(BF16)
        return m_new, acc * jnp.exp2(m - m_new) + _dot(v_ref[c], p)

    def steps(qt, c0, count, cm, m, acc, then=None):
        for u in range(count):
            if not isinstance(c0, int) or c0 + u + 1 < n_chunks:
                cm_next = scores(qt, c0 + u + 1, bufs[(u + 1) % 2])
            else:
                cm_next = None if then is None else scores(then, 0, bufs[0])
            m, acc = update(c0 + u, bufs[u % 2], cm, m, acc)
            cm = cm_next
        return cm, m, acc

    n_trips = (n_chunks - 2) // trip
    cm = scores(tiles[0], 0, sa_scr)
    for t, qt in enumerate(tiles):
        init = (cm, jnp.full((1, tq), NEG_BIG, F32), jnp.zeros((V_ROWS, tq), F32))
        carry = lax.fori_loop(0, n_trips, lambda i, c, qt=qt: steps(qt, i * trip, trip, *c), init)
        cm, _, acc = steps(qt, n_trips * trip, n_chunks - n_trips * trip, *carry,
                           then=tiles[t + 1] if t + 1 < len(tiles) else None)
        o_ref[:, tq * t:tq * (t + 1)] = (acc[:D_V] * (1.0 / acc[D_V:D_V + 1])).astype(o_ref.dtype)


def _attention(qt, k, vt):
    B, _, _, S = qt.shape
    n_chunks, tk = vt.shape[2], vt.shape[4]
    tq = Q_TILE
    tqb = Q_TILE * Q_TILES_PER_STEP
    kslot = lambda h: jnp.where(h < N_HEADS_A, 0, h - (N_HEADS_A - 1))
    vslot = lambda h: jnp.where(h < N_HEADS_A, h // (N_HEADS_A // N_KV_A), h - (N_HEADS_A - N_KV_A))
    return pl.pallas_call(
        functools.partial(_attn_kernel, n_chunks=n_chunks, tk=tk, tq=tq, trip=ATTN_TRIP),
        out_shape=jax.ShapeDtypeStruct((B, N_HEADS, D_V, S), BF16),
        grid=(B, N_HEADS, S // tqb),
        in_specs=[
            pl.BlockSpec((None, None, QK_PAD, tqb), lambda b, h, i: (b, h, 0, i)),
            pl.BlockSpec((None, None, S, QK_PAD), lambda b, h, i: (b, kslot(h), 0, 0)),
            pl.BlockSpec((None, None, n_chunks, V_ROWS, tk), lambda b, h, i: (b, vslot(h), 0, 0, 0)),
        ],
        out_specs=pl.BlockSpec((None, None, D_V, tqb), lambda b, h, i: (b, h, 0, i)),
        scratch_shapes=[pltpu.VMEM((tk, tq), F32), pltpu.VMEM((tk, tq), F32)],
        compiler_params=_cparams(("parallel", "parallel", "arbitrary")),
        name="attn_sweep",
    )(qt, k, vt)


def _f1_kernel(x_ref, mod_ref, gmix_ref, fc_ref, m1_ref, y_ref, z_scr):
    n1 = x_ref.shape[0]
    x = x_ref[...].reshape(n1 * FFT_T, D_MODEL)
    mod = mod_ref[...]
    h = _adaln_rows(x, gmix_ref[...], mod[0:1], mod[1:2])
    hb = h.astype(BF16)
    fc = fc_ref[...]
    for g in range(N_FGROUPS):
        zg = _dot(hb[:, FGROUP * g:FGROUP * (g + 1)], fc)
        z_scr[:, :, FGROUP * g:FGROUP * (g + 1)] = zg[:, :FGROUP].reshape(n1, FFT_T, FGROUP)
        z_scr[:, :, D_MODEL + FGROUP * g:D_MODEL + FGROUP * (g + 1)] = zg[:, FGROUP:].reshape(n1, FFT_T, FGROUP)
    m1 = m1_ref[...]
    for j in range(FFT_T):
        zj = z_scr[:, j, :]
        st = jnp.concatenate([zj[:, :D_MODEL], zj[:, D_MODEL:]], axis=0).astype(BF16)
        y = _dot(m1, st)
        y_ref[:, j, 0:D_MODEL] = y[:n1]
        y_ref[:, j, D_MODEL:2 * D_MODEL] = y[n1:]


def _f2_kernel(y_ref, x_ref, mod_ref, cs_ref, wc_ref, o_ref, f_scr):
    for j in range(FFT_T):
        yk = y_ref[j]
        st = jnp.concatenate([yk[:, :D_MODEL], yk[:, D_MODEL:]], axis=0).astype(BF16)
        f_scr[FFT_N2 * j:FFT_N2 * (j + 1), :] = _dot(cs_ref[j], st).astype(BF16)
    m = _dot(f_scr[...], wc_ref[...])
    gt1 = mod_ref[...][2:3]
    for j in range(FFT_T):
        o_ref[:, j, :] = x_ref[:, j, :] + gt1 * m[FFT_N2 * j:FFT_N2 * (j + 1)]


def _dft_constants(S):
    n1 = S // FFT_N2
    k = np.arange(FGROUP)
    ang = 2.0 * np.pi * np.outer(k, k) / FGROUP
    fc = np.concatenate([np.cos(ang), -np.sin(ang)], axis=1) / math.sqrt(FGROUP)
    a = np.arange(n1)
    ang1 = 2.0 * np.pi * np.outer(a, a) / n1
    c1, s1 = np.cos(ang1), np.sin(ang1)
    m1 = np.block([[c1, s1], [-s1, c1]]) / math.sqrt(n1)
    b = np.arange(FFT_N2)
    ang2 = 2.0 * np.pi * np.outer(b, b) / FFT_N2
    c2, s2 = np.cos(ang2), np.sin(ang2)
    angt = 2.0 * np.pi * np.outer(a, b) / S
    tr, ti = np.cos(angt)[:, None, :], -np.sin(angt)[:, None, :]
    cs = np.concatenate([c2 * tr + s2 * ti, s2 * tr - c2 * ti], axis=2) / math.sqrt(FFT_N2)
    return jnp.asarray(fc, BF16), jnp.asarray(m1, BF16), jnp.asarray(cs, BF16)


def _fourier_layer(x, mod_l, g_mix, wc):
    B, S, _ = x.shape
    n1 = S // FFT_N2
    fc, m1, cs = _dft_constants(S)
    const2 = lambda b, i: (0, 0)
    y = pl.pallas_call(
        _f1_kernel,
        out_shape=jax.ShapeDtypeStruct((B, n1, FFT_N2, 2 * D_MODEL), F32),
        grid=(B, FFT_N2 // FFT_T),
        in_specs=[
            pl.BlockSpec((None, n1, FFT_T, D_MODEL), lambda b, i: (b, 0, i, 0)),
            pl.BlockSpec((None, 6, D_MODEL), lambda b, i: (b, 0, 0)),
            pl.BlockSpec((1, D_MODEL), const2),
            pl.BlockSpec((FGROUP, 2 * FGROUP), const2),
            pl.BlockSpec((2 * n1, 2 * n1), const2),
        ],
        out_specs=pl.BlockSpec((None, n1, FFT_T, 2 * D_MODEL), lambda b, i: (b, 0, i, 0)),
        scratch_shapes=[pltpu.VMEM((n1, FFT_T, 2 * D_MODEL), F32)],
        compiler_params=_cparams(("parallel", "parallel")),
        name="fourier_stage1",
    )(x.reshape(B, n1, FFT_N2, D_MODEL), mod_l, g_mix, fc, m1)
    out = pl.pallas_call(
        _f2_kernel,
        out_shape=jax.ShapeDtypeStruct((B, FFT_N2, n1, D_MODEL), F32),
        grid=(B, n1 // FFT_T),
        in_specs=[
            pl.BlockSpec((None, FFT_T, FFT_N2, 2 * D_MODEL), lambda b, i: (b, i, 0, 0)),
            pl.BlockSpec((None, FFT_N2, FFT_T, D_MODEL), lambda b, i: (b, 0, i, 0)),
            pl.BlockSpec((None, 6, D_MODEL), lambda b, i: (b, 0, 0)),
            pl.BlockSpec((FFT_T, FFT_N2, 2 * FFT_N2), lambda b, i: (i, 0, 0)),
            pl.BlockSpec((D_MODEL, D_MODEL), const2),
        ],
        out_specs=pl.BlockSpec((None, FFT_N2, FFT_T, D_MODEL), lambda b, i: (b, 0, i, 0)),
        scratch_shapes=[pltpu.VMEM((FFT_T * FFT_N2, D_MODEL), BF16)],
        compiler_params=_cparams(("parallel", "parallel")),
        name="fourier_stage2",
    )(y, x.reshape(B, FFT_N2, n1, D_MODEL), mod_l, cs, wc)
    return out.reshape(B, S, D_MODEL)


def _route_t(s, sb):
    rows = [sb[e:e + 1] for e in range(N_EXPERTS)]
    gscore = []
    for g in range(N_EGROUPS):
        a, b, c, d = rows[4 * g:4 * g + 4]
        gscore.append(jnp.maximum(jnp.maximum(jnp.maximum(a + b, a + c), jnp.maximum(a + d, b + c)),
                                  jnp.maximum(b + d, c + d)))
    best = gscore[0]
    gsel = jnp.zeros_like(best)
    for g in range(1, N_EGROUPS):
        upd = gscore[g] > best
        gsel = jnp.where(upd, float(g), gsel)
        best = jnp.where(upd, gscore[g], best)
    picked = []
    for e in range(N_EXPERTS):
        g = e // EXPERTS_PER_GROUP
        rank = jnp.zeros_like(best)
        for j in range(4 * g, 4 * g + 4):
            if j == e:
                continue
            ahead = (rows[j] >= rows[e]) if j < e else (rows[j] > rows[e])
            rank = rank + jnp.where(ahead, 1.0, 0.0)
        sel = jnp.where(rank < 1.5, 1.0, 0.0) * jnp.where(gsel == float(g), 1.0, 0.0)
        picked.append(sel * s[e:e + 1])
    total = picked[0]
    for e in range(1, N_EXPERTS):
        total = total + picked[e]
    inv = 1.0 / total
    return [p * inv for p in picked], gsel


def _moe_kernel(*refs, final, after_attention):
    if after_attention:
        o_ref, wo_ref, *refs = refs
    (x_ref, mod_ref, gffn_ref, wr_ref, bias_ref, earlier_ref, wg_ref, wu_ref, wd_ref, gfin_ref,
     out_ref, srt_scr, y_scr) = refs
    tm = x_ref.shape[0]
    x = x_ref[...]
    mod = mod_ref[...]
    if after_attention:
        ot = o_ref[...].reshape(N_HEADS * D_V, tm)
        x = x + mod[2:3] * _dot_tn(ot, wo_ref[...])
    t = _adaln_rows(x, gffn_ref[...], mod[3:4], mod[4:5])
    t_hi = t.astype(BF16)
    t_lo = (t - t_hi.astype(F32)).astype(BF16)
    wr = wr_ref[...]
    a = _dot_nt(wr, t_hi)
    b = _dot_nt(wr[0:N_EXPERTS], t_lo)
    logits = a[0:N_EXPERTS] + a[N_EXPERTS:2 * N_EXPERTS] + b
    s = _sigmoid(logits)
    gates, gsel = _route_t(s, s + bias_ref[...])

    masks = [jnp.where(gsel == float(g), 1.0, 0.0) for g in range(N_EGROUPS)]
    m8 = jnp.concatenate(masks + [jnp.zeros((SUBLANES - N_EGROUPS, tm), F32)], axis=0)
    ranks = _dot(m8.astype(BF16), earlier_ref[...])
    counts = jnp.sum(m8, axis=1, keepdims=True)
    n_chunks = jnp.floor((counts + (MOE_CHUNK - 1)) * (1.0 / MOE_CHUNK))
    starts = []
    start = jnp.zeros((1, 1), F32)
    for g in range(N_EGROUPS):
        starts.append(start)
        start = start + n_chunks[g:g + 1] * MOE_CHUNK
    pos = masks[0] * (starts[0] + ranks[0:1])
    for g in range(1, N_EGROUPS):
        pos = pos + masks[g] * (starts[g] + ranks[g:g + 1])
    perm = jnp.where(lax.broadcasted_iota(jnp.int32, (MOE_ROWS, tm), 0) == pos.astype(jnp.int32),
                     1.0, 0.0).astype(BF16)

    g4 = []
    for j in range(EXPERTS_PER_GROUP):
        r = masks[0] * gates[j]
        for g in range(1, N_EGROUPS):
            r = r + masks[g] * gates[EXPERTS_PER_GROUP * g + j]
        g4.append(r)
    g4t = jnp.concatenate(g4 + [jnp.zeros((LANES - EXPERTS_PER_GROUP, tm), F32)], axis=0).T
    src = jnp.concatenate([t_hi, g4t.astype(BF16)], axis=1)
    srt_scr[...] = _dot(perm, src).astype(BF16)
    y_scr[...] = jnp.zeros_like(y_scr)

    for g in range(N_EGROUPS):
        base = starts[g][0, 0].astype(jnp.int32)

        def chunk(j, carry, g=g, base=base):
            off = pl.multiple_of(base + j * MOE_CHUNK, MOE_CHUNK)
            xs = srt_scr[pl.ds(off, MOE_CHUNK), 0:D_MODEL]
            gs = srt_scr[pl.ds(off, MOE_CHUNK), D_MODEL:D_MODEL + LANES].astype(F32)
            he = []
            for e in range(EXPERTS_PER_GROUP):
                u1 = _dot(xs, wg_ref[EXPERTS_PER_GROUP * g + e])
                u2 = _dot(xs, wu_ref[EXPERTS_PER_GROUP * g + e])
                he.append(u1 * _sigmoid(u1) * u2 * gs[:, e:e + 1])
            he = jnp.concatenate(he, axis=1).astype(BF16)
            y_scr[pl.ds(off, MOE_CHUNK), :] = _dot(he, wd_ref[g]).astype(BF16)
            return carry

        lax.fori_loop(0, n_chunks[g, 0].astype(jnp.int32), chunk, 0)

    y = x + mod[5:6] * _dot_tn(perm, y_scr[...])
    if final:
        y = _rms_rows(y, gfin_ref[...])
    out_ref[...] = y


def _moe(x, mod_l, g_ffn, wr, bias, wg, wu, wd, g_final, final, attn=None):
    B, S, _ = x.shape
    tm = MOE_TILE
    const = lambda b, i: (0, 0)
    earlier = jnp.asarray(np.triu(np.ones((tm, tm), np.float32), k=1), BF16)
    wspec = pl.BlockSpec((N_EXPERTS, D_MODEL, D_EXPERT), lambda b, i: (0, 0, 0),
                         pipeline_mode=pl.Buffered(1))
    wdspec = pl.BlockSpec((N_EGROUPS, EXPERTS_PER_GROUP * D_EXPERT, D_MODEL), lambda b, i: (0, 0, 0),
                          pipeline_mode=pl.Buffered(1))
    attn_specs = [] if attn is None else [
        pl.BlockSpec((None, N_HEADS, D_V, tm), lambda b, i: (b, 0, 0, i)),
        pl.BlockSpec((D_MODEL, D_MODEL), const, pipeline_mode=pl.Buffered(1)),
    ]
    return pl.pallas_call(
        functools.partial(_moe_kernel, final=final, after_attention=attn is not None),
        out_shape=jax.ShapeDtypeStruct(x.shape, F32),
        grid=(B, S // tm),
        in_specs=attn_specs + [
            pl.BlockSpec((None, tm, D_MODEL), lambda b, i: (b, i, 0)),
            pl.BlockSpec((None, 6, D_MODEL), lambda b, i: (b, 0, 0)),
            pl.BlockSpec((1, D_MODEL), const),
            pl.BlockSpec((2 * N_EXPERTS, D_MODEL), const),
            pl.BlockSpec((N_EXPERTS, 1), const),
            pl.BlockSpec((tm, tm), const, pipeline_mode=pl.Buffered(1)),
            wspec, wspec, wdspec,
            pl.BlockSpec((1, D_MODEL), const),
        ],
        out_specs=pl.BlockSpec((None, tm, D_MODEL), lambda b, i: (b, i, 0)),
        scratch_shapes=[
            pltpu.VMEM((MOE_ROWS, D_MODEL + LANES), BF16),
            pltpu.VMEM((MOE_ROWS, D_MODEL), BF16),
        ],
        compiler_params=_cparams(("parallel", "parallel")),
        name="moe",
    )(*(attn or ()), x, mod_l, g_ffn, wr, bias, earlier, wg, wu, wd, g_final)


def _rope_tables(S):
    pos = jnp.arange(S, dtype=jnp.int32)
    rows = (pos // GRID_W).astype(F32)[None, :]
    cols = (pos % GRID_W).astype(F32)[None, :]
    tabs = []
    for half in (HEAD_DIM // 2, D_ROPE // 2):
        inv = (ROPE_THETA ** (-jnp.arange(0, half, 2, dtype=F32) / half))[:, None]
        for p in (rows, cols):
            ang = inv * p
            tabs += [jnp.cos(ang), jnp.sin(ang)]
    return jnp.concatenate(tabs, axis=0)


def _trunk(x, mod, p):
    B, S, _ = x.shape
    tab = _rope_tables(S)
    for l in range(DEPTH):
        mod_l = mod[l]
        g_mix = p["g_mix"][l][None, :]
        if l % 2 == 0:
            i = l // 2
            qt, k, vt = _pre_attn(x, mod_l, g_mix, tab, p["wint"][i], p["g_qa"][i][:, None],
                                  p["g_ka"][i][:, None], p["g_cq"][i][:, None], p["wuqt"][i],
                                  p["g_ckv"][i][:, None], p["wukvt"][i])
            attn = (_attention(qt, k, vt), p["wo"][i])
        else:
            attn = None
            x = _fourier_layer(x, mod_l, g_mix, p["wc"][l // 2])
        x = _moe(x, mod_l, p["g_ffn"][l][None, :], p["wr"], p["bias"], p["wg"][l], p["wu"][l],
                 p["wd"][l], p["g_final"], final=(l == DEPTH - 1), attn=attn)
    return x


def kernel(x_prompt, x_sample, c_prompt, c_sample, w_ada, b_ada, g_mix, g_ffn, w_in, g_qa, g_ka, g_cq,
           w_uq, g_ckv, w_ukv, w_o_attn, w_fourier, w_router, router_bias, w_gate, w_up, w_down, g_final):
    nbp = x_prompt.shape[0]
    mod = _modulation(jnp.concatenate([c_prompt, c_sample], axis=0), w_ada, b_ada)
    mod = mod.reshape(DEPTH, -1, 6, D_MODEL)
    wrt = w_router.T
    wr_hi = wrt.astype(BF16)
    wr_lo = (wrt - wr_hi.astype(F32)).astype(BF16)
    p = {
        "g_mix": g_mix, "g_ffn": g_ffn, "g_qa": g_qa, "g_ka": g_ka, "g_cq": g_cq, "g_ckv": g_ckv,
        "wint": jnp.swapaxes(w_in, 1, 2).astype(BF16),
        "wuqt": jnp.swapaxes(w_uq, 1, 2).astype(BF16),
        "wukvt": jnp.swapaxes(w_ukv, 1, 2).astype(BF16),
        "wo": w_o_attn.astype(BF16),
        "wc": w_fourier.astype(BF16),
        "wr": jnp.concatenate([wr_hi, wr_lo], axis=0),
        "bias": router_bias[:, None],
        "wg": w_gate.astype(BF16),
        "wu": w_up.astype(BF16),
        "wd": w_down.astype(BF16).reshape(DEPTH, N_EGROUPS, EXPERTS_PER_GROUP * D_EXPERT, D_MODEL),
        "g_final": g_final[None, :],
    }
    y_prompt = _trunk(x_prompt, mod[:, :nbp], p)
    y_sample = _trunk(x_sample, mod[:, nbp:], p)
    return (y_prompt, y_sample)
```

```python
import functools
import math

import numpy as np
import jax
import jax.numpy as jnp
from jax import lax
from jax.experimental import pallas as pl
from jax.experimental.pallas import tpu as pltpu

F32 = jnp.float32
BF16 = jnp.bfloat16
QK_DTYPE = jnp.float8_e4m3fn
QK_MAX = float(jnp.finfo(QK_DTYPE).max)
LANES = 128
SUBLANES = 8

D_MODEL = 1024
DEPTH = 4
GRID_W = 64
HEAD_DIM = 64
N_HEADS_A = 8
N_KV_A = 2
N_HEADS_B = 8
D_NOPE = 64
D_ROPE = 32
D_V = 64
D_CQ = 384
D_CKV = 256
N_FGROUPS = 4
FGROUP = D_MODEL // N_FGROUPS
ROPE_THETA = 10000.0
N_EXPERTS = 16
N_EGROUPS = 4
EXPERTS_PER_GROUP = 4
D_EXPERT = 256
EPS = 1e-6
IN_SIZES = (N_HEADS_A * HEAD_DIM, N_KV_A * HEAD_DIM, N_KV_A * HEAD_DIM, D_CQ, D_CKV, D_ROPE)
IN_WIDTH = sum(IN_SIZES)
IN_OFFS = tuple(int(v) for v in np.cumsum((0,) + IN_SIZES))

N_HEADS = N_HEADS_A + N_HEADS_B
N_KSLOTS = 1 + N_HEADS_B
N_VSLOTS = N_KV_A + N_HEADS_B
QK_PAD = LANES
LOG2E = 1.4426950408889634
SCALE_A = HEAD_DIM ** -0.5 * LOG2E
SCALE_B = (D_NOPE + D_ROPE) ** -0.5 * LOG2E
NEG_BIG = -1e30

FFT_N2 = 128
FFT_T = SUBLANES

TOKEN_TILE = 512
PRE_ATTN_TILES = 2
Q_TILE = 512
Q_TILES_PER_STEP = 4
ATTN_TRIP = 6
V_ROWS = D_V + 16
MOE_TILE = 512
MOE_CHUNK = 128
MOE_ROWS = MOE_TILE + (N_EGROUPS - 1) * MOE_CHUNK
VMEM_LIMIT = 56 * 1024 * 1024


def _cparams(sem):
    return pltpu.CompilerParams(dimension_semantics=sem, vmem_limit_bytes=VMEM_LIMIT)


def _to_qk(x):
    return jnp.clip(x, -QK_MAX, QK_MAX).astype(QK_DTYPE)


def _sigmoid(x):
    return 1.0 / (1.0 + jnp.exp(-x))


def _rms_rows(x, g):
    ms = jnp.mean(x * x, axis=-1, keepdims=True)
    return x * lax.rsqrt(ms + EPS) * g


def _adaln_rows(x, g, shift, scale):
    return _rms_rows(x, g * (1.0 + scale)) + shift


def _rms_cols(xt, gcol):
    ms = jnp.mean(xt * xt, axis=0, keepdims=True)
    return xt * lax.rsqrt(ms + EPS) * gcol


def _dot(a, b):
    return jnp.dot(a, b, preferred_element_type=F32)


def _dot_nt(a, b):
    return lax.dot_general(a, b, (((1,), (1,)), ((), ())), preferred_element_type=F32)


def _dot_tn(a, b):
    return lax.dot_general(a, b, (((0,), (0,)), ((), ())), preferred_element_type=F32)


def _mod_kernel(c_ref, w_ref, b_ref, o_ref):
    c = c_ref[...]
    ca = c * _sigmoid(c)
    o_ref[...] = _dot(ca, w_ref[...]) + b_ref[...]


def _modulation(c_all, w_ada, b_ada):
    nb = c_all.shape[0]
    tn = 1536
    return pl.pallas_call(
        _mod_kernel,
        out_shape=jax.ShapeDtypeStruct((DEPTH, nb, 6 * D_MODEL), F32),
        grid=(DEPTH, 6 * D_MODEL // tn),
        in_specs=[
            pl.BlockSpec((nb, D_MODEL), lambda l, j: (0, 0)),
            pl.BlockSpec((None, D_MODEL, tn), lambda l, j: (l, 0, j)),
            pl.BlockSpec((None, 1, tn), lambda l, j: (l, 0, j)),
        ],
        out_specs=pl.BlockSpec((None, nb, tn), lambda l, j: (l, 0, j)),
        compiler_params=_cparams(("arbitrary", "arbitrary")),
        name="adaln_mod",
    )(c_all, w_ada, b_ada.reshape(DEPTH, 1, 6 * D_MODEL))


def _rope_t(xt, c, s):
    n = c.shape[0]
    x1, x2 = xt[:n], xt[n:]
    return jnp.concatenate([x1 * c - x2 * s, x1 * s + x2 * c], axis=0)


def _axial_t(xt, cr, sr, cc, sc):
    d2 = xt.shape[0] // 2
    return jnp.concatenate([_rope_t(xt[:d2], cr, sr), _rope_t(xt[d2:], cc, sc)], axis=0)


def _pre_attn_kernel(x_ref, mod_ref, gmix_ref, tab_ref, wint_ref, gqa_ref, gka_ref, gcq_ref,
                     wuqt_ref, gckv_ref, wukvt_ref, q_ref, k_ref, v_ref):
    tm = TOKEN_TILE
    mod = mod_ref[...]
    o_q, o_k, o_v, o_cq, o_ckv, o_kr = IN_OFFS[:6]
    dq = D_NOPE + D_ROPE
    dkv = D_NOPE + D_V
    zeros64 = jnp.zeros((HEAD_DIM, tm), F32)
    zeros32 = jnp.zeros((QK_PAD - D_NOPE - D_ROPE, tm), F32)
    ones16 = jnp.ones((V_ROWS - D_V, tm), F32)

    def project(t):
        h = _adaln_rows(x_ref[tm * t:tm * (t + 1), :], gmix_ref[...], mod[0:1], mod[1:2])
        return _dot_nt(wint_ref[...], h.astype(BF16))

    def latents(zt):
        cqn = _rms_cols(zt[o_cq:o_ckv], gcq_ref[...]).astype(BF16)
        ckvn = _rms_cols(zt[o_ckv:o_kr], gckv_ref[...]).astype(BF16)
        return _dot(wuqt_ref[...], cqn), _dot(wukvt_ref[...], ckvn)

    def finish(t, zt, qbt, kvt):
        lanes = slice(tm * t, tm * (t + 1))
        tab = tab_ref[:, lanes]
        ra = (tab[0:16], tab[16:32], tab[32:48], tab[48:64])
        rb = (tab[64:72], tab[72:80], tab[80:88], tab[88:96])
        gqa = gqa_ref[...]
        for hh in range(N_HEADS_A):
            qh = _axial_t(_rms_cols(zt[o_q + HEAD_DIM * hh:o_q + HEAD_DIM * (hh + 1)], gqa), *ra) * SCALE_A
            parts = [qh, zeros64] if hh // (N_HEADS_A // N_KV_A) == 0 else [zeros64, qh]
            q_ref[hh, :, lanes] = _to_qk(jnp.concatenate(parts, axis=0))
        gka = gka_ref[...]
        kts = [jnp.concatenate(
            [_axial_t(_rms_cols(zt[o_k + HEAD_DIM * g:o_k + HEAD_DIM * (g + 1)], gka), *ra)
             for g in range(N_KV_A)], axis=0)]
        for g in range(N_KV_A):
            v_ref[g, t] = jnp.concatenate(
                [zt[o_v + HEAD_DIM * g:o_v + HEAD_DIM * (g + 1)], ones16], axis=0).astype(BF16)
        for hh in range(N_HEADS_B):
            nope = qbt[dq * hh:dq * hh + D_NOPE]
            rp = _axial_t(qbt[dq * hh + D_NOPE:dq * (hh + 1)], *rb)
            q_ref[N_HEADS_A + hh, :, lanes] = _to_qk(jnp.concatenate([nope, rp, zeros32], axis=0) * SCALE_B)
        kr = _axial_t(zt[o_kr:o_kr + D_ROPE], *rb)
        for hh in range(N_HEADS_B):
            kts.append(jnp.concatenate([kvt[dkv * hh:dkv * hh + D_NOPE], kr, zeros32], axis=0))
            v_ref[N_KV_A + hh, t] = jnp.concatenate(
                [kvt[dkv * hh + D_NOPE:dkv * (hh + 1)], ones16], axis=0).astype(BF16)
        k_all = jnp.concatenate(kts, axis=0).T
        for i in range(N_KSLOTS):
            k_ref[i, lanes, :] = _to_qk(k_all[:, QK_PAD * i:QK_PAD * (i + 1)])

    zts = [project(t) for t in range(PRE_ATTN_TILES)]
    lat = [latents(zt) for zt in zts]
    for t in range(PRE_ATTN_TILES):
        finish(t, zts[t], *lat[t])


def _pre_attn(x, mod_l, g_mix, tab, wint, gqa, gka, gcq, wuqt, gckv, wukvt):
    B, S, _ = x.shape
    tm = TOKEN_TILE
    nt = S // tm
    tb = tm * PRE_ATTN_TILES
    const = lambda b, i: (0, 0)
    return pl.pallas_call(
        _pre_attn_kernel,
        out_shape=(
            jax.ShapeDtypeStruct((B, N_HEADS, QK_PAD, S), QK_DTYPE),
            jax.ShapeDtypeStruct((B, N_KSLOTS, S, QK_PAD), QK_DTYPE),
            jax.ShapeDtypeStruct((B, N_VSLOTS, nt, V_ROWS, tm), BF16),
        ),
        grid=(B, S // tb),
        in_specs=[
            pl.BlockSpec((None, tb, D_MODEL), lambda b, i: (b, i, 0)),
            pl.BlockSpec((None, 6, D_MODEL), lambda b, i: (b, 0, 0)),
            pl.BlockSpec((1, D_MODEL), const),
            pl.BlockSpec((96, tb), lambda b, i: (0, i)),
            pl.BlockSpec((IN_WIDTH, D_MODEL), const),
            pl.BlockSpec((HEAD_DIM, 1), const),
            pl.BlockSpec((HEAD_DIM, 1), const),
            pl.BlockSpec((D_CQ, 1), const),
            pl.BlockSpec((N_HEADS_B * (D_NOPE + D_ROPE), D_CQ), const),
            pl.BlockSpec((D_CKV, 1), const),
            pl.BlockSpec((N_HEADS_B * (D_NOPE + D_V), D_CKV), const),
        ],
        out_specs=(
            pl.BlockSpec((None, N_HEADS, QK_PAD, tb), lambda b, i: (b, 0, 0, i)),
            pl.BlockSpec((None, N_KSLOTS, tb, QK_PAD), lambda b, i: (b, 0, i, 0)),
            pl.BlockSpec((None, N_VSLOTS, PRE_ATTN_TILES, V_ROWS, tm), lambda b, i: (b, 0, i, 0, 0)),
        ),
        compiler_params=_cparams(("parallel", "parallel")),
        name="pre_attn",
    )(x, mod_l, g_mix, tab, wint, gqa, gka, gcq, wuqt, gckv, wukvt)


def _attn_kernel(q_ref, k_ref, v_ref, o_ref, sa_scr, sb_scr, *, n_chunks, tk, tq, trip):
    bufs = (sa_scr, sb_scr)
    tiles = [q_ref[:, tq * t:tq * (t + 1)] for t in range(q_ref.shape[1] // tq)]

    def scores(qt, c, s_ref):
        off = c * tk if isinstance(c, int) else pl.multiple_of(c * tk, tk)
        s = _dot(k_ref[pl.ds(off, tk), :], qt)
        s_ref[...] = s
        return jnp.max(s, axis=0, keepdims=True)

    def update(c, s_ref, cm, m, acc):
        m_new = jnp.maximum(m, cm)
        p = jnp.exp2(s_ref[...] - m_new).astype(BF16)
        return m_new, acc * jnp.exp2(m - m_new) + _dot(v_ref[c], p)

    def steps(qt, c0, count, cm, m, acc, then=None):
        for u in range(count):
            if not isinstance(c0, int) or c0 + u + 1 < n_chunks:
                cm_next = scores(qt, c0 + u + 1, bufs[(u + 1) % 2])
            else:
                cm_next = None if then is None else scores(then, 0, bufs[0])
            m, acc = update(c0 + u, bufs[u % 2], cm, m, acc)
            cm = cm_next
        return cm, m, acc

    n_trips = (n_chunks - 2) // trip
    cm = scores(tiles[0], 0, sa_scr)
    for t, qt in enumerate(tiles):
        init = (cm, jnp.full((1, tq), NEG_BIG, F32), jnp.zeros((V_ROWS, tq), F32))
        carry = lax.fori_loop(0, n_trips, lambda i, c, qt=qt: steps(qt, i * trip, trip, *c), init)
        cm, _, acc = steps(qt, n_trips * trip, n_chunks - n_trips * trip, *carry,
                           then=tiles[t + 1] if t + 1 < len(tiles) else None)
        o_ref[:, tq * t:tq * (t + 1)] = (acc[:D_V] * (1.0 / acc[D_V:D_V + 1])).astype(o_ref.dtype)


def _attention(qt, k, vt):
    B, _, _, S = qt.shape
    n_chunks, tk = vt.shape[2], vt.shape[4]
    tq = Q_TILE
    tqb = Q_TILE * Q_TILES_PER_STEP
    kslot = lambda h: jnp.where(h < N_HEADS_A, 0, h - (N_HEADS_A - 1))
    vslot = lambda h: jnp.where(h < N_HEADS_A, h // (N_HEADS_A // N_KV_A), h - (N_HEADS_A - N_KV_A))
    return pl.pallas_call(
        functools.partial(_attn_kernel, n_chunks=n_chunks, tk=tk, tq=tq, trip=ATTN_TRIP),
        out_shape=jax.ShapeDtypeStruct((B, N_HEADS, D_V, S), BF16),
        grid=(B, N_HEADS, S // tqb),
        in_specs=[
            pl.BlockSpec((None, None, QK_PAD, tqb), lambda b, h, i: (b, h, 0, i)),
            pl.BlockSpec((None, None, S, QK_PAD), lambda b, h, i: (b, kslot(h), 0, 0)),
            pl.BlockSpec((None, None, n_chunks, V_ROWS, tk), lambda b, h, i: (b, vslot(h), 0, 0, 0)),
        ],
        out_specs=pl.BlockSpec((None, None, D_V, tqb), lambda b, h, i: (b, h, 0, i)),
        scratch_shapes=[pltpu.VMEM((tk, tq), F32), pltpu.VMEM((tk, tq), F32)],
        compiler_params=_cparams(("parallel", "parallel", "arbitrary")),
        name="attn_sweep",
    )(qt, k, vt)


def _f1_kernel(x_ref, mod_ref, gmix_ref, fc_ref, m1_ref, y_ref, z_scr):
    n1 = x_ref.shape[0]
    x = x_ref[...].reshape(n1 * FFT_T, D_MODEL)
    mod = mod_ref[...]
    h = _adaln_rows(x, gmix_ref[...], mod[0:1], mod[1:2])
    hb = h.astype(BF16)
    fc = fc_ref[...]
    for g in range(N_FGROUPS):
        zg = _dot(hb[:, FGROUP * g:FGROUP * (g + 1)], fc)
        z_scr[:, :, FGROUP * g:FGROUP * (g + 1)] = zg[:, :FGROUP].reshape(n1, FFT_T, FGROUP)
        z_scr[:, :, D_MODEL + FGROUP * g:D_MODEL + FGROUP * (g + 1)] = zg[:, FGROUP:].reshape(n1, FFT_T, FGROUP)
    m1 = m1_ref[...]
    for j in range(FFT_T):
        zj = z_scr[:, j, :]
        st = jnp.concatenate([zj[:, :D_MODEL], zj[:, D_MODEL:]], axis=0).astype(BF16)
        y = _dot(m1, st)
        y_ref[:, j, 0:D_MODEL] = y[:n1]
        y_ref[:, j, D_MODEL:2 * D_MODEL] = y[n1:]


def _f2_kernel(y_ref, x_ref, mod_ref, cs_ref, wc_ref, o_ref, f_scr):
    for j in range(FFT_T):
        yk = y_ref[j]
        st = jnp.concatenate([yk[:, :D_MODEL], yk[:, D_MODEL:]], axis=0).astype(BF16)
        f_scr[FFT_N2 * j:FFT_N2 * (j + 1), :] = _dot(cs_ref[j], st).astype(BF16)
    m = _dot(f_scr[...], wc_ref[...])
    gt1 = mod_ref[...][2:3]
    for j in range(FFT_T):
        o_ref[:, j, :] = x_ref[:, j, :] + gt1 * m[FFT_N2 * j:FFT_N2 * (j + 1)]


def _dft_constants(S):
    n1 = S // FFT_N2
    k = np.arange(FGROUP)
    ang = 2.0 * np.pi * np.outer(k, k) / FGROUP
    fc = np.concatenate([np.cos(ang), -np.sin(ang)], axis=1) / math.sqrt(FGROUP)
    a = np.arange(n1)
    ang1 = 2.0 * np.pi * np.outer(a, a) / n1
    c1, s1 = np.cos(ang1), np.sin(ang1)
    m1 = np.block([[c1, s1], [-s1, c1]]) / math.sqrt(n1)
    b = np.arange(FFT_N2)
    ang2 = 2.0 * np.pi * np.outer(b, b) / FFT_N2
    c2, s2 = np.cos(ang2), np.sin(ang2)
    angt = 2.0 * np.pi * np.outer(a, b) / S
    tr, ti = np.cos(angt)[:, None, :], -np.sin(angt)[:, None, :]
    cs = np.concatenate([c2 * tr + s2 * ti, s2 * tr - c2 * ti], axis=2) / math.sqrt(FFT_N2)
    return jnp.asarray(fc, BF16), jnp.asarray(m1, BF16), jnp.asarray(cs, BF16)


def _fourier_layer(x, mod_l, g_mix, wc):
    B, S, _ = x.shape
    n1 = S // FFT_N2
    fc, m1, cs = _dft_constants(S)
    const2 = lambda b, i: (0, 0)
    y = pl.pallas_call(
        _f1_kernel,
        out_shape=jax.ShapeDtypeStruct((B, n1, FFT_N2, 2 * D_MODEL), F32),
        grid=(B, FFT_N2 // FFT_T),
        in_specs=[
            pl.BlockSpec((None, n1, FFT_T, D_MODEL), lambda b, i: (b, 0, i, 0)),
            pl.BlockSpec((None, 6, D_MODEL), lambda b, i: (b, 0, 0)),
            pl.BlockSpec((1, D_MODEL), const2),
            pl.BlockSpec((FGROUP, 2 * FGROUP), const2),
            pl.BlockSpec((2 * n1, 2 * n1), const2),
        ],
        out_specs=pl.BlockSpec((None, n1, FFT_T, 2 * D_MODEL), lambda b, i: (b, 0, i, 0)),
        scratch_shapes=[pltpu.VMEM((n1, FFT_T, 2 * D_MODEL), F32)],
        compiler_params=_cparams(("parallel", "parallel")),
        name="fourier_stage1",
    )(x.reshape(B, n1, FFT_N2, D_MODEL), mod_l, g_mix, fc, m1)
    out = pl.pallas_call(
        _f2_kernel,
        out_shape=jax.ShapeDtypeStruct((B, FFT_N2, n1, D_MODEL), F32),
        grid=(B, n1 // FFT_T),
        in_specs=[
            pl.BlockSpec((None, FFT_T, FFT_N2, 2 * D_MODEL), lambda b, i: (b, i, 0, 0)),
            pl.BlockSpec((None, FFT_N2, FFT_T, D_MODEL), lambda b, i: (b, 0, i, 0)),
            pl.BlockSpec((None, 6, D_MODEL), lambda b, i: (b, 0, 0)),
            pl.BlockSpec((FFT_T, FFT_N2, 2 * FFT_N2), lambda b, i: (i, 0, 0)),
            pl.BlockSpec((D_MODEL, D_MODEL), const2),
        ],
        out_specs=pl.BlockSpec((None, FFT_N2, FFT_T, D_MODEL), lambda b, i: (b, 0, i, 0)),
        scratch_shapes=[pltpu.VMEM((FFT_T * FFT_N2, D_MODEL), BF16)],
        compiler_params=_cparams(("parallel", "parallel")),
        name="fourier_stage2",
    )(y, x.reshape(B, FFT_N2, n1, D_MODEL), mod_l, cs, wc)
    return out.reshape(B, S, D_MODEL)


def _route_t(s, sb):
    rows = [sb[e:e + 1] for e in range(N_EXPERTS)]
    gscore = []
    for g in range(N_EGROUPS):
        a, b, c, d = rows[4 * g:4 * g + 4]
        gscore.append(jnp.maximum(jnp.maximum(jnp.maximum(a + b, a + c), jnp.maximum(a + d, b + c)),
                                  jnp.maximum(b + d, c + d)))
    best = gscore[0]
    gsel = jnp.zeros_like(best)
    for g in range(1, N_EGROUPS):
        upd = gscore[g] > best
        gsel = jnp.where(upd, float(g), gsel)
        best = jnp.where(upd, gscore[g], best)
    picked = []
    for e in range(N_EXPERTS):
        g = e // EXPERTS_PER_GROUP
        rank = jnp.zeros_like(best)
        for j in range(4 * g, 4 * g + 4):
            if j == e:
                continue
            ahead = (rows[j] >= rows[e]) if j < e else (rows[j] > rows[e])
            rank = rank + jnp.where(ahead, 1.0, 0.0)
        sel = jnp.where(rank < 1.5, 1.0, 0.0) * jnp.where(gsel == float(g), 1.0, 0.0)
        picked.append(sel * s[e:e + 1])
    total = picked[0]
    for e in range(1, N_EXPERTS):
        total = total + picked[e]
    inv = 1.0 / total
    return [p * inv for p in picked], gsel


def _moe_kernel(*refs, final, after_attention):
    if after_attention:
        o_ref, wo_ref, *refs = refs
    (x_ref, mod_ref, gffn_ref, wr_ref, bias_ref, earlier_ref, wg_ref, wu_ref, wd_ref, gfin_ref,
     out_ref, srt_scr, y_scr) = refs
    tm = x_ref.shape[0]
    x = x_ref[...]
    mod = mod_ref[...]
    if after_attention:
        ot = o_ref[...].reshape(N_HEADS * D_V, tm)
        x = x + mod[2:3] * _dot_tn(ot, wo_ref[...])
    t = _adaln_rows(x, gffn_ref[...], mod[3:4], mod[4:5])
    t_hi = t.astype(BF16)
    t_lo = (t - t_hi.astype(F32)).astype(BF16)
    wr = wr_ref[...]
    a = _dot_nt(wr, t_hi)
    b = _dot_nt(wr[0:N_EXPERTS], t_lo)
    logits = a[0:N_EXPERTS] + a[N_EXPERTS:2 * N_EXPERTS] + b
    s = _sigmoid(logits)
    gates, gsel = _route_t(s, s + bias_ref[...])

    masks = [jnp.where(gsel == float(g), 1.0, 0.0) for g in range(N_EGROUPS)]
    m8 = jnp.concatenate(masks + [jnp.zeros((SUBLANES - N_EGROUPS, tm), F32)], axis=0)
    ranks = _dot(m8.astype(BF16), earlier_ref[...])
    counts = jnp.sum(m8, axis=1, keepdims=True)
    n_chunks = jnp.floor((counts + (MOE_CHUNK - 1)) * (1.0 / MOE_CHUNK))
    starts = []
    start = jnp.zeros((1, 1), F32)
    for g in range(N_EGROUPS):
        starts.append(start)
        start = start + n_chunks[g:g + 1] * MOE_CHUNK
    pos = masks[0] * (starts[0] + ranks[0:1])
    for g in range(1, N_EGROUPS):
        pos = pos + masks[g] * (starts[g] + ranks[g:g + 1])
    perm = jnp.where(lax.broadcasted_iota(jnp.int32, (MOE_ROWS, tm), 0) == pos.astype(jnp.int32),
                     1.0, 0.0).astype(BF16)

    g4 = []
    for j in range(EXPERTS_PER_GROUP):
        r = masks[0] * gates[j]
        for g in range(1, N_EGROUPS):
            r = r + masks[g] * gates[EXPERTS_PER_GROUP * g + j]
        g4.append(r)
    g4t = jnp.concatenate(g4 + [jnp.zeros((LANES - EXPERTS_PER_GROUP, tm), F32)], axis=0).T
    src = jnp.concatenate([t_hi, g4t.astype(BF16)], axis=1)
    srt_scr[...] = _dot(perm, src).astype(BF16)
    y_scr[...] = jnp.zeros_like(y_scr)

    for g in range(N_EGROUPS):
        base = starts[g][0, 0].astype(jnp.int32)

        def chunk(j, carry, g=g, base=base):
            off = pl.multiple_of(base + j * MOE_CHUNK, MOE_CHUNK)
            xs = srt_scr[pl.ds(off, MOE_CHUNK), 0:D_MODEL]
            gs = srt_scr[pl.ds(off, MOE_CHUNK), D_MODEL:D_MODEL + LANES].astype(F32)
            he = []
            for e in range(EXPERTS_PER_GROUP):
                u1 = _dot(xs, wg_ref[EXPERTS_PER_GROUP * g + e])
                u2 = _dot(xs, wu_ref[EXPERTS_PER_GROUP * g + e])
                he.append(u1 * _sigmoid(u1) * u2 * gs[:, e:e + 1])
            he = jnp.concatenate(he, axis=1).astype(BF16)
            y_scr[pl.ds(off, MOE_CHUNK), :] = _dot(he, wd_ref[g]).astype(BF16)
            return carry

        lax.fori_loop(0, n_chunks[g, 0].astype(jnp.int32), chunk, 0)

    y = x + mod[5:6] * _dot_tn(perm, y_scr[...])
    if final:
        y = _rms_rows(y, gfin_ref[...])
    out_ref[...] = y


def _moe(x, mod_l, g_ffn, wr, bias, wg, wu, wd, g_final, final, attn=None):
    B, S, _ = x.shape
    tm = MOE_TILE
    const = lambda b, i: (0, 0)
    earlier = jnp.asarray(np.triu(np.ones((tm, tm), np.float32), k=1), BF16)
    wspec = pl.BlockSpec((N_EXPERTS, D_MODEL, D_EXPERT), lambda b, i: (0, 0, 0),
                         pipeline_mode=pl.Buffered(1))
    wdspec = pl.BlockSpec((N_EGROUPS, EXPERTS_PER_GROUP * D_EXPERT, D_MODEL), lambda b, i: (0, 0, 0),
                          pipeline_mode=pl.Buffered(1))
    attn_specs = [] if attn is None else [
        pl.BlockSpec((None, N_HEADS, D_V, tm), lambda b, i: (b, 0, 0, i)),
        pl.BlockSpec((D_MODEL, D_MODEL), const, pipeline_mode=pl.Buffered(1)),
    ]
    return pl.pallas_call(
        functools.partial(_moe_kernel, final=final, after_attention=attn is not None),
        out_shape=jax.ShapeDtypeStruct(x.shape, F32),
        grid=(B, S // tm),
        in_specs=attn_specs + [
            pl.BlockSpec((None, tm, D_MODEL), lambda b, i: (b, i, 0)),
            pl.BlockSpec((None, 6, D_MODEL), lambda b, i: (b, 0, 0)),
            pl.BlockSpec((1, D_MODEL), const),
            pl.BlockSpec((2 * N_EXPERTS, D_MODEL), const),
            pl.BlockSpec((N_EXPERTS, 1), const),
            pl.BlockSpec((tm, tm), const, pipeline_mode=pl.Buffered(1)),
            wspec, wspec, wdspec,
            pl.BlockSpec((1, D_MODEL), const),
        ],
        out_specs=pl.BlockSpec((None, tm, D_MODEL), lambda b, i: (b, i, 0)),
        scratch_shapes=[
            pltpu.VMEM((MOE_ROWS, D_MODEL + LANES), BF16),
            pltpu.VMEM((MOE_ROWS, D_MODEL), BF16),
        ],
        compiler_params=_cparams(("parallel", "parallel")),
        name="moe",
    )(*(attn or ()), x, mod_l, g_ffn, wr, bias, earlier, wg, wu, wd, g_final)


def _rope_tables(S):
    pos = jnp.arange(S, dtype=jnp.int32)
    rows = (pos // GRID_W).astype(F32)[None, :]
    cols = (pos % GRID_W).astype(F32)[None, :]
    tabs = []
    for half in (HEAD_DIM // 2, D_ROPE // 2):
        inv = (ROPE_THETA ** (-jnp.arange(0, half, 2, dtype=F32) / half))[:, None]
        for p in (rows, cols):
            ang = inv * p
            tabs += [jnp.cos(ang), jnp.sin(ang)]
    return jnp.concatenate(tabs, axis=0)


def _trunk(x, mod, p):
    B, S, _ = x.shape
    tab = _rope_tables(S)
    for l in range(DEPTH):
        mod_l = mod[l]
        g_mix = p["g_mix"][l][None, :]
        if l % 2 == 0:
            i = l // 2
            qt, k, vt = _pre_attn(x, mod_l, g_mix, tab, p["wint"][i], p["g_qa"][i][:, None],
                                  p["g_ka"][i][:, None], p["g_cq"][i][:, None], p["wuqt"][i],
                                  p["g_ckv"][i][:, None], p["wukvt"][i])
            attn = (_attention(qt, k, vt), p["wo"][i])
        else:
            attn = None
            x = _fourier_layer(x, mod_l, g_mix, p["wc"][l // 2])
        x = _moe(x, mod_l, p["g_ffn"][l][None, :], p["wr"], p["bias"], p["wg"][l], p["wu"][l],
                 p["wd"][l], p["g_final"], final=(l == DEPTH - 1), attn=attn)
    return x


def kernel(x_prompt, x_sample, c_prompt, c_sample, w_ada, b_ada, g_mix, g_ffn, w_in, g_qa, g_ka, g_cq,
           w_uq, g_ckv, w_ukv, w_o_attn, w_fourier, w_router, router_bias, w_gate, w_up, w_down, g_final):
    nbp = x_prompt.shape[0]
    mod = _modulation(jnp.concatenate([c_prompt, c_sample], axis=0), w_ada, b_ada)
    mod = mod.reshape(DEPTH, -1, 6, D_MODEL)
    wrt = w_router.T
    wr_hi = wrt.astype(BF16)
    wr_lo = (wrt - wr_hi.astype(F32)).astype(BF16)
    p = {
        "g_mix": g_mix, "g_ffn": g_ffn, "g_qa": g_qa, "g_ka": g_ka, "g_cq": g_cq, "g_ckv": g_ckv,
        "wint": jnp.swapaxes(w_in, 1, 2).astype(BF16),
        "wuqt": jnp.swapaxes(w_uq, 1, 2).astype(BF16),
        "wukvt": jnp.swapaxes(w_ukv, 1, 2).astype(BF16),
        "wo": w_o_attn.astype(BF16),
        "wc": w_fourier.astype(BF16),
        "wr": jnp.concatenate([wr_hi, wr_lo], axis=0),
        "bias": router_bias[:, None],
        "wg": w_gate.astype(BF16),
        "wu": w_up.astype(BF16),
        "wd": w_down.astype(BF16).reshape(DEPTH, N_EGROUPS, EXPERTS_PER_GROUP * D_EXPERT, D_MODEL),
        "g_final": g_final[None, :],
    }
    y_prompt = _trunk(x_prompt, mod[:, :nbp], p)
    y_sample = _trunk(x_sample, mod[:, nbp:], p)
    return (y_prompt, y_sample)
```

```python
import functools
import math

import numpy as np
import jax
import jax.numpy as jnp
from jax import lax
from jax.experimental import pallas as pl
from jax.experimental.pallas import tpu as pltpu

F32 = jnp.float32
BF16 = jnp.bfloat16
QK_DTYPE = jnp.float8_e4m3fn
QK_MAX = float(jnp.finfo(QK_DTYPE).max)
LANES = 128
SUBLANES = 8

D_MODEL = 1024
DEPTH = 4
GRID_W = 64
HEAD_DIM = 64
N_HEADS_A = 8
N_KV_A = 2
N_HEADS_B = 8
D_NOPE = 64
D_ROPE = 32
D_V = 64
D_CQ = 384
D_CKV = 256
N_FGROUPS = 4
FGROUP = D_MODEL // N_FGROUPS
ROPE_THETA = 10000.0
N_EXPERTS = 16
N_EGROUPS = 4
EXPERTS_PER_GROUP = 4
D_EXPERT = 256
EPS = 1e-6
IN_SIZES = (N_HEADS_A * HEAD_DIM, N_KV_A * HEAD_DIM, N_KV_A * HEAD_DIM, D_CQ, D_CKV, D_ROPE)
IN_WIDTH = sum(IN_SIZES)
IN_OFFS = tuple(int(v) for v in np.cumsum((0,) + IN_SIZES))

N_HEADS = N_HEADS_A + N_HEADS_B
N_KSLOTS = 1 + N_HEADS_B
N_VSLOTS = N_KV_A + N_HEADS_B
QK_PAD = LANES
LOG2E = 1.4426950408889634
SCALE_A = HEAD_DIM ** -0.5 * LOG2E
SCALE_B = (D_NOPE + D_ROPE) ** -0.5 * LOG2E
NEG_BIG = -1e30

FFT_N2 = 128
FFT_T = SUBLANES

TOKEN_TILE = 512
PRE_ATTN_TILES = 2
Q_TILE = 512
Q_TILES_PER_STEP = 4
ATTN_TRIP = 6
V_ROWS = D_V + 16
MOE_TILE = 512
MOE_CHUNK = 128
MOE_ROWS = MOE_TILE + (N_EGROUPS - 1) * MOE_CHUNK
VMEM_LIMIT = 56 * 1024 * 1024


def _cparams(sem):
    return pltpu.CompilerParams(dimension_semantics=sem, vmem_limit_bytes=VMEM_LIMIT)


def _to_qk(x):
    return jnp.clip(x, -QK_MAX, QK_MAX).astype(QK_DTYPE)


def _sigmoid(x):
    return 1.0 / (1.0 + jnp.exp(-x))


def _rms_rows(x, g):
    ms = jnp.mean(x * x, axis=-1, keepdims=True)
    return x * lax.rsqrt(ms + EPS) * g


def _adaln_rows(x, g, shift, scale):
    return _rms_rows(x, g * (1.0 + scale)) + shift


def _rms_cols(xt, gcol):
    ms = jnp.mean(xt * xt, axis=0, keepdims=True)
    return xt * lax.rsqrt(ms + EPS) * gcol


def _dot(a, b):
    return jnp.dot(a, b, preferred_element_type=F32)


def _dot_nt(a, b):
    return lax.dot_general(a, b, (((1,), (1,)), ((), ())), preferred_element_type=F32)


def _dot_tn(a, b):
    return lax.dot_general(a, b, (((0,), (0,)), ((), ())), preferred_element_type=F32)


def _mod_kernel(c_ref, w_ref, b_ref, o_ref):
    c = c_ref[...]
    ca = c * _sigmoid(c)
    o_ref[...] = _dot(ca, w_ref[...]) + b_ref[...]


def _modulation(c_all, w_ada, b_ada):
    nb = c_all.shape[0]
    tn = 1536
    return pl.pallas_call(
        _mod_kernel,
        out_shape=jax.ShapeDtypeStruct((DEPTH, nb, 6 * D_MODEL), F32),
        grid=(DEPTH, 6 * D_MODEL // tn),
        in_specs=[
            pl.BlockSpec((nb, D_MODEL), lambda l, j: (0, 0)),
            pl.BlockSpec((None, D_MODEL, tn), lambda l, j: (l, 0, j)),
            pl.BlockSpec((None, 1, tn), lambda l, j: (l, 0, j)),
        ],
        out_specs=pl.BlockSpec((None, nb, tn), lambda l, j: (l, 0, j)),
        compiler_params=_cparams(("arbitrary", "arbitrary")),
        name="adaln_mod",
    )(c_all, w_ada, b_ada.reshape(DEPTH, 1, 6 * D_MODEL))


def _rope_t(xt, c, s):
    n = c.shape[0]
    x1, x2 = xt[:n], xt[n:]
    return jnp.concatenate([x1 * c - x2 * s, x1 * s + x2 * c], axis=0)


def _axial_t(xt, cr, sr, cc, sc):
    d2 = xt.shape[0] // 2
    return jnp.concatenate([_rope_t(xt[:d2], cr, sr), _rope_t(xt[d2:], cc, sc)], axis=0)


def _pre_attn_kernel(x_ref, mod_ref, gmix_ref, tab_ref, wint_ref, gqa_ref, gka_ref, gcq_ref,
                     wuqt_ref, gckv_ref, wukvt_ref, q_ref, k_ref, v_ref):
    tm = TOKEN_TILE
    mod = mod_ref[...]
    o_q, o_k, o_v, o_cq, o_ckv, o_kr = IN_OFFS[:6]
    dq = D_NOPE + D_ROPE
    dkv = D_NOPE + D_V
    zeros64 = jnp.zeros((HEAD_DIM, tm), F32)
    zeros32 = jnp.zeros((QK_PAD - D_NOPE - D_ROPE, tm), F32)
    ones16 = jnp.ones((V_ROWS - D_V, tm), F32)

    def project(t):
        h = _adaln_rows(x_ref[tm * t:tm * (t + 1), :], gmix_ref[...], mod[0:1], mod[1:2])
        return _dot_nt(wint_ref[...], h.astype(BF16))

    def latents(zt):
        cqn = _rms_cols(zt[o_cq:o_ckv], gcq_ref[...]).astype(BF16)
        ckvn = _rms_cols(zt[o_ckv:o_kr], gckv_ref[...]).astype(BF16)
        return _dot(wuqt_ref[...], cqn), _dot(wukvt_ref[...], ckvn)

    def finish(t, zt, qbt, kvt):
        lanes = slice(tm * t, tm * (t + 1))
        tab = tab_ref[:, lanes]
        ra = (tab[0:16], tab[16:32], tab[32:48], tab[48:64])
        rb = (tab[64:72], tab[72:80], tab[80:88], tab[88:96])
        gqa = gqa_ref[...]
        for hh in range(N_HEADS_A):
            qh = _axial_t(_rms_cols(zt[o_q + HEAD_DIM * hh:o_q + HEAD_DIM * (hh + 1)], gqa), *ra) * SCALE_A
            parts = [qh, zeros64] if hh // (N_HEADS_A // N_KV_A) == 0 else [zeros64, qh]
            q_ref[hh, :, lanes] = _to_qk(jnp.concatenate(parts, axis=0))
        gka = gka_ref[...]
        kts = [jnp.concatenate(
            [_axial_t(_rms_cols(zt[o_k + HEAD_DIM * g:o_k + HEAD_DIM * (g + 1)], gka), *ra)
             for g in range(N_KV_A)], axis=0)]
        for g in range(N_KV_A):
            v_ref[g, t] = jnp.concatenate(
                [zt[o_v + HEAD_DIM * g:o_v + HEAD_DIM * (g + 1)], ones16], axis=0).astype(BF16)
        for hh in range(N_HEADS_B):
            nope = qbt[dq * hh:dq * hh + D_NOPE]
            rp = _axial_t(qbt[dq * hh + D_NOPE:dq * (hh + 1)], *rb)
            q_ref[N_HEADS_A + hh, :, lanes] = _to_qk(jnp.concatenate([nope, rp, zeros32], axis=0) * SCALE_B)
        kr = _axial_t(zt[o_kr:o_kr + D_ROPE], *rb)
        for hh in range(N_HEADS_B):
            kts.append(jnp.concatenate([kvt[dkv * hh:dkv * hh + D_NOPE], kr, zeros32], axis=0))
            v_ref[N_KV_A + hh, t] = jnp.concatenate(
                [kvt[dkv * hh + D_NOPE:dkv * (hh + 1)], ones16], axis=0).astype(BF16)
        k_all = jnp.concatenate(kts, axis=0).T
        for i in range(N_KSLOTS):
            k_ref[i, lanes, :] = _to_qk(k_all[:, QK_PAD * i:QK_PAD * (i + 1)])

    zts = [project(t) for t in range(PRE_ATTN_TILES)]
    lat = [latents(zt) for zt in zts]
    for t in range(PRE_ATTN_TILES):
        finish(t, zts[t], *lat[t])


def _pre_attn(x, mod_l, g_mix, tab, wint, gqa, gka, gcq, wuqt, gckv, wukvt):
    B, S, _ = x.shape
    tm = TOKEN_TILE
    nt = S // tm
    tb = tm * PRE_ATTN_TILES
    const = lambda b, i: (0, 0)
    return pl.pallas_call(
        _pre_attn_kernel,
        out_shape=(
            jax.ShapeDtypeStruct((B, N_HEADS, QK_PAD, S), QK_DTYPE),
            jax.ShapeDtypeStruct((B, N_KSLOTS, S, QK_PAD), QK_DTYPE),
            jax.ShapeDtypeStruct((B, N_VSLOTS, nt, V_ROWS, tm), BF16),
        ),
        grid=(B, S // tb),
        in_specs=[
            pl.BlockSpec((None, tb, D_MODEL), lambda b, i: (b, i, 0)),
            pl.BlockSpec((None, 6, D_MODEL), lambda b, i: (b, 0, 0)),
            pl.BlockSpec((1, D_MODEL), const),
            pl.BlockSpec((96, tb), lambda b, i: (0, i)),
            pl.BlockSpec((IN_WIDTH, D_MODEL), const),
            pl.BlockSpec((HEAD_DIM, 1), const),
            pl.BlockSpec((HEAD_DIM, 1), const),
            pl.BlockSpec((D_CQ, 1), const),
            pl.BlockSpec((N_HEADS_B * (D_NOPE + D_ROPE), D_CQ), const),
            pl.BlockSpec((D_CKV, 1), const),
            pl.BlockSpec((N_HEADS_B * (D_NOPE + D_V), D_CKV), const),
        ],
        out_specs=(
            pl.BlockSpec((None, N_HEADS, QK_PAD, tb), lambda b, i: (b, 0, 0, i)),
            pl.BlockSpec((None, N_KSLOTS, tb, QK_PAD), lambda b, i: (b, 0, i, 0)),
            pl.BlockSpec((None, N_VSLOTS, PRE_ATTN_TILES, V_ROWS, tm), lambda b, i: (b, 0, i, 0, 0)),
        ),
        compiler_params=_cparams(("parallel", "parallel")),
        name="pre_attn",
    )(x, mod_l, g_mix, tab, wint, gqa, gka, gcq, wuqt, gckv, wukvt)


def _attn_kernel(q_ref, k_ref, v_ref, o_ref, s0a, s0b, s1a, s1b, *, n_chunks, tk, tq, trip):
    n_tiles = q_ref.shape[1] // tq
    bufs = ((s0a, s0b), (s1a, s1b))

    def scores(qt, c, s_ref):
        off = c * tk if isinstance(c, int) else pl.multiple_of(c * tk, tk)
        s = _dot(k_ref[pl.ds(off, tk), :], qt)
        s_ref[...] = s
        return jnp.max(s, axis=0, keepdims=True)

    def update(c, s_ref, cm, m, acc):
        m_new = jnp.maximum(m, cm)
        p = jnp.exp2(s_ref[...] - m_new).astype(BF16)
        return m_new, acc * jnp.exp2(m - m_new) + _dot(v_ref[c], p)

    def steps(qts, c0, count, carry, then=None):
        cms, ms, accs = carry
        cms, ms, accs = list(cms), list(ms), list(accs)
        for u in range(count):
            nxt = [None, None]
            if not isinstance(c0, int) or c0 + u + 1 < n_chunks:
                for w in range(2):
                    nxt[w] = scores(qts[w], c0 + u + 1, bufs[w][(u + 1) % 2])
            elif then is not None:
                for w in range(2):
                    nxt[w] = scores(then[w], 0, bufs[w][0])
            for w in range(2):
                ms[w], accs[w] = update(c0 + u, bufs[w][u % 2], cms[w], ms[w], accs[w])
            cms = nxt
        return tuple(cms), tuple(ms), tuple(accs)

    n_trips = (n_chunks - 2) // trip
    pairs = [[q_ref[:, tq * (t + w):tq * (t + w + 1)] for w in range(2)] for t in range(0, n_tiles, 2)]
    cms = tuple(scores(pairs[0][w], 0, bufs[w][0]) for w in range(2))
    for i, qts in enumerate(pairs):
        init = (cms, tuple(jnp.full((1, tq), NEG_BIG, F32) for _ in range(2)),
                tuple(jnp.zeros((V_ROWS, tq), F32) for _ in range(2)))
        carry = lax.fori_loop(0, n_trips, lambda j, c, qts=qts: steps(qts, j * trip, trip, c), init)
        cms, _, accs = steps(qts, n_trips * trip, n_chunks - n_trips * trip, carry,
                             then=pairs[i + 1] if i + 1 < len(pairs) else None)
        for w in range(2):
            t = 2 * i + w
            o_ref[:, tq * t:tq * (t + 1)] = (accs[w][:D_V] * (1.0 / accs[w][D_V:D_V + 1])).astype(o_ref.dtype)


def _attention(qt, k, vt):
    B, _, _, S = qt.shape
    n_chunks, tk = vt.shape[2], vt.shape[4]
    tq = Q_TILE
    tqb = Q_TILE * Q_TILES_PER_STEP
    kslot = lambda h: jnp.where(h < N_HEADS_A, 0, h - (N_HEADS_A - 1))
    vslot = lambda h: jnp.where(h < N_HEADS_A, h // (N_HEADS_A // N_KV_A), h - (N_HEADS_A - N_KV_A))
    return pl.pallas_call(
        functools.partial(_attn_kernel, n_chunks=n_chunks, tk=tk, tq=tq, trip=ATTN_TRIP),
        out_shape=jax.ShapeDtypeStruct((B, N_HEADS, D_V, S), BF16),
        grid=(B, N_HEADS, S // tqb),
        in_specs=[
            pl.BlockSpec((None, None, QK_PAD, tqb), lambda b, h, i: (b, h, 0, i)),
            pl.BlockSpec((None, None, S, QK_PAD), lambda b, h, i: (b, kslot(h), 0, 0)),
            pl.BlockSpec((None, None, n_chunks, V_ROWS, tk), lambda b, h, i: (b, vslot(h), 0, 0, 0)),
        ],
        out_specs=pl.BlockSpec((None, None, D_V, tqb), lambda b, h, i: (b, h, 0, i)),
        scratch_shapes=[pltpu.VMEM((tk, tq), F32)] * 4,
        compiler_params=_cparams(("parallel", "parallel", "arbitrary")),
        name="attn_sweep",
    )(qt, k, vt)


def _f1_kernel(x_ref, mod_ref, gmix_ref, fc_ref, m1_ref, y_ref, z_scr):
    n1 = x_ref.shape[0]
    x = x_ref[...].reshape(n1 * FFT_T, D_MODEL)
    mod = mod_ref[...]
    h = _adaln_rows(x, gmix_ref[...], mod[0:1], mod[1:2])
    hb = h.astype(BF16)
    fc = fc_ref[...]
    for g in range(N_FGROUPS):
        zg = _dot(hb[:, FGROUP * g:FGROUP * (g + 1)], fc)
        z_scr[:, :, FGROUP * g:FGROUP * (g + 1)] = zg[:, :FGROUP].reshape(n1, FFT_T, FGROUP)
        z_scr[:, :, D_MODEL + FGROUP * g:D_MODEL + FGROUP * (g + 1)] = zg[:, FGROUP:].reshape(n1, FFT_T, FGROUP)
    m1 = m1_ref[...]
    for j in range(FFT_T):
        zj = z_scr[:, j, :]
        st = jnp.concatenate([zj[:, :D_MODEL], zj[:, D_MODEL:]], axis=0).astype(BF16)
        y = _dot(m1, st)
        y_ref[:, j, 0:D_MODEL] = y[:n1]
        y_ref[:, j, D_MODEL:2 * D_MODEL] = y[n1:]


def _f2_kernel(y_ref, x_ref, mod_ref, cs_ref, wc_ref, o_ref, f_scr):
    for j in range(FFT_T):
        yk = y_ref[j]
        st = jnp.concatenate([yk[:, :D_MODEL], yk[:, D_MODEL:]], axis=0).astype(BF16)
        f_scr[FFT_N2 * j:FFT_N2 * (j + 1), :] = _dot(cs_ref[j], st).astype(BF16)
    m = _dot(f_scr[...], wc_ref[...])
    gt1 = mod_ref[...][2:3]
    for j in range(FFT_T):
        o_ref[:, j, :] = x_ref[:, j, :] + gt1 * m[FFT_N2 * j:FFT_N2 * (j + 1)]


def _dft_constants(S):
    n1 = S // FFT_N2
    k = np.arange(FGROUP)
    ang = 2.0 * np.pi * np.outer(k, k) / FGROUP
    fc = np.concatenate([np.cos(ang), -np.sin(ang)], axis=1) / math.sqrt(FGROUP)
    a = np.arange(n1)
    ang1 = 2.0 * np.pi * np.outer(a, a) / n1
    c1, s1 = np.cos(ang1), np.sin(ang1)
    m1 = np.block([[c1, s1], [-s1, c1]]) / math.sqrt(n1)
    b = np.arange(FFT_N2)
    ang2 = 2.0 * np.pi * np.outer(b, b) / FFT_N2
    c2, s2 = np.cos(ang2), np.sin(ang2)
    angt = 2.0 * np.pi * np.outer(a, b) / S
    tr, ti = np.cos(angt)[:, None, :], -np.sin(angt)[:, None, :]
    cs = np.concatenate([c2 * tr + s2 * ti, s2 * tr - c2 * ti], axis=2) / math.sqrt(FFT_N2)
    return jnp.asarray(fc, BF16), jnp.asarray(m1, BF16), jnp.asarray(cs, BF16)


def _fourier_layer(x, mod_l, g_mix, wc):
    B, S, _ = x.shape
    n1 = S // FFT_N2
    fc, m1, cs = _dft_constants(S)
    const2 = lambda b, i: (0, 0)
    y = pl.pallas_call(
        _f1_kernel,
        out_shape=jax.ShapeDtypeStruct((B, n1, FFT_N2, 2 * D_MODEL), F32),
        grid=(B, FFT_N2 // FFT_T),
        in_specs=[
            pl.BlockSpec((None, n1, FFT_T, D_MODEL), lambda b, i: (b, 0, i, 0)),
            pl.BlockSpec((None, 6, D_MODEL), lambda b, i: (b, 0, 0)),
            pl.BlockSpec((1, D_MODEL), const2),
            pl.BlockSpec((FGROUP, 2 * FGROUP), const2),
            pl.BlockSpec((2 * n1, 2 * n1), const2),
        ],
        out_specs=pl.BlockSpec((None, n1, FFT_T, 2 * D_MODEL), lambda b, i: (b, 0, i, 0)),
        scratch_shapes=[pltpu.VMEM((n1, FFT_T, 2 * D_MODEL), F32)],
        compiler_params=_cparams(("parallel", "parallel")),
        name="fourier_stage1",
    )(x.reshape(B, n1, FFT_N2, D_MODEL), mod_l, g_mix, fc, m1)
    out = pl.pallas_call(
        _f2_kernel,
        out_shape=jax.ShapeDtypeStruct((B, FFT_N2, n1, D_MODEL), F32),
        grid=(B, n1 // FFT_T),
        in_specs=[
            pl.BlockSpec((None, FFT_T, FFT_N2, 2 * D_MODEL), lambda b, i: (b, i, 0, 0)),
            pl.BlockSpec((None, FFT_N2, FFT_T, D_MODEL), lambda b, i: (b, 0, i, 0)),
            pl.BlockSpec((None, 6, D_MODEL), lambda b, i: (b, 0, 0)),
            pl.BlockSpec((FFT_T, FFT_N2, 2 * FFT_N2), lambda b, i: (i, 0, 0)),
            pl.BlockSpec((D_MODEL, D_MODEL), const2),
        ],
        out_specs=pl.BlockSpec((None, FFT_N2, FFT_T, D_MODEL), lambda b, i: (b, 0, i, 0)),
        scratch_shapes=[pltpu.VMEM((FFT_T * FFT_N2, D_MODEL), BF16)],
        compiler_params=_cparams(("parallel", "parallel")),
        name="fourier_stage2",
    )(y, x.reshape(B, FFT_N2, n1, D_MODEL), mod_l, cs, wc)
    return out.reshape(B, S, D_MODEL)


def _route_t(s, sb):
    rows = [sb[e:e + 1] for e in range(N_EXPERTS)]
    gscore = []
    for g in range(N_EGROUPS):
        a, b, c, d = rows[4 * g:4 * g + 4]
        gscore.append(jnp.maximum(jnp.maximum(jnp.maximum(a + b, a + c), jnp.maximum(a + d, b + c)),
                                  jnp.maximum(b + d, c + d)))
    best = gscore[0]
    gsel = jnp.zeros_like(best)
    for g in range(1, N_EGROUPS):
        upd = gscore[g] > best
        gsel = jnp.where(upd, float(g), gsel)
        best = jnp.where(upd, gscore[g], best)
    picked = []
    for e in range(N_EXPERTS):
        g = e // EXPERTS_PER_GROUP
        rank = jnp.zeros_like(best)
        for j in range(4 * g, 4 * g + 4):
            if j == e:
                continue
            ahead = (rows[j] >= rows[e]) if j < e else (rows[j] > rows[e])
            rank = rank + jnp.where(ahead, 1.0, 0.0)
        sel = jnp.where(rank < 1.5, 1.0, 0.0) * jnp.where(gsel == float(g), 1.0, 0.0)
        picked.append(sel * s[e:e + 1])
    total = picked[0]
    for e in range(1, N_EXPERTS):
        total = total + picked[e]
    inv = 1.0 / total
    return [p * inv for p in picked], gsel


def _moe_kernel(*refs, final, after_attention):
    if after_attention:
        o_ref, wo_ref, *refs = refs
    (x_ref, mod_ref, gffn_ref, wr_ref, bias_ref, earlier_ref, wg_ref, wu_ref, wd_ref, gfin_ref,
     out_ref, srt_scr, y_scr) = refs
    tm = x_ref.shape[0]
    x = x_ref[...]
    mod = mod_ref[...]
    if after_attention:
        ot = o_ref[...].reshape(N_HEADS * D_V, tm)
        x = x + mod[2:3] * _dot_tn(ot, wo_ref[...])
    t = _adaln_rows(x, gffn_ref[...], mod[3:4], mod[4:5])
    t_hi = t.astype(BF16)
    t_lo = (t - t_hi.astype(F32)).astype(BF16)
    wr = wr_ref[...]
    a = _dot_nt(wr, t_hi)
    b = _dot_nt(wr[0:N_EXPERTS], t_lo)
    logits = a[0:N_EXPERTS] + a[N_EXPERTS:2 * N_EXPERTS] + b
    s = _sigmoid(logits)
    gates, gsel = _route_t(s, s + bias_ref[...])

    masks = [jnp.where(gsel == float(g), 1.0, 0.0) for g in range(N_EGROUPS)]
    m8 = jnp.concatenate(masks + [jnp.zeros((SUBLANES - N_EGROUPS, tm), F32)], axis=0)
    ranks = _dot(m8.astype(BF16), earlier_ref[...])
    counts = jnp.sum(m8, axis=1, keepdims=True)
    n_chunks = jnp.floor((counts + (MOE_CHUNK - 1)) * (1.0 / MOE_CHUNK))
    starts = []
    start = jnp.zeros((1, 1), F32)
    for g in range(N_EGROUPS):
        starts.append(start)
        start = start + n_chunks[g:g + 1] * MOE_CHUNK
    pos = masks[0] * (starts[0] + ranks[0:1])
    for g in range(1, N_EGROUPS):
        pos = pos + masks[g] * (starts[g] + ranks[g:g + 1])
    perm = jnp.where(lax.broadcasted_iota(jnp.int32, (MOE_ROWS, tm), 0) == pos.astype(jnp.int32),
                     1.0, 0.0).astype(BF16)

    g4 = []
    for j in range(EXPERTS_PER_GROUP):
        r = masks[0] * gates[j]
        for g in range(1, N_EGROUPS):
            r = r + masks[g] * gates[EXPERTS_PER_GROUP * g + j]
        g4.append(r)
    g4t = jnp.concatenate(g4 + [jnp.zeros((LANES - EXPERTS_PER_GROUP, tm), F32)], axis=0).T
    src = jnp.concatenate([t_hi, g4t.astype(BF16)], axis=1)
    srt_scr[...] = _dot(perm, src).astype(BF16)
    y_scr[...] = jnp.zeros_like(y_scr)

    for g in range(N_EGROUPS):
        base = starts[g][0, 0].astype(jnp.int32)

        def chunk(j, carry, g=g, base=base):
            off = pl.multiple_of(base + j * MOE_CHUNK, MOE_CHUNK)
            xs = srt_scr[pl.ds(off, MOE_CHUNK), 0:D_MODEL]
            gs = srt_scr[pl.ds(off, MOE_CHUNK), D_MODEL:D_MODEL + LANES].astype(F32)
            he = []
            for e in range(EXPERTS_PER_GROUP):
                u1 = _dot(xs, wg_ref[EXPERTS_PER_GROUP * g + e])
                u2 = _dot(xs, wu_ref[EXPERTS_PER_GROUP * g + e])
                he.append(u1 * _sigmoid(u1) * u2 * gs[:, e:e + 1])
            he = jnp.concatenate(he, axis=1).astype(BF16)
            y_scr[pl.ds(off, MOE_CHUNK), :] = _dot(he, wd_ref[g]).astype(BF16)
            return carry

        lax.fori_loop(0, n_chunks[g, 0].astype(jnp.int32), chunk, 0)

    y = x + mod[5:6] * _dot_tn(perm, y_scr[...])
    if final:
        y = _rms_rows(y, gfin_ref[...])
    out_ref[...] = y


def _moe(x, mod_l, g_ffn, wr, bias, wg, wu, wd, g_final, final, attn=None):
    B, S, _ = x.shape
    tm = MOE_TILE
    const = lambda b, i: (0, 0)
    earlier = jnp.asarray(np.triu(np.ones((tm, tm), np.float32), k=1), BF16)
    wspec = pl.BlockSpec((N_EXPERTS, D_MODEL, D_EXPERT), lambda b, i: (0, 0, 0),
                         pipeline_mode=pl.Buffered(1))
    wdspec = pl.BlockSpec((N_EGROUPS, EXPERTS_PER_GROUP * D_EXPERT, D_MODEL), lambda b, i: (0, 0, 0),
                          pipeline_mode=pl.Buffered(1))
    attn_specs = [] if attn is None else [
        pl.BlockSpec((None, N_HEADS, D_V, tm), lambda b, i: (b, 0, 0, i)),
        pl.BlockSpec((D_MODEL, D_MODEL), const, pipeline_mode=pl.Buffered(1)),
    ]
    return pl.pallas_call(
        functools.partial(_moe_kernel, final=final, after_attention=attn is not None),
        out_shape=jax.ShapeDtypeStruct(x.shape, F32),
        grid=(B, S // tm),
        in_specs=attn_specs + [
            pl.BlockSpec((None, tm, D_MODEL), lambda b, i: (b, i, 0)),
            pl.BlockSpec((None, 6, D_MODEL), lambda b, i: (b, 0, 0)),
            pl.BlockSpec((1, D_MODEL), const),
            pl.BlockSpec((2 * N_EXPERTS, D_MODEL), const),
            pl.BlockSpec((N_EXPERTS, 1), const),
            pl.BlockSpec((tm, tm), const, pipeline_mode=pl.Buffered(1)),
            wspec, wspec, wdspec,
            pl.BlockSpec((1, D_MODEL), const),
        ],
        out_specs=pl.BlockSpec((None, tm, D_MODEL), lambda b, i: (b, i, 0)),
        scratch_shapes=[
            pltpu.VMEM((MOE_ROWS, D_MODEL + LANES), BF16),
            pltpu.VMEM((MOE_ROWS, D_MODEL), BF16),
        ],
        compiler_params=_cparams(("parallel", "parallel")),
        name="moe",
    )(*(attn or ()), x, mod_l, g_ffn, wr, bias, earlier, wg, wu, wd, g_final)


def _rope_tables(S):
    pos = jnp.arange(S, dtype=jnp.int32)
    rows = (pos // GRID_W).astype(F32)[None, :]
    cols = (pos % GRID_W).astype(F32)[None, :]
    tabs = []
    for half in (HEAD_DIM // 2, D_ROPE // 2):
        inv = (ROPE_THETA ** (-jnp.arange(0, half, 2, dtype=F32) / half))[:, None]
        for p in (rows, cols):
            ang = inv * p
            tabs += [jnp.cos(ang), jnp.sin(ang)]
    return jnp.concatenate(tabs, axis=0)


def _trunk(x, mod, p):
    B, S, _ = x.shape
    tab = _rope_tables(S)
    for l in range(DEPTH):
        mod_l = mod[l]
        g_mix = p["g_mix"][l][None, :]
        if l % 2 == 0:
            i = l // 2
            qt, k, vt = _pre_attn(x, mod_l, g_mix, tab, p["wint"][i], p["g_qa"][i][:, None],
                                  p["g_ka"][i][:, None], p["g_cq"][i][:, None], p["wuqt"][i],
                                  p["g_ckv"][i][:, None], p["wukvt"][i])
            attn = (_attention(qt, k, vt), p["wo"][i])
        else:
            attn = None
            x = _fourier_layer(x, mod_l, g_mix, p["wc"][l // 2])
        x = _moe(x, mod_l, p["g_ffn"][l][None, :], p["wr"], p["bias"], p["wg"][l], p["wu"][l],
                 p["wd"][l], p["g_final"], final=(l == DEPTH - 1), attn=attn)
    return x


def kernel(x_prompt, x_sample, c_prompt, c_sample, w_ada, b_ada, g_mix, g_ffn, w_in, g_qa, g_ka, g_cq,
           w_uq, g_ckv, w_ukv, w_o_attn, w_fourier, w_router, router_bias, w_gate, w_up, w_down, g_final):
    nbp = x_prompt.shape[0]
    mod = _modulation(jnp.concatenate([c_prompt, c_sample], axis=0), w_ada, b_ada)
    mod = mod.reshape(DEPTH, -1, 6, D_MODEL)
    wrt = w_router.T
    wr_hi = wrt.astype(BF16)
    wr_lo = (wrt - wr_hi.astype(F32)).astype(BF16)
    p = {
        "g_mix": g_mix, "g_ffn": g_ffn, "g_qa": g_qa, "g_ka": g_ka, "g_cq": g_cq, "g_ckv": g_ckv,
        "wint": jnp.swapaxes(w_in, 1, 2).astype(BF16),
        "wuqt": jnp.swapaxes(w_uq, 1, 2).astype(BF16),
        "wukvt": jnp.swapaxes(w_ukv, 1, 2).astype(BF16),
        "wo": w_o_attn.astype(BF16),
        "wc": w_fourier.astype(BF16),
        "wr": jnp.concatenate([wr_hi, wr_lo], axis=0),
        "bias": router_bias[:, None],
        "wg": w_gate.astype(BF16),
        "wu": w_up.astype(BF16),
        "wd": w_down.astype(BF16).reshape(DEPTH, N_EGROUPS, EXPERTS_PER_GROUP * D_EXPERT, D_MODEL),
        "g_final": g_final[None, :],
    }
    y_prompt = _trunk(x_prompt, mod[:, :nbp], p)
    y_sample = _trunk(x_sample, mod[:, nbp:], p)
    return (y_prompt, y_sample)
```
